```python
import math
import jax, jax.numpy as jnp
from jax import lax
import numpy as np

D_MODEL = 1024
BATCH = 16
SEQ = 2048
DEPTH = 4

PLE_DIM = 256
D_FF = 2816
RMS_EPS = 1e-6
N_BRANCH = 3
MIX_W = 512

HG_HEADS = 8
HG_DK = 64
HG_DV = 64
HG_CHUNK = 64
NSA_HEADS = 8
NSA_KV_HEADS = 2
NSA_HD = 64
CMP_LEN = 32
CMP_STRIDE = 16
CMP_HIDDEN = 128
SLC_BLOCK = 64
SLC_TOPN = 8
SLC_QBLOCK = 64
WIN = 256
WIN_QBLOCK = 128
RW_HEADS = 8
RW_HD = 64
RW_DECAY_LORA = 64
RW_A_LORA = 64
RW_GATE_LORA = 128
RW_GN_EPS = 64e-5
REL_BUCKETS = 32
REL_MAX_DIST = 128

HG_W = HG_HEADS * HG_DK
NSA_W = NSA_HEADS * NSA_HD
NSA_KV_W = NSA_KV_HEADS * NSA_HD
RW_W = RW_HEADS * RW_HD
RW_SIZES = (RW_W, RW_W, RW_W, RW_DECAY_LORA, RW_A_LORA, RW_GATE_LORA)
RW_COLS = 3 * RW_W + RW_DECAY_LORA + RW_A_LORA + RW_GATE_LORA
IN_SIZES = (HG_W, HG_W, HG_W, HG_W, NSA_W, NSA_KV_W, NSA_KV_W, NSA_KV_W, NSA_KV_W, NSA_KV_W, NSA_KV_W, 3 * NSA_HEADS, RW_COLS, N_BRANCH * D_MODEL)
IN_COLS = 4 * HG_W + NSA_W + 6 * NSA_KV_W + 3 * NSA_HEADS + RW_COLS + N_BRANCH * D_MODEL

kernel_name = "hybrid_hgrn2_nsa_rwkv7_macaron"


def _split(t, sizes):
    return jnp.split(t, np.cumsum(sizes)[:-1].tolist(), axis=-1)


def rmsnorm(x, g, eps=RMS_EPS):
    xf = x.astype(jnp.float32)
    y = xf * lax.rsqrt(jnp.mean(xf * xf, axis=-1, keepdims=True) + eps)
    return (y * g.astype(jnp.float32)).astype(x.dtype)


def swiglu(x, w_gu, w_d):
    gate, up = jnp.split(x @ w_gu, 2, axis=-1)
    return (jax.nn.silu(gate) * up) @ w_d


def t5_bucket(n):
    n = jnp.maximum(n, 0)
    max_exact = REL_BUCKETS // 2
    nf = jnp.maximum(n, 1).astype(jnp.float32)
    large = max_exact + (jnp.log(nf / max_exact) / math.log(REL_MAX_DIST / max_exact) * (REL_BUCKETS - max_exact)).astype(jnp.int32)
    large = jnp.minimum(large, REL_BUCKETS - 1)
    return jnp.where(n < max_exact, n, large)


def masked_softmax(logits, mask):
    logits = jnp.where(mask, logits.astype(jnp.float32), -jnp.inf)
    m = jnp.max(logits, axis=-1, keepdims=True)
    m = jnp.where(jnp.isfinite(m), m, 0.0)
    e = jnp.exp(logits - m)
    return e / jnp.maximum(jnp.sum(e, axis=-1, keepdims=True), 1e-30)


def hgrn2(q_raw, f_raw, i_raw, g_raw, lb, norm_g):
    B, S, _ = q_raw.shape
    H, C = HG_HEADS, HG_CHUNK
    N = S // C
    f32 = jnp.float32
    z = f_raw.astype(f32)
    lb = jnp.maximum(lb, 0.0)
    log_f = jnp.logaddexp(jax.nn.log_sigmoid(z), jnp.log(lb) + jax.nn.log_sigmoid(-z))
    k = (1.0 - lb) * jax.nn.sigmoid(-z)
    q = jax.nn.silu(q_raw.astype(f32))
    v = i_raw.astype(f32)

    def chunks(t, d):
        return t.reshape(B, N, C, H, d).transpose(1, 0, 3, 2, 4)

    causal = jnp.tril(jnp.ones((C, C), dtype=bool))[:, :, None]

    def step(state, inp):
        qc, kc, vc, lfc = inp
        b = jnp.cumsum(lfc, axis=2)
        decay = jnp.exp(jnp.where(causal, b[:, :, :, None, :] - b[:, :, None, :, :], -jnp.inf))
        scores = jnp.sum(qc[:, :, :, None, :] * kc[:, :, None, :, :] * decay, axis=-1)
        o = scores @ vc + jnp.einsum("bhtd,bhde->bhte", qc * jnp.exp(b), state)
        b_end = b[:, :, -1:, :]
        state = jnp.exp(b_end[:, :, 0, :, None]) * state + jnp.einsum("bhsd,bhse->bhde", kc * jnp.exp(b_end - b), vc)
        return state, o

    state0 = jnp.zeros((B, H, HG_DK, HG_DV), f32)
    _, o = lax.scan(step, state0, (chunks(q, HG_DK), chunks(k, HG_DK), chunks(v, HG_DV), chunks(log_f, HG_DK)))
    o = o.transpose(1, 0, 3, 2, 4).reshape(B, S, H, HG_DV)
    o = o * lax.rsqrt(jnp.mean(o * o, axis=-1, keepdims=True) + RMS_EPS) * norm_g.astype(f32).reshape(H, HG_DV)
    o = o.reshape(B, S, H * HG_DV) * jax.nn.silu(g_raw.astype(f32))
    return o.astype(q_raw.dtype)


def nsa(q_raw, k_cmp, v_cmp, k_slc, v_slc, k_win, v_win, gate_raw, pe, w1, w2, rel_bias):
    B, S, _ = q_raw.shape
    G, HPG, Dh = NSA_KV_HEADS, NSA_HEADS // NSA_KV_HEADS, NSA_HD
    q = q_raw.reshape(B, S, G, HPG, Dh) * (Dh ** -0.5)
    kv = lambda t: t.reshape(B, S, G, Dh)
    pos = jnp.arange(S)

    n_cmp = (S - CMP_LEN) // CMP_STRIDE + 1
    blk_idx = np.arange(n_cmp)[:, None] * CMP_STRIDE + np.arange(CMP_LEN)[None, :]

    def compress(t, pe_, w1_, w2_):
        blocks = t[:, blk_idx] + pe_[None, None, :, None, :]
        hid = jax.nn.silu(jnp.einsum("bnlgd,ldh->bngh", blocks, w1_.reshape(CMP_LEN, Dh, CMP_HIDDEN)))
        return hid @ w2_

    kc = compress(kv(k_cmp), pe[0], w1[0], w2[0])
    vc = compress(kv(v_cmp), pe[1], w1[1], w2[1])
    dist_c = pos[:, None] - jnp.asarray(blk_idx[:, -1])[None, :]
    bias_c = rel_bias[t5_bucket(dist_c)].reshape(S, n_cmp, G, HPG).transpose(2, 3, 0, 1)
    p_cmp = masked_softmax(jnp.einsum("bsghd,bngd->bghsn", q, kc) + bias_c, dist_c >= 0)
    o_cmp = jnp.einsum("bghsn,bngd->bsghd", p_cmp, vc.astype(jnp.float32))

    n_slc = S // SLC_BLOCK
    s_lo = np.arange(n_slc) * SLC_BLOCK
    cover = ((blk_idx[:, :1] <= (s_lo + SLC_BLOCK - 1)[None, :]) & (blk_idx[:, -1:] >= s_lo[None, :])).astype(np.float32)
    imp = jnp.einsum("bghsn,nm->bgsm", p_cmp, jnp.asarray(cover))
    cur = (pos // SLC_BLOCK)[:, None]
    blk = jnp.arange(n_slc)[None, :]
    forced = (blk == 0) | (blk == cur) | (blk == cur - 1)
    score = jnp.where(forced, jnp.inf, jnp.where(blk <= cur, imp, -jnp.inf))
    n_sel = min(SLC_TOPN, n_slc)
    _, sel = lax.top_k(score, n_sel)

    ks = kv(k_slc).reshape(B, n_slc, SLC_BLOCK, G, Dh).transpose(0, 3, 1, 2, 4)
    vs = kv(v_slc).reshape(B, n_slc, SLC_BLOCK, G, Dh).transpose(0, 3, 1, 2, 4)
    n_qb = S // SLC_QBLOCK
    q_b = q.reshape(B, n_qb, SLC_QBLOCK, G, HPG, Dh).transpose(1, 0, 2, 3, 4, 5)
    sel_b = sel.reshape(B, G, n_qb, SLC_QBLOCK, n_sel).transpose(2, 0, 1, 3, 4)
    qpos_b = pos.reshape(n_qb, SLC_QBLOCK)
    bi = jnp.arange(B)[:, None, None, None]
    gi = jnp.arange(G)[None, :, None, None]
    table_g = rel_bias.reshape(REL_BUCKETS, G, HPG).transpose(1, 0, 2)
    n_keys = n_sel * SLC_BLOCK

    def slc_block(args):
        qb, sb, qp = args
        kg = ks[bi, gi, sb].reshape(B, G, SLC_QBLOCK, n_keys, Dh)
        vg = vs[bi, gi, sb].reshape(B, G, SLC_QBLOCK, n_keys, Dh)
        kpos = (sb[..., None] * SLC_BLOCK + jnp.arange(SLC_BLOCK)).reshape(B, G, SLC_QBLOCK, n_keys)
        dist = qp[None, None, :, None] - kpos
        bias = table_g[gi, t5_bucket(dist)].transpose(0, 1, 4, 2, 3)
        logits = jnp.einsum("bqghd,bgqkd->bghqk", qb, kg) + bias
        pr = masked_softmax(logits, (dist >= 0)[:, :, None])
        return jnp.einsum("bghqk,bgqkd->bqghd", pr, vg.astype(jnp.float32))

    o_slc = lax.map(slc_block, (q_b, sel_b, qpos_b)).transpose(1, 0, 2, 3, 4, 5).reshape(B, S, G, HPG, Dh)

    nwb = S // WIN_QBLOCK
    nprev = WIN // WIN_QBLOCK
    kw_len = (nprev + 1) * WIN_QBLOCK

    def band(t):
        tb = jnp.pad(t.reshape(B, nwb, WIN_QBLOCK, G, Dh), ((0, 0), (nprev, 0), (0, 0), (0, 0), (0, 0)))
        return jnp.concatenate([tb[:, j:j + nwb] for j in range(nprev + 1)], axis=2)

    kwb, vwb = band(kv(k_win)), band(kv(v_win))
    ka = np.arange(kw_len)
    rel = nprev * WIN_QBLOCK + np.arange(WIN_QBLOCK)[:, None] - ka[None, :]
    kpos_w = (np.arange(nwb)[:, None] - nprev) * WIN_QBLOCK + ka[None, :]
    mask_w = ((rel >= 0) & (rel < WIN))[None] & (kpos_w >= 0)[:, None, :]
    bias_w = rel_bias[t5_bucket(jnp.asarray(rel))].reshape(WIN_QBLOCK, kw_len, G, HPG).transpose(2, 3, 0, 1)
    qw = q.reshape(B, nwb, WIN_QBLOCK, G, HPG, Dh)
    pw = masked_softmax(jnp.einsum("bnqghd,bnkgd->bnghqk", qw, kwb) + bias_w, jnp.asarray(mask_w)[None, :, None, None])
    o_win = jnp.einsum("bnghqk,bnkgd->bnqghd", pw, vwb.astype(jnp.float32)).reshape(B, S, G, HPG, Dh)

    g = jax.nn.sigmoid(gate_raw.astype(jnp.float32)).reshape(B, S, G, HPG, 3)
    o = g[..., 0:1] * o_cmp + g[..., 1:2] * o_slc + g[..., 2:3] * o_win
    return o.reshape(B, S, NSA_W).astype(q_raw.dtype)


def rwkv7(proj, mu, w0, wB, a0, aB, gB, k_k, k_a, r_k, ln_w, ln_b):
    B, S, _ = proj.shape
    H, N = RW_HEADS, RW_HD
    f32 = jnp.float32
    prev = jnp.pad(proj, ((0, 0), (1, 0), (0, 0)))[:, :-1]
    xm = proj + (prev - proj) * mu
    r, k, v, wl, al, gl = _split(xm, RW_SIZES)
    w = -jax.nn.softplus(-(w0 + jnp.tanh(wl) @ wB)) - 0.5
    decay = jnp.exp(-jnp.exp(w.astype(f32)))
    a = jax.nn.sigmoid(a0 + al @ aB)
    g = jax.nn.sigmoid(gl) @ gB
    hd = lambda t: t.astype(f32).reshape(B, S, H, N)
    kk = hd(k * k_k)
    kk = kk / jnp.maximum(jnp.sqrt(jnp.sum(kk * kk, axis=-1, keepdims=True)), 1e-12)
    k = k * (1.0 + (a - 1.0) * k_a)
    r_h, k_h, v_h, a_h, w_h = hd(r), hd(k), hd(v), hd(a), hd(decay)
    tm = lambda t: t.transpose(1, 0, 2, 3)

    def step(st, inp):
        r_t, w_t, k_t, v_t, kk_t, a_t = inp
        sa = jnp.einsum("bhvk,bhk->bhv", st, -kk_t)
        st = st * w_t[:, :, None, :] + sa[..., None] * (kk_t * a_t)[:, :, None, :] + v_t[..., None] * k_t[:, :, None, :]
        return st, jnp.einsum("bhvk,bhk->bhv", st, r_t)

    st0 = jnp.zeros((B, H, N, N), f32)
    _, y = lax.scan(step, st0, (tm(r_h), tm(w_h), tm(k_h), tm(v_h), tm(kk), tm(a_h)))
    y = y.transpose(1, 0, 2, 3)
    mean = jnp.mean(y, axis=-1, keepdims=True)
    var = jnp.mean(jnp.square(y - mean), axis=-1, keepdims=True)
    y = (y - mean) * lax.rsqrt(var + RW_GN_EPS) * ln_w.astype(f32).reshape(H, N) + ln_b.astype(f32).reshape(H, N)
    y = y + jnp.sum(r_h * k_h * r_k.astype(f32), axis=-1, keepdims=True) * v_h
    return (y.reshape(B, S, H * N) * g.astype(f32)).astype(proj.dtype)


def setup_inputs(seed: int = 0) -> dict:
    key = jax.random.key(seed)
    keys = iter(jax.random.split(key, 40))
    f32 = jnp.float32
    L, D = DEPTH, D_MODEL

    def nrm(shape, scale):
        return jax.random.normal(next(keys), shape, f32) * scale

    def gain(shape):
        return 1.0 + 0.02 * jax.random.normal(next(keys), shape, f32)

    return {
        "x": nrm((BATCH, SEQ, D), 1.0),
        "p": nrm((L, BATCH, SEQ, PLE_DIM), 1.0),
        "ffn1_norm": gain((L, D)),
        "ffn1_wgu": nrm((L, D, 2 * D_FF), D ** -0.5),
        "ffn1_wd": nrm((L, D_FF, D), D_FF ** -0.5),
        "mix_norm": gain((L, D)),
        "w_in": nrm((L, D, IN_COLS), D ** -0.5),
        "hg_lb": nrm((L, HG_W), 0.1),
        "hg_norm": gain((L, HG_W)),
        "cmp_pe": nrm((L, 2, CMP_LEN, NSA_HD), 0.1),
        "cmp_w1": nrm((L, 2, CMP_LEN * NSA_HD, CMP_HIDDEN), (CMP_LEN * NSA_HD) ** -0.5),
        "cmp_w2": nrm((L, 2, CMP_HIDDEN, NSA_HD), CMP_HIDDEN ** -0.5),
        "rel_bias": nrm((REL_BUCKETS, NSA_HEADS), 0.5),
        "rw_mu": jax.random.uniform(next(keys), (L, RW_COLS), f32),
        "rw_w0": nrm((L, RW_W), 0.5),
        "rw_wB": nrm((L, RW_DECAY_LORA, RW_W), RW_DECAY_LORA ** -0.5),
        "rw_a0": nrm((L, RW_W), 0.5),
        "rw_aB": nrm((L, RW_A_LORA, RW_W), RW_A_LORA ** -0.5),
        "rw_gB": nrm((L, RW_GATE_LORA, RW_W), RW_GATE_LORA ** -0.5),
        "rw_kk": 0.85 + nrm((L, RW_W), 0.1),
        "rw_ka": 1.0 + nrm((L, RW_W), 0.1),
        "rw_rk": nrm((L, RW_HEADS, RW_HD), 0.1),
        "rw_ln_w": gain((L, RW_W)),
        "rw_ln_b": nrm((L, RW_W), 0.02),
        "w_branch": nrm((L, N_BRANCH, MIX_W, D), MIX_W ** -0.5),
        "w_out": nrm((L, D, D), D ** -0.5),
        "ffn2_norm": gain((L, D)),
        "ffn2_wgu": nrm((L, D, 2 * D_FF), D ** -0.5),
        "ffn2_wd": nrm((L, D_FF, D), D_FF ** -0.5),
        "ple_norm": gain((L, D)),
        "ple_gate_w": nrm((L, D, D), D ** -0.5),
        "ple_w": nrm((L, PLE_DIM, D), PLE_DIM ** -0.5),
        "final_norm": gain((D,)),
    }


def reference(x, p, ffn1_norm, ffn1_wgu, ffn1_wd, mix_norm, w_in, hg_lb, hg_norm,
              cmp_pe, cmp_w1, cmp_w2, rel_bias, rw_mu, rw_w0, rw_wB, rw_a0, rw_aB, rw_gB,
              rw_kk, rw_ka, rw_rk, rw_ln_w, rw_ln_b, w_branch, w_out,
              ffn2_norm, ffn2_wgu, ffn2_wd, ple_norm, ple_gate_w, ple_w, final_norm):
    B, S, D = x.shape
    lb_w = jax.nn.softmax(hg_lb.astype(jnp.float32), axis=0)
    lower_bounds = jnp.cumsum(lb_w, axis=0) - lb_w[0]
    h = x
    for i in range(DEPTH):
        h = h + 0.5 * swiglu(rmsnorm(h, ffn1_norm[i]), ffn1_wgu[i], ffn1_wd[i])
        u = rmsnorm(h, mix_norm[i])
        (hq, hf, hi, hg, nq, kc, vc, ksl, vsl, kw, vw, ngate, rwp, mgate) = _split(u @ w_in[i], IN_SIZES)
        o_hg = hgrn2(hq, hf, hi, hg, lower_bounds[i], hg_norm[i])
        o_ns = nsa(nq, kc, vc, ksl, vsl, kw, vw, ngate, cmp_pe[i], cmp_w1[i], cmp_w2[i], rel_bias)
        o_rw = rwkv7(rwp, rw_mu[i], rw_w0[i], rw_wB[i], rw_a0[i], rw_aB[i], rw_gB[i],
                     rw_kk[i], rw_ka[i], rw_rk[i], rw_ln_w[i], rw_ln_b[i])
        gates = jax.nn.sigmoid(mgate).reshape(B, S, N_BRANCH, D)
        merged = (gates[:, :, 0] * (o_hg @ w_branch[i, 0])
                  + gates[:, :, 1] * (o_ns @ w_branch[i, 1])
                  + gates[:, :, 2] * (o_rw @ w_branch[i, 2]))
        h = h + merged @ w_out[i]
        h = h + 0.5 * swiglu(rmsnorm(h, ffn2_norm[i]), ffn2_wgu[i], ffn2_wd[i])
        ple_gate = jax.nn.sigmoid(rmsnorm(h, ple_norm[i]) @ ple_gate_w[i])
        h = h + ple_gate * (p[i] @ ple_w[i])
    return rmsnorm(h, final_norm)
```

```python
import functools
import math

import jax
import jax.numpy as jnp
import numpy as np
from jax import lax
from jax.experimental import pallas as pl
from jax.experimental.pallas import tpu as pltpu

F32 = jnp.float32
BF16 = jnp.bfloat16

RMS_EPS = 1e-6
LANES = 128
VMEM_LIMIT = 48 * 1024 * 1024

HG_W = 512
NSA_W = 512
RW_W = 512
D_MODEL = 1024
COL_MG = 0
COL_HG = 3072
COL_NSA = 5120
COL_RW = 6656
N_PROJ = 8704


def _cparams(sem):
    return pltpu.CompilerParams(dimension_semantics=sem, vmem_limit_bytes=VMEM_LIMIT)


def _rms(x, g):
    return x * lax.rsqrt(jnp.mean(x * x, axis=-1, keepdims=True) + RMS_EPS) * g


def _sigmoid(x):
    return 1.0 / (1.0 + jnp.exp(-x))


def _silu(x):
    return x * _sigmoid(x)


def _dot(a, b):
    return jnp.dot(a, b, preferred_element_type=F32)


def _ffn_kernel(h_ref, g_ref, wg_ref, wu_ref, wd_ref, o_ref, xn_ref, acc_ref):
    j = pl.program_id(1)

    @pl.when(j == 0)
    def _():
        xn_ref[...] = _rms(h_ref[...], g_ref[...]).astype(BF16)
        acc_ref[...] = jnp.zeros_like(acc_ref)

    xn = xn_ref[...]
    gate = _dot(xn, wg_ref[...])
    up = _dot(xn, wu_ref[...])
    act = (_silu(gate) * up).astype(BF16)
    acc_ref[...] += _dot(act, wd_ref[...])

    @pl.when(j == pl.num_programs(1) - 1)
    def _():
        o_ref[...] = h_ref[...] + 0.5 * acc_ref[...]


def _ffn(h, g, wgu, wd, tm, tf):
    T, D = h.shape
    FF = wd.shape[0]
    nf = FF // tf
    return pl.pallas_call(
        _ffn_kernel,
        out_shape=jax.ShapeDtypeStruct((T, D), F32),
        grid=(T // tm, nf),
        in_specs=[
            pl.BlockSpec((tm, D), lambda i, j: (i, 0)),
            pl.BlockSpec((1, D), lambda i, j: (0, 0)),
            pl.BlockSpec((D, tf), lambda i, j: (0, j)),
            pl.BlockSpec((D, tf), lambda i, j: (0, j + nf)),
            pl.BlockSpec((tf, D), lambda i, j: (j, 0)),
        ],
        out_specs=pl.BlockSpec((tm, D), lambda i, j: (i, 0)),
        scratch_shapes=[pltpu.VMEM((tm, D), BF16), pltpu.VMEM((tm, D), F32)],
        compiler_params=_cparams(("parallel", "arbitrary")),
        name="ffn",
    )(h, g, wgu, wgu, wd)


def _proj_kernel(h_ref, g_ref, w_ref, o_ref, xn_ref):
    @pl.when(pl.program_id(1) == 0)
    def _():
        xn_ref[...] = _rms(h_ref[...], g_ref[...]).astype(BF16)

    o_ref[...] = _dot(xn_ref[...], w_ref[...])


def _proj(h, g, w, tm, tn):
    T, D = h.shape
    N = w.shape[1]
    return pl.pallas_call(
        _proj_kernel,
        out_shape=jax.ShapeDtypeStruct((T, N), F32),
        grid=(T // tm, N // tn),
        in_specs=[
            pl.BlockSpec((tm, D), lambda i, j: (i, 0)),
            pl.BlockSpec((1, D), lambda i, j: (0, 0)),
            pl.BlockSpec((D, tn), lambda i, j: (0, j)),
        ],
        out_specs=pl.BlockSpec((tm, tn), lambda i, j: (i, j)),
        scratch_shapes=[pltpu.VMEM((tm, D), BF16)],
        compiler_params=_cparams(("parallel", "arbitrary")),
        name="in_proj",
    )(h, g, w)


def _merge_kernel(h_ref, m0_ref, m1_ref, m2_ref, a_ref, b_ref, c_ref, wb_ref, wo_ref, o_ref):
    merged = _sigmoid(m0_ref[...]) * _dot(a_ref[...].astype(BF16), wb_ref[0])
    merged += _sigmoid(m1_ref[...]) * _dot(b_ref[...].astype(BF16), wb_ref[1])
    merged += _sigmoid(m2_ref[...]) * _dot(c_ref[...].astype(BF16), wb_ref[2])
    o_ref[...] = h_ref[...] + _dot(merged.astype(BF16), wo_ref[...])


def _merge(h, proj, o_hg, o_ns, o_rw, wb, wo, tm):
    T, D = h.shape
    W = o_hg.shape[1]
    mg0 = COL_MG // D
    return pl.pallas_call(
        _merge_kernel,
        out_shape=jax.ShapeDtypeStruct((T, D), F32),
        grid=(T // tm,),
        in_specs=[
            pl.BlockSpec((tm, D), lambda i: (i, 0)),
            pl.BlockSpec((tm, D), lambda i: (i, mg0)),
            pl.BlockSpec((tm, D), lambda i: (i, mg0 + 1)),
            pl.BlockSpec((tm, D), lambda i: (i, mg0 + 2)),
            pl.BlockSpec((tm, W), lambda i: (i, 0)),
            pl.BlockSpec((tm, W), lambda i: (i, 0)),
            pl.BlockSpec((tm, W), lambda i: (i, 0)),
            pl.BlockSpec((3, W, D), lambda i: (0, 0, 0)),
            pl.BlockSpec((D, D), lambda i: (0, 0)),
        ],
        out_specs=pl.BlockSpec((tm, D), lambda i: (i, 0)),
        compiler_params=_cparams(("parallel",)),
        name="merge",
    )(h, proj, proj, proj, o_hg, o_ns, o_rw, wb, wo)


def _ple_kernel(h_ref, g_ref, wg_ref, p_ref, wp_ref, fg_ref, o_ref, *, final):
    h = h_ref[...]
    gate = _sigmoid(_dot(_rms(h, g_ref[...]).astype(BF16), wg_ref[...]))
    out = h + gate * _dot(p_ref[...].astype(BF16), wp_ref[...])
    if final:
        out = _rms(out, fg_ref[...])
    o_ref[...] = out


def _ple(h, g, wg, p, wp, fg, tm, final):
    T, D = h.shape
    P = p.shape[1]
    return pl.pallas_call(
        functools.partial(_ple_kernel, final=final),
        out_shape=jax.ShapeDtypeStruct((T, D), F32),
        grid=(T // tm,),
        in_specs=[
            pl.BlockSpec((tm, D), lambda i: (i, 0)),
            pl.BlockSpec((1, D), lambda i: (0, 0)),
            pl.BlockSpec((D, D), lambda i: (0, 0)),
            pl.BlockSpec((tm, P), lambda i: (i, 0)),
            pl.BlockSpec((P, D), lambda i: (0, 0)),
            pl.BlockSpec((1, D), lambda i: (0, 0)),
        ],
        out_specs=pl.BlockSpec((tm, D), lambda i: (i, 0)),
        compiler_params=_cparams(("parallel",)),
        name="ple",
    )(h, g, wg, p, wp, fg)


def _softplus(x):
    return jnp.maximum(x, 0.0) + jnp.log1p(jnp.exp(-jnp.abs(x)))


def _split2(x):
    hi = x.astype(BF16)
    lo = (x - hi.astype(F32)).astype(BF16)
    return hi, lo


def _split3(x):
    hi = x.astype(BF16)
    r1 = x - hi.astype(F32)
    mid = r1.astype(BF16)
    lo = (r1 - mid.astype(F32)).astype(BF16)
    return hi, mid, lo


def _dot3(a, b):
    ah, al = _split2(a)
    bh, bl = _split2(b)
    return _dot(ah, bh) + (_dot(ah, bl) + _dot(al, bh))


def _segsum(x, j):
    hi, lo = _split2(x)
    return _dot(hi, j) + _dot(lo, j)


def _cumsum_rows(tri, x):
    hi, mid, lo = _split3(x)
    return _dot(tri, hi) + (_dot(tri, mid) + _dot(tri, lo))


def _pair_stack(x, lo_mask):
    return jnp.concatenate([jnp.where(lo_mask, x, 0.0), jnp.where(lo_mask, 0.0, x)], axis=0)


def _consts():
    i512 = np.arange(512)
    j512 = (i512[:, None] // 64 == i512[None, :] // 64).astype(np.float32)
    i64 = np.arange(64)
    tri64 = (i64[:, None] >= i64[None, :]).astype(np.float32)
    i128 = np.arange(128)
    same = i128[:, None] // 16 == i128[None, :] // 16
    tri16 = (same & (i128[:, None] >= i128[None, :])).astype(np.float32)
    tot16 = same.astype(np.float32)
    return {"j512": jnp.asarray(j512, BF16), "tri64": jnp.asarray(tri64, BF16),
            "tri16": jnp.asarray(tri16, BF16), "tot16": jnp.asarray(tot16, BF16)}


RW_HD = 64
RW_CH = 64
RW_GN_EPS = 64e-5


def _rwkv_kernel(r_ref, k_ref, v_ref, l_ref, mu_ref, vec_ref, wb_ref, ab_ref, gb_ref, j_ref,
                 tri_ref, o_ref, carry_ref, st_ref, *, nch):
    C = RW_CH
    TC = nch * C
    W = r_ref.shape[1]
    npair = W // LANES

    @pl.when(pl.program_id(1) == 0)
    def _():
        carry_ref[...] = jnp.zeros_like(carry_ref)
        st_ref[...] = jnp.zeros_like(st_ref)

    row = lax.broadcasted_iota(jnp.int32, (TC, W), 0)

    def shift(x_ref, idx):
        x = x_ref[...]
        prev = jnp.where(row == 0, carry_ref[idx:idx + 1, :], pltpu.roll(x, 1, axis=0))
        carry_ref[idx:idx + 1, :] = x[TC - 1:TC, :]
        return x + (prev - x) * mu_ref[idx:idx + 1, :]

    xr = shift(r_ref, 0)
    xk = shift(k_ref, 1)
    xv = shift(v_ref, 2)
    xl = shift(l_ref, 3)
    w0, a0, k_k, k_a = (vec_ref[i:i + 1, :] for i in range(4))
    ln_w, ln_b, r_k = (vec_ref[i:i + 1, :] for i in range(4, 7))
    jmat = j_ref[...]

    wlal = xl[:, 0:LANES]
    w_pre = w0 + _dot(jnp.tanh(wlal).astype(BF16), wb_ref[...])
    a_pre = a0 + _dot(wlal.astype(BF16), ab_ref[...])
    gate = _dot(_sigmoid(xl[:, LANES:2 * LANES]).astype(BF16), gb_ref[...])
    logw = -jnp.exp(-_softplus(-w_pre) - 0.5)
    a = _sigmoid(a_pre)
    kkr = xk * k_k
    kk = kkr / jnp.maximum(jnp.sqrt(_segsum(kkr * kkr, jmat)), 1e-12)
    k2 = xk * (1.0 + (a - 1.0) * k_a)
    ka = kk * a

    lane = lax.broadcasted_iota(jnp.int32, (C, LANES), 1)
    trow = lax.broadcasted_iota(jnp.int32, (C, LANES), 0)
    lo_mask = lane < RW_HD
    scol = lane & (RW_HD - 1)
    strict = trow > scol
    incl = trow >= scol
    eye2 = (trow == scol).astype(F32)
    r128 = lax.broadcasted_iota(jnp.int32, (LANES, LANES), 0)
    c128 = lax.broadcasted_iota(jnp.int32, (LANES, LANES), 1)
    bd_mask = (r128 // RW_HD) == (c128 // RW_HD)
    diag_mask = r128 == c128
    tri = tri_ref[...]

    def bf(x):
        return x.astype(BF16)

    def stack(x):
        return _pair_stack(x, lo_mask)

    ys = []
    for c in range(nch):
        rs = slice(c * C, (c + 1) * C)
        lw = logw[rs]
        b = _cumsum_rows(tri, lw)
        bend = b[C - 1:C, :]
        eb = jnp.exp(b)
        enb = jnp.exp(-b)
        egc = jnp.exp(bend - b)
        g_end = jnp.exp(bend)
        rt_all = xr[rs] * eb
        kt_all = k2[rs] * enb
        at_all = ka[rs] * enb
        bt_all = kk[rs] * jnp.exp(b - lw)
        kg_all = k2[rs] * egc
        ag_all = ka[rs] * egc
        v_all = xv[rs]
        yp = []
        for p in range(npair):
            ls = slice(p * LANES, (p + 1) * LANES)
            rt, kt, at, bt, kg, ag, vv = (t[:, ls] for t in (rt_all, kt_all, at_all, bt_all, kg_all, ag_all, v_all))
            lhs = bf(jnp.concatenate([bt, rt], axis=0))
            rhs = bf(jnp.concatenate([stack(at), stack(kt)], axis=0))
            gm = lax.dot_general(lhs, rhs, (((1,), (1,)), ((), ())), preferred_element_type=F32)
            a_ba = jnp.where(strict, gm[0:C, 0:LANES], 0.0)
            a_bk = jnp.where(strict, gm[0:C, LANES:], 0.0)
            a_ra = jnp.where(incl, gm[C:, 0:LANES], 0.0)
            a_rk = jnp.where(incl, gm[C:, LANES:], 0.0)
            pw = -a_ba
            ti = eye2 + pw
            for _ in range(int(math.log2(C)) - 1):
                pw = _dot(bf(pw), bf(stack(pw)))
                ti = ti + _dot(bf(ti), bf(stack(pw)))
            tib = bf(ti)
            wm = _dot(tib, bf(stack(bt)))
            av = _dot(bf(a_bk), bf(stack(vv)))
            u0 = _dot(tib, bf(stack(av)))
            y0 = _dot(bf(jnp.concatenate([a_rk, a_ra], axis=1)),
                      bf(jnp.concatenate([stack(vv), -stack(u0)], axis=0)))
            rw = rt - _dot(bf(a_ra), bf(stack(wm)))
            m2 = jnp.where(diag_mask, jnp.broadcast_to(g_end[:, ls], (LANES, LANES)), 0.0) - jnp.where(
                bd_mask, _dot(bf(ag.T), bf(wm)), 0.0)
            n2 = jnp.where(bd_mask, _dot(bf(jnp.concatenate([kg, ag], axis=0).T),
                                         bf(jnp.concatenate([vv, -u0], axis=0))), 0.0)
            s2 = st_ref[p]
            yp.append(y0 + _dot(bf(rw), bf(s2)))
            st_ref[p] = _dot3(m2, s2) + n2
        ys.append(jnp.concatenate(yp, axis=1))
    y = jnp.concatenate(ys, axis=0) if nch > 1 else ys[0]

    inv_n = 1.0 / RW_HD
    mean = _segsum(y, jmat) * inv_n
    yc = y - mean
    var = _segsum(yc * yc, jmat) * inv_n
    yn = yc * lax.rsqrt(var + RW_GN_EPS) * ln_w + ln_b
    bonus = _segsum(xr * k2 * r_k, jmat) * xv
    o_ref[...] = (yn + bonus) * gate


def _rwkv(proj, mu, vec, wb, ab, gb, jmat, tri, B, S, nch):
    T = proj.shape[0]
    W = RW_W
    TC = nch * RW_CH
    nblk = S // TC
    c0 = COL_RW // W
    row_map = lambda col: (lambda b, i: (b * nblk + i, col))
    const = lambda b, i: (0, 0)
    return pl.pallas_call(
        functools.partial(_rwkv_kernel, nch=nch),
        out_shape=jax.ShapeDtypeStruct((T, W), F32),
        grid=(B, nblk),
        in_specs=[
            pl.BlockSpec((TC, W), row_map(c0)),
            pl.BlockSpec((TC, W), row_map(c0 + 1)),
            pl.BlockSpec((TC, W), row_map(c0 + 2)),
            pl.BlockSpec((TC, W), row_map(c0 + 3)),
            pl.BlockSpec(mu.shape, const),
            pl.BlockSpec(vec.shape, const),
            pl.BlockSpec(wb.shape, const),
            pl.BlockSpec(ab.shape, const),
            pl.BlockSpec(gb.shape, const),
            pl.BlockSpec(jmat.shape, const),
            pl.BlockSpec(tri.shape, const),
        ],
        out_specs=pl.BlockSpec((TC, W), lambda b, i: (b * nblk + i, 0)),
        scratch_shapes=[pltpu.VMEM((8, W), F32), pltpu.VMEM((W // LANES, LANES, LANES), F32)],
        compiler_params=_cparams(("parallel", "arbitrary")),
        name="rwkv7",
    )(proj, proj, proj, proj, mu, vec, wb, ab, gb, jmat, tri)


def _rwkv_params(mu, w0, wB, a0, aB, gB, k_k, k_a, r_k, ln_w, ln_b):
    W = RW_W
    mu4 = jnp.stack([mu[0:W], mu[W:2 * W], mu[2 * W:3 * W], jnp.pad(mu[3 * W:], (0, W - (mu.shape[0] - 3 * W)))])
    vec = jnp.stack([w0, a0, k_k, k_a, ln_w, ln_b, r_k.reshape(-1), jnp.zeros_like(w0)])
    wb = jnp.pad(wB, ((0, LANES - wB.shape[0]), (0, 0))).astype(BF16)
    ab = jnp.pad(aB, ((LANES - aB.shape[0], 0), (0, 0))).astype(BF16)
    return mu4, vec, wb, ab, gB.astype(BF16)


HG_HD = 64
HG_SUB = 16


def _hgrn_kernel(q_ref, f_ref, i_ref, g_ref, lbp_ref, ng_ref, j_ref, tri_ref, tot_ref, o_ref, st_ref,
                 *, layer):
    TC, W = q_ref.shape
    npair = W // LANES
    nsub = TC // HG_SUB

    @pl.when(pl.program_id(1) == 0)
    def _():
        st_ref[...] = jnp.zeros_like(st_ref)

    z = f_ref[...]
    log_f = -_softplus(-z)
    k = _sigmoid(-z)
    if layer > 0:
        lbp = lbp_ref[...]
        e = jnp.exp(lbp - jnp.max(lbp, axis=0, keepdims=True))
        sm = e / jnp.sum(e, axis=0, keepdims=True)
        lb = sm[1:2, :]
        for j in range(2, layer + 1):
            lb = lb + sm[j:j + 1, :]
        lb = jnp.maximum(lb, 0.0)
        t2 = jnp.log(lb) - _softplus(z)
        log_f = jnp.maximum(log_f, t2) + jnp.log1p(jnp.exp(-jnp.abs(log_f - t2)))
        k = (1.0 - lb) * k
    q = _silu(q_ref[...])
    v = i_ref[...]
    b = _cumsum_rows(tri_ref[...], log_f)
    bend = _cumsum_rows(tot_ref[...], log_f)
    qe = q * jnp.exp(b)
    kg = k * jnp.exp(bend - b)
    jmat = j_ref[...]
    j128 = jmat[0:LANES, 0:LANES]

    rowb = lax.broadcasted_iota(jnp.int32, (TC, LANES), 0)
    trow = lax.broadcasted_iota(jnp.int32, (HG_SUB, W), 0)
    r128 = lax.broadcasted_iota(jnp.int32, (LANES, LANES), 0)
    c128 = lax.broadcasted_iota(jnp.int32, (LANES, LANES), 1)
    bd_mask = (r128 // HG_HD) == (c128 // HG_HD)

    def bf(x):
        return x.astype(BF16)

    vts = [bf(v[:, p * LANES:(p + 1) * LANES].T) for p in range(npair)]
    outs = []
    for i in range(nsub):
        rs = slice(i * HG_SUB, (i + 1) * HG_SUB)
        b_i, q_i, k_i, v_i = b[rs], q[rs], k[rs], v[rs]
        xs = []
        for s in range(HG_SUB):
            dec = jnp.exp(jnp.minimum(b_i - b_i[s:s + 1, :], 0.0))
            xs.append(jnp.where(trow >= s, q_i * (k_i[s:s + 1, :] * dec), 0.0))
        x = bf(jnp.concatenate(xs, axis=0))
        g_end = jnp.exp(bend[i * HG_SUB:i * HG_SUB + 1, :])
        op = []
        for p in range(npair):
            ls = slice(p * LANES, (p + 1) * LANES)
            pm = _dot(x[:, ls], j128)
            od = pm[0:HG_SUB] * v_i[0:1, ls]
            for s in range(1, HG_SUB):
                od = od + pm[s * HG_SUB:(s + 1) * HG_SUB] * v_i[s:s + 1, ls]
            st = st_ref[p]
            oi = lax.dot_general(bf(qe[rs, ls]), bf(st), (((1,), (1,)), ((), ())), preferred_element_type=F32)
            kgm = jnp.where((rowb >= i * HG_SUB) & (rowb < (i + 1) * HG_SUB), kg[:, ls], 0.0)
            st_ref[p] = st * g_end[:, ls] + jnp.where(bd_mask, _dot(vts[p], bf(kgm)), 0.0)
            op.append(od + oi)
        outs.append(jnp.concatenate(op, axis=1))
    o = jnp.concatenate(outs, axis=0)
    ms = _segsum(o * o, jmat) * (1.0 / HG_HD)
    o_ref[...] = o * lax.rsqrt(ms + RMS_EPS) * ng_ref[...] * _silu(g_ref[...])


def _hgrn(proj, lbp, ng, jmat, tri, tot, B, S, layer, tc):
    T = proj.shape[0]
    W = HG_W
    nblk = S // tc
    c0 = COL_HG // W
    row_map = lambda col: (lambda b, i: (b * nblk + i, col))
    const = lambda b, i: (0, 0)
    return pl.pallas_call(
        functools.partial(_hgrn_kernel, layer=layer),
        out_shape=jax.ShapeDtypeStruct((T, W), F32),
        grid=(B, nblk),
        in_specs=[
            pl.BlockSpec((tc, W), row_map(c0)),
            pl.BlockSpec((tc, W), row_map(c0 + 1)),
            pl.BlockSpec((tc, W), row_map(c0 + 2)),
            pl.BlockSpec((tc, W), row_map(c0 + 3)),
            pl.BlockSpec(lbp.shape, const),
            pl.BlockSpec(ng.shape, const),
            pl.BlockSpec(jmat.shape, const),
            pl.BlockSpec(tri.shape, const),
            pl.BlockSpec(tot.shape, const),
        ],
        out_specs=pl.BlockSpec((tc, W), lambda b, i: (b * nblk + i, 0)),
        scratch_shapes=[pltpu.VMEM((W // LANES, LANES, LANES), F32)],
        compiler_params=_cparams(("parallel", "arbitrary")),
        name="hgrn2",
    )(proj, proj, proj, proj, lbp, ng, jmat, tri, tot)


NSA_HD = 64
NSA_TQ = 128
CMP_STRIDE = 16
CMP_LEN = 32
SLC_BLOCK = 64
SLC_TOPN = 8
NEG_INF = float("-inf")


def _nt(a, b):
    return lax.dot_general(a, b, (((1,), (1,)), ((), ())), preferred_element_type=F32)


def _nsa_kernel(q_ref, kv1_ref, kv2_ref, gt_ref, w1_ref, pe_ref, w2_ref, biasc_ref, tb_ref, c31_ref,
                cov_ref, exp_ref, eg_ref, wm_ref, o_ref, cmp_ref, kv_ref, msk_ref, m_ref, l_ref, acc_ref, cx_ref):
    TQ = NSA_TQ
    S = kv1_ref.shape[0]
    nkc = S // TQ
    ncmp = S // CMP_STRIDE
    i = pl.program_id(1)
    lane = lax.broadcasted_iota(jnp.int32, (1, LANES), 1)
    lo = lane < NSA_HD

    def pad4(x):
        xs = pltpu.roll(x, NSA_HD, axis=1)
        return (jnp.where(lo, x, 0.0), jnp.where(lo, 0.0, xs), jnp.where(lo, xs, 0.0), jnp.where(lo, 0.0, x))

    @pl.when(i == 0)
    def _():
        for src, (ref, c0) in enumerate(((kv1_ref, 2), (kv1_ref, 3), (kv2_ref, 0), (kv2_ref, 1))):
            for ge, xv in enumerate(pad4(ref[:, c0 * LANES:(c0 + 1) * LANES])):
                kv_ref[src, ge] = xv.astype(BF16)
        for t in range(2):
            cx_ref[...] = kv1_ref[:, t * LANES:(t + 1) * LANES]
            acc_a = jnp.zeros((ncmp, 2 * LANES), F32)
            acc_b = jnp.zeros((ncmp, 2 * LANES), F32)
            for l in range(CMP_STRIDE):
                xl = cx_ref[pl.ds(l, ncmp, stride=CMP_STRIDE), :]
                acc_a += _dot((xl + pe_ref[t, l:l + 1, :]).astype(BF16), w1_ref[t, l])
                acc_b += _dot((xl + pe_ref[t, CMP_STRIDE + l:CMP_STRIDE + l + 1, :]).astype(BF16),
                              w1_ref[t, CMP_STRIDE + l])
            hid = _silu(acc_a + pltpu.roll(acc_b, ncmp - 1, axis=0))
            for ge, xv in enumerate(pad4(_dot(hid.astype(BF16), w2_ref[t]))):
                cmp_ref[t, ge] = xv.astype(BF16)

    qs = (q_ref[...] * (NSA_HD ** -0.5)).astype(BF16)
    lhs = [jnp.concatenate([qs[:, 2 * g * LANES:(2 * g + 1) * LANES],
                            qs[:, (2 * g + 1) * LANES:(2 * g + 2) * LANES]], axis=0) for g in range(2)]

    o_cmp = []
    for g in range(2):
        og = None
        psum = None
        for e in range(2):
            ge = 2 * g + e
            s = _nt(lhs[g], cmp_ref[0, ge]) + biasc_ref[ge, 0]
            mx = jnp.max(s, axis=1, keepdims=True)
            mx = jnp.where(mx == NEG_INF, 0.0, mx)
            ex = jnp.exp(s - mx)
            p = ex / jnp.maximum(jnp.sum(ex, axis=1, keepdims=True), 1e-30)
            pv = _dot(p.astype(BF16), cmp_ref[1, ge])
            og = pv if og is None else og + pv
            ph = p[0:TQ] + p[TQ:]
            psum = ph if psum is None else psum + ph
        o_cmp.append(og)
        hi, lw = _split2(psum)
        imp_t = (_nt(cov_ref[...], hi) + _nt(cov_ref[...], lw))[0:32]
        blk = lax.broadcasted_iota(jnp.int32, (32, TQ), 0)
        cur = (i * TQ + lax.broadcasted_iota(jnp.int32, (32, TQ), 1)) // SLC_BLOCK
        forced = (blk == 0) | (blk == cur) | (blk == cur - 1)
        score = jnp.where(forced, jnp.inf, jnp.where(blk <= cur, imp_t, NEG_INF))
        cnt = jnp.zeros((32, TQ), F32)
        for mp in range(32):
            sm = score[mp:mp + 1, :]
            ahead = (sm > score) | ((sm == score) & (blk > mp))
            cnt = cnt + jnp.where(ahead, 1.0, 0.0)
        sel_t = jnp.where(cnt < SLC_TOPN, 1.0, 0.0)
        sel = jnp.concatenate([sel_t, jnp.zeros((LANES - 32, TQ), F32)], axis=0).T
        full = _dot(sel.astype(BF16), exp_ref[...])
        for c in range(nkc):
            msk_ref[g, c] = jnp.where(full[:, c * TQ:(c + 1) * TQ] > 0.5, 0.0, NEG_INF)

    def reset():
        m_ref[...] = jnp.full_like(m_ref, NEG_INF)
        l_ref[...] = jnp.zeros_like(l_ref)
        acc_ref[...] = jnp.zeros_like(acc_ref)

    def step(lhs_g, src, ge, kc, bias):
        rows = pl.ds(pl.multiple_of(kc * TQ, TQ), TQ)
        s = _nt(lhs_g, kv_ref[src, ge, rows, :]) + bias
        m_prev = m_ref[...]
        m_new = jnp.maximum(m_prev, jnp.max(s, axis=1, keepdims=True))
        m_safe = jnp.where(m_new == NEG_INF, 0.0, m_new)
        alpha = jnp.exp(m_prev - m_safe)
        p = jnp.exp(s - m_safe)
        l_ref[...] = alpha * l_ref[...] + jnp.sum(p, axis=1, keepdims=True)
        acc_ref[...] = alpha * acc_ref[...] + _dot(p.astype(BF16), kv_ref[src + 1, ge, rows, :])
        m_ref[...] = m_new

    def result():
        return acc_ref[...] / jnp.maximum(l_ref[...], 1e-30)

    def mask2(g, kc):
        mk = msk_ref[g, kc]
        return jnp.concatenate([mk, mk], axis=0)

    o_slc, o_win = [], []
    for g in range(2):
        og_s = None
        og_w = None
        for e in range(2):
            ge = 2 * g + e
            reset()

            def far(kc, carry, g=g, ge=ge):
                step(lhs[g], 0, ge, kc, c31_ref[ge] + mask2(g, kc))
                return carry

            lax.fori_loop(0, jnp.maximum(i - 1, 0), far, 0)

            @pl.when(i >= 1)
            def _(g=g, ge=ge):
                step(lhs[g], 0, ge, i - 1, tb_ref[1, ge] + mask2(g, i - 1))

            step(lhs[g], 0, ge, i, tb_ref[0, ge] + mask2(g, i))
            r = result()
            og_s = r if og_s is None else og_s + r
            reset()

            @pl.when(i >= 2)
            def _(g=g, ge=ge):
                step(lhs[g], 2, ge, i - 2, c31_ref[ge] + wm_ref[...])

            @pl.when(i >= 1)
            def _(g=g, ge=ge):
                step(lhs[g], 2, ge, i - 1, tb_ref[1, ge])

            step(lhs[g], 2, ge, i, tb_ref[0, ge])
            r = result()
            og_w = r if og_w is None else og_w + r
        o_slc.append(og_s)
        o_win.append(og_w)

    def lanes(parts):
        return jnp.concatenate([parts[0][0:TQ], parts[0][TQ:], parts[1][0:TQ], parts[1][TQ:]], axis=1)

    sig_hi, sig_lo = _split2(_sigmoid(gt_ref[...]))
    out = None
    for j, parts in enumerate((o_cmp, o_slc, o_win)):
        gj = _dot(sig_hi, eg_ref[j]) + _dot(sig_lo, eg_ref[j])
        term = gj * lanes(parts)
        out = term if out is None else out + term
    o_ref[...] = out


def _nsa(proj, w1b, peb, w2b, tabs, B, S):
    T = proj.shape[0]
    W = NSA_W
    TQ = NSA_TQ
    nq = S // TQ
    c0 = COL_NSA // W
    cg = (COL_NSA + 2 * W + 2 * LANES) // LANES
    biasc, tb, c31, cov, expm, eg, wm = tabs

    def const(shape):
        return pl.BlockSpec(shape, lambda b, i, _n=len(shape): (0,) * _n)

    return pl.pallas_call(
        _nsa_kernel,
        out_shape=jax.ShapeDtypeStruct((T, W), F32),
        grid=(B, nq),
        in_specs=[
            pl.BlockSpec((TQ, W), lambda b, i: (b * nq + i, c0)),
            pl.BlockSpec((S, W), lambda b, i: (b, c0 + 1)),
            pl.BlockSpec((S, W), lambda b, i: (b, c0 + 2)),
            pl.BlockSpec((TQ, LANES), lambda b, i: (b * nq + i, cg)),
            const(w1b.shape), const(peb.shape), const(w2b.shape),
            pl.BlockSpec((4, 1, 2 * TQ, LANES), lambda b, i: (0, i, 0, 0)),
            const(tb.shape), const(c31.shape), const(cov.shape), const(expm.shape), const(eg.shape),
            const(wm.shape),
        ],
        out_specs=pl.BlockSpec((TQ, W), lambda b, i: (b * nq + i, 0)),
        scratch_shapes=[
            pltpu.VMEM((2, 4, S // CMP_STRIDE, LANES), BF16),
            pltpu.VMEM((4, 4, S, LANES), BF16),
            pltpu.VMEM((2, S // TQ, TQ, LANES), F32),
            pltpu.VMEM((2 * TQ, LANES), F32),
            pltpu.VMEM((2 * TQ, LANES), F32),
            pltpu.VMEM((2 * TQ, LANES), F32),
            pltpu.VMEM((S, LANES), F32),
        ],
        compiler_params=_cparams(("parallel", "arbitrary")),
        name="nsa",
    )(proj, proj, proj, proj, w1b, peb, w2b, biasc, tb, c31, cov, expm, eg, wm)


def _blockdiag2(w):
    z = jnp.zeros_like(w)
    return jnp.concatenate([jnp.concatenate([w, z], axis=-1), jnp.concatenate([z, w], axis=-1)], axis=-2)


def _nsa_params(pe, w1, w2):
    L, Dh = pe.shape[1], pe.shape[2]
    w1b = _blockdiag2(w1.reshape(2, L, Dh, -1)).astype(BF16)
    w2b = _blockdiag2(w2).astype(BF16)
    peb = jnp.concatenate([pe, pe], axis=-1)
    return w1b, peb, w2b


def _t5_bucket_np(n):
    n = np.maximum(n, 0)
    nf = np.maximum(n, 1).astype(np.float64)
    large = 16 + (np.log(nf / 16.0) / math.log(128 / 16) * 16).astype(np.int64)
    return np.where(n < 16, n, np.minimum(large, 31))


def _nsa_tables(rel_bias, S):
    TQ = NSA_TQ
    nq = S // TQ
    heads = np.array([[4 * g + e, 4 * g + 2 + e] for g in range(2) for e in range(2)])
    rb = rel_bias.astype(F32)

    def stacked(bucket, valid):
        vals = rb[jnp.asarray(bucket)]
        vals = jnp.where(jnp.asarray(valid)[..., None], vals, NEG_INF)
        vals = jnp.moveaxis(vals, -1, 0)[jnp.asarray(heads)]
        return jnp.concatenate([vals[:, 0], vals[:, 1]], axis=-2)

    pos = np.arange(S)
    ncmp = S // CMP_STRIDE
    end = np.arange(ncmp) * CMP_STRIDE + CMP_LEN - 1
    dist_c = pos[:, None] - end[None, :]
    valid_c = (dist_c >= 0) & (np.arange(ncmp) < ncmp - 1)[None, :]
    biasc = stacked(_t5_bucket_np(dist_c).reshape(nq, TQ, ncmp), valid_c.reshape(nq, TQ, ncmp))
    ii = np.arange(TQ)
    d0 = ii[:, None] - ii[None, :]
    tb = jnp.stack([stacked(_t5_bucket_np(d0), d0 >= 0), stacked(_t5_bucket_np(d0 + TQ), np.ones_like(d0, bool))])
    c31 = stacked(np.full((TQ, LANES), 31), np.ones((TQ, LANES), bool))
    nslc = S // SLC_BLOCK
    s_lo = np.arange(nslc) * SLC_BLOCK
    start = np.arange(ncmp) * CMP_STRIDE
    cover = ((start[:, None] <= (s_lo + SLC_BLOCK - 1)[None, :]) & (end[:, None] >= s_lo[None, :]))
    cover &= (np.arange(ncmp) < ncmp - 1)[:, None]
    cov = np.zeros((LANES, ncmp), np.float32)
    cov[:nslc] = cover.T
    expm = np.zeros((LANES, S), np.float32)
    expm[:nslc] = (np.arange(S)[None, :] // SLC_BLOCK == np.arange(nslc)[:, None])
    eg = np.zeros((3, LANES, NSA_W), np.float32)
    for j in range(3):
        eg[j, (np.arange(NSA_W) // NSA_HD) * 3 + j, np.arange(NSA_W)] = 1.0
    wm = np.where(ii[None, :] > ii[:, None], 0.0, -np.inf).astype(np.float32)
    wm = np.concatenate([wm, wm], axis=0)
    return (biasc, tb, c31, jnp.asarray(cov, BF16), jnp.asarray(expm, BF16), jnp.asarray(eg, BF16),
            jnp.asarray(wm))


def _tiles(T, n_ff):
    tm = 512 if T % 512 == 0 else T
    tf = n_ff // 2 if (n_ff // 2) % LANES == 0 else n_ff
    tn = N_PROJ // 4
    return tm, tf, tn


def _pack_w_in(w):
    hg_end = 4 * HG_W
    nsa_end = hg_end + NSA_W + 6 * 2 * NSA_HD + 3 * 8
    rw_end = nsa_end + 3 * RW_W + 256
    d = w.shape[0]
    parts = [w[:, rw_end:], w[:, :hg_end], w[:, hg_end:nsa_end],
             jnp.zeros((d, COL_RW - COL_NSA - (nsa_end - hg_end)), w.dtype), w[:, nsa_end:rw_end],
             jnp.zeros((d, N_PROJ - COL_RW - (rw_end - nsa_end)), w.dtype)]
    return jnp.concatenate(parts, axis=1).astype(BF16)


def kernel(x, p, ffn1_norm, ffn1_wgu, ffn1_wd, mix_norm, w_in, hg_lb, hg_norm, cmp_pe, cmp_w1, cmp_w2, rel_bias, rw_mu, rw_w0, rw_wB, rw_a0, rw_aB, rw_gB, rw_kk, rw_ka, rw_rk, rw_ln_w, rw_ln_b, w_branch, w_out, ffn2_norm, ffn2_wgu, ffn2_wd, ple_norm, ple_gate_w, ple_w, final_norm):
    B, S, D = x.shape
    depth = ffn1_norm.shape[0]
    T = B * S
    assert D == D_MODEL and S // CMP_STRIDE == LANES and S % NSA_TQ == 0
    assert w_in.shape[2] - 3 * D_MODEL == 4 * HG_W + NSA_W + 12 * NSA_HD + 24 + 3 * RW_W + 256
    tm, tf, tn = _tiles(T, ffn1_wd.shape[1])
    consts = _consts()
    tabs = _nsa_tables(rel_bias, S)
    row = lambda v: v.reshape(1, -1)
    h = x.reshape(T, D)
    for i in range(depth):
        h = _ffn(h, row(ffn1_norm[i]), ffn1_wgu[i].astype(BF16), ffn1_wd[i].astype(BF16), tm, tf)
        proj = _proj(h, row(mix_norm[i]), _pack_w_in(w_in[i]), tm, tn)
        o_hg = _hgrn(proj, hg_lb, row(hg_norm[i]), consts["j512"], consts["tri16"], consts["tot16"], B, S, i, 128)
        o_ns = _nsa(proj, *_nsa_params(cmp_pe[i], cmp_w1[i], cmp_w2[i]), tabs, B, S)
        o_rw = _rwkv(proj, *_rwkv_params(rw_mu[i], rw_w0[i], rw_wB[i], rw_a0[i], rw_aB[i], rw_gB[i], rw_kk[i],
                                         rw_ka[i], rw_rk[i], rw_ln_w[i], rw_ln_b[i]),
                     consts["j512"], consts["tri64"], B, S, 2)
        h = _merge(h, proj, o_hg, o_ns, o_rw, w_branch[i].astype(BF16), w_out[i].astype(BF16), tm)
        h = _ffn(h, row(ffn2_norm[i]), ffn2_wgu[i].astype(BF16), ffn2_wd[i].astype(BF16), tm, tf)
        h = _ple(h, row(ple_norm[i]), ple_gate_w[i].astype(BF16), p[i].reshape(T, -1), ple_w[i].astype(BF16),
                 row(final_norm), tm, i == depth - 1)
    return h.reshape(B, S, D)
```

```python
import functools
import math

import jax
import jax.numpy as jnp
import numpy as np
from jax import lax
from jax.experimental import pallas as pl
from jax.experimental.pallas import tpu as pltpu

F32 = jnp.float32
BF16 = jnp.bfloat16

RMS_EPS = 1e-6
LANES = 128
VMEM_LIMIT = 48 * 1024 * 1024

HG_W = 512
NSA_W = 512
RW_W = 512
D_MODEL = 1024
COL_MG = 0
COL_HG = 3072
COL_NSA = 5120
COL_RW = 6656
N_PROJ = 8704


def _cparams(sem):
    return pltpu.CompilerParams(dimension_semantics=sem, vmem_limit_bytes=VMEM_LIMIT)


def _rms(x, g):
    return x * lax.rsqrt(jnp.mean(x * x, axis=-1, keepdims=True) + RMS_EPS) * g


def _sigmoid(x):
    return 1.0 / (1.0 + jnp.exp(-x))


def _silu(x):
    return x * _sigmoid(x)


def _dot(a, b):
    return jnp.dot(a, b, preferred_element_type=F32)


def _nt(a, b):
    return lax.dot_general(a, b, (((1,), (1,)), ((), ())), preferred_element_type=F32)


def _ffn_kernel(h_ref, g_ref, wg_ref, wu_ref, wd_ref, o_ref, xn_ref, acc_ref):
    j = pl.program_id(1)

    @pl.when(j == 0)
    def _():
        xn_ref[...] = _rms(h_ref[...], g_ref[...]).astype(BF16)
        acc_ref[...] = jnp.zeros_like(acc_ref)

    xn = xn_ref[...]
    gate = _dot(xn, wg_ref[...])
    up = _dot(xn, wu_ref[...])
    act = (_silu(gate) * up).astype(BF16)
    acc_ref[...] += _dot(act, wd_ref[...])

    @pl.when(j == pl.num_programs(1) - 1)
    def _():
        o_ref[...] = h_ref[...] + 0.5 * acc_ref[...]


def _ffn(h, g, wgu, wd, tm, tf):
    T, D = h.shape
    FF = wd.shape[0]
    nf = FF // tf
    return pl.pallas_call(
        _ffn_kernel,
        out_shape=jax.ShapeDtypeStruct((T, D), F32),
        grid=(T // tm, nf),
        in_specs=[
            pl.BlockSpec((tm, D), lambda i, j: (i, 0)),
            pl.BlockSpec((1, D), lambda i, j: (0, 0)),
            pl.BlockSpec((D, tf), lambda i, j: (0, j)),
            pl.BlockSpec((D, tf), lambda i, j: (0, j + nf)),
            pl.BlockSpec((tf, D), lambda i, j: (j, 0)),
        ],
        out_specs=pl.BlockSpec((tm, D), lambda i, j: (i, 0)),
        scratch_shapes=[pltpu.VMEM((tm, D), BF16), pltpu.VMEM((tm, D), F32)],
        compiler_params=_cparams(("parallel", "arbitrary")),
        name="ffn",
    )(h, g, wgu, wgu, wd)


def _proj_kernel(h_ref, g_ref, w_ref, o_ref, xn_ref):
    @pl.when(pl.program_id(1) == 0)
    def _():
        xn_ref[...] = _rms(h_ref[...], g_ref[...]).astype(BF16)

    o_ref[...] = _dot(xn_ref[...], w_ref[...])


def _proj(h, g, w, tm, tn):
    T, D = h.shape
    N = w.shape[1]
    return pl.pallas_call(
        _proj_kernel,
        out_shape=jax.ShapeDtypeStruct((T, N), F32),
        grid=(T // tm, N // tn),
        in_specs=[
            pl.BlockSpec((tm, D), lambda i, j: (i, 0)),
            pl.BlockSpec((1, D), lambda i, j: (0, 0)),
            pl.BlockSpec((D, tn), lambda i, j: (0, j)),
        ],
        out_specs=pl.BlockSpec((tm, tn), lambda i, j: (i, j)),
        scratch_shapes=[pltpu.VMEM((tm, D), BF16)],
        compiler_params=_cparams(("parallel", "arbitrary")),
        name="in_proj",
    )(h, g, w)


def _merge_kernel(h_ref, m0_ref, m1_ref, m2_ref, a_ref, b_ref, c_ref, wb_ref, wo_ref, o_ref):
    merged = _sigmoid(m0_ref[...]) * _dot(a_ref[...].astype(BF16), wb_ref[0])
    merged += _sigmoid(m1_ref[...]) * _dot(b_ref[...].astype(BF16), wb_ref[1])
    merged += _sigmoid(m2_ref[...]) * _dot(c_ref[...].astype(BF16), wb_ref[2])
    o_ref[...] = h_ref[...] + _dot(merged.astype(BF16), wo_ref[...])


def _merge(h, proj, o_hg, o_ns, o_rw, wb, wo, tm):
    T, D = h.shape
    W = o_hg.shape[1]
    mg0 = COL_MG // D
    return pl.pallas_call(
        _merge_kernel,
        out_shape=jax.ShapeDtypeStruct((T, D), F32),
        grid=(T // tm,),
        in_specs=[
            pl.BlockSpec((tm, D), lambda i: (i, 0)),
            pl.BlockSpec((tm, D), lambda i: (i, mg0)),
            pl.BlockSpec((tm, D), lambda i: (i, mg0 + 1)),
            pl.BlockSpec((tm, D), lambda i: (i, mg0 + 2)),
            pl.BlockSpec((tm, W), lambda i: (i, 0)),
            pl.BlockSpec((tm, W), lambda i: (i, 0)),
            pl.BlockSpec((tm, W), lambda i: (i, 0)),
            pl.BlockSpec((3, W, D), lambda i: (0, 0, 0)),
            pl.BlockSpec((D, D), lambda i: (0, 0)),
        ],
        out_specs=pl.BlockSpec((tm, D), lambda i: (i, 0)),
        compiler_params=_cparams(("parallel",)),
        name="merge",
    )(h, proj, proj, proj, o_hg, o_ns, o_rw, wb, wo)


def _ple_kernel(h_ref, g_ref, wg_ref, p_ref, wp_ref, fg_ref, o_ref, *, final):
    h = h_ref[...]
    gate = _sigmoid(_dot(_rms(h, g_ref[...]).astype(BF16), wg_ref[...]))
    out = h + gate * _dot(p_ref[...].astype(BF16), wp_ref[...])
    if final:
        out = _rms(out, fg_ref[...])
    o_ref[...] = out


def _ple(h, g, wg, p, wp, fg, tm, final):
    T, D = h.shape
    P = p.shape[1]
    return pl.pallas_call(
        functools.partial(_ple_kernel, final=final),
        out_shape=jax.ShapeDtypeStruct((T, D), F32),
        grid=(T // tm,),
        in_specs=[
            pl.BlockSpec((tm, D), lambda i: (i, 0)),
            pl.BlockSpec((1, D), lambda i: (0, 0)),
            pl.BlockSpec((D, D), lambda i: (0, 0)),
            pl.BlockSpec((tm, P), lambda i: (i, 0)),
            pl.BlockSpec((P, D), lambda i: (0, 0)),
            pl.BlockSpec((1, D), lambda i: (0, 0)),
        ],
        out_specs=pl.BlockSpec((tm, D), lambda i: (i, 0)),
        compiler_params=_cparams(("parallel",)),
        name="ple",
    )(h, g, wg, p, wp, fg)


def _softplus(x):
    return jnp.maximum(x, 0.0) + jnp.log1p(jnp.exp(-jnp.abs(x)))


def _split2(x):
    hi = x.astype(BF16)
    lo = (x - hi.astype(F32)).astype(BF16)
    return hi, lo


def _split3(x):
    hi = x.astype(BF16)
    r1 = x - hi.astype(F32)
    mid = r1.astype(BF16)
    lo = (r1 - mid.astype(F32)).astype(BF16)
    return hi, mid, lo


def _dot3(a, b):
    ah, al = _split2(a)
    bh, bl = _split2(b)
    return _dot(ah, bh) + (_dot(ah, bl) + _dot(al, bh))


def _segsum(x, j):
    hi, lo = _split2(x)
    return _dot(hi, j) + _dot(lo, j)


def _cumsum_rows(tri, x):
    hi, mid, lo = _split3(x)
    return _dot(tri, hi) + (_dot(tri, mid) + _dot(tri, lo))


def _pair_stack(x, lo_mask):
    return jnp.concatenate([jnp.where(lo_mask, x, 0.0), jnp.where(lo_mask, 0.0, x)], axis=0)


def _consts():
    i512 = np.arange(512)
    j512 = (i512[:, None] // 64 == i512[None, :] // 64).astype(np.float32)
    i64 = np.arange(64)
    tri64 = (i64[:, None] >= i64[None, :]).astype(np.float32)
    i128 = np.arange(128)
    same = i128[:, None] // 16 == i128[None, :] // 16
    tri16 = (same & (i128[:, None] >= i128[None, :])).astype(np.float32)
    tot16 = same.astype(np.float32)
    return {"j512": jnp.asarray(j512, BF16), "tri64": jnp.asarray(tri64, BF16),
            "tri16": jnp.asarray(tri16, BF16), "tot16": jnp.asarray(tot16, BF16)}


RW_HD = 64
RW_CH = 64
RW_GN_EPS = 64e-5


def _rwkv_kernel(r_ref, k_ref, v_ref, l_ref, mu_ref, vec_ref, wb_ref, ab_ref, gb_ref, j_ref,
                 tri_ref, o_ref, carry_ref, st_ref, *, nch):
    C = RW_CH
    TC = nch * C
    W = r_ref.shape[1]
    npair = W // LANES

    @pl.when(pl.program_id(1) == 0)
    def _():
        carry_ref[...] = jnp.zeros_like(carry_ref)
        st_ref[...] = jnp.zeros_like(st_ref)

    row = lax.broadcasted_iota(jnp.int32, (TC, W), 0)

    def shift(x_ref, idx):
        x = x_ref[...]
        prev = jnp.where(row == 0, carry_ref[idx:idx + 1, :], pltpu.roll(x, 1, axis=0))
        carry_ref[idx:idx + 1, :] = x[TC - 1:TC, :]
        return x + (prev - x) * mu_ref[idx:idx + 1, :]

    xr = shift(r_ref, 0)
    xk = shift(k_ref, 1)
    xv = shift(v_ref, 2)
    xl = shift(l_ref, 3)
    w0, a0, k_k, k_a = (vec_ref[i:i + 1, :] for i in range(4))
    ln_w, ln_b, r_k = (vec_ref[i:i + 1, :] for i in range(4, 7))
    jmat = j_ref[...]

    wlal = xl[:, 0:LANES]
    w_pre = w0 + _dot(jnp.tanh(wlal).astype(BF16), wb_ref[...])
    a_pre = a0 + _dot(wlal.astype(BF16), ab_ref[...])
    gate = _dot(_sigmoid(xl[:, LANES:2 * LANES]).astype(BF16), gb_ref[...])
    logw = -jnp.exp(-_softplus(-w_pre) - 0.5)
    a = _sigmoid(a_pre)
    kkr = xk * k_k
    kk = kkr / jnp.maximum(jnp.sqrt(_segsum(kkr * kkr, jmat)), 1e-12)
    k2 = xk * (1.0 + (a - 1.0) * k_a)
    ka = kk * a

    lane = lax.broadcasted_iota(jnp.int32, (C, LANES), 1)
    trow = lax.broadcasted_iota(jnp.int32, (C, LANES), 0)
    lo_mask = lane < RW_HD
    scol = lane & (RW_HD - 1)
    strict = trow > scol
    incl = trow >= scol
    eye2 = (trow == scol).astype(F32)
    r128 = lax.broadcasted_iota(jnp.int32, (LANES, LANES), 0)
    c128 = lax.broadcasted_iota(jnp.int32, (LANES, LANES), 1)
    bd_mask = (r128 // RW_HD) == (c128 // RW_HD)
    diag_mask = r128 == c128
    tri = tri_ref[...]

    def bf(x):
        return x.astype(BF16)

    def stack(x):
        return _pair_stack(x, lo_mask)

    ops = []
    for c in range(nch):
        rs = slice(c * C, (c + 1) * C)
        lw = logw[rs]
        b = _cumsum_rows(tri, lw)
        bend = b[C - 1:C, :]
        enb = jnp.exp(-b)
        egc = jnp.exp(bend - b)
        g_end = jnp.exp(bend)
        full = (xr[rs] * jnp.exp(b), k2[rs] * enb, ka[rs] * enb, kk[rs] * jnp.exp(b - lw), k2[rs] * egc,
                ka[rs] * egc, xv[rs], jnp.broadcast_to(g_end, (C, W)))
        for p in range(npair):
            ops.append(tuple(t[:, p * LANES:(p + 1) * LANES] for t in full))
    n = len(ops)
    gms = [_nt(bf(jnp.concatenate([bt, rt], axis=0)), bf(jnp.concatenate([stack(at), stack(kt)], axis=0)))
           for rt, kt, at, bt, _, _, _, _ in ops]
    a_ba = [jnp.where(strict, gm[0:C, 0:LANES], 0.0) for gm in gms]
    a_bk = [jnp.where(strict, gm[0:C, LANES:], 0.0) for gm in gms]
    a_ra = [jnp.where(incl, gm[C:, 0:LANES], 0.0) for gm in gms]
    a_rk = [jnp.where(incl, gm[C:, LANES:], 0.0) for gm in gms]
    pw = [-a for a in a_ba]
    ti = [eye2 + x for x in pw]
    for _ in range(int(math.log2(C)) - 1):
        pw = [_dot(bf(x), bf(stack(x))) for x in pw]
        ti = [t + _dot(bf(t), bf(stack(x))) for t, x in zip(ti, pw)]
    tib = [bf(t) for t in ti]
    wm = [_dot(tib[j], bf(stack(ops[j][3]))) for j in range(n)]
    av = [_dot(bf(a_bk[j]), bf(stack(ops[j][6]))) for j in range(n)]
    u0 = [_dot(tib[j], bf(stack(av[j]))) for j in range(n)]
    y0 = [_dot(bf(jnp.concatenate([a_rk[j], a_ra[j]], axis=1)),
               bf(jnp.concatenate([stack(ops[j][6]), -stack(u0[j])], axis=0))) for j in range(n)]
    rw = [ops[j][0] - _dot(bf(a_ra[j]), bf(stack(wm[j]))) for j in range(n)]
    m2 = [jnp.where(diag_mask, jnp.concatenate([ops[j][7], ops[j][7]], axis=0), 0.0)
          - jnp.where(bd_mask, _dot(bf(ops[j][5].T), bf(wm[j])), 0.0) for j in range(n)]
    n2 = [jnp.where(bd_mask, _dot(bf(jnp.concatenate([ops[j][4], ops[j][5]], axis=0).T),
                                  bf(jnp.concatenate([ops[j][6], -u0[j]], axis=0))), 0.0) for j in range(n)]
    s2 = [st_ref[p] for p in range(npair)]
    ys = []
    for c in range(nch):
        js = [c * npair + p for p in range(npair)]
        ys.append(jnp.concatenate([y0[j] + _dot(bf(rw[j]), bf(s2[p])) for p, j in enumerate(js)], axis=1))
        s2 = [_dot3(m2[j], s2[p]) + n2[j] for p, j in enumerate(js)]
    for p in range(npair):
        st_ref[p] = s2[p]
    y = jnp.concatenate(ys, axis=0) if nch > 1 else ys[0]

    inv_n = 1.0 / RW_HD
    mean = _segsum(y, jmat) * inv_n
    yc = y - mean
    var = _segsum(yc * yc, jmat) * inv_n
    yn = yc * lax.rsqrt(var + RW_GN_EPS) * ln_w + ln_b
    bonus = _segsum(xr * k2 * r_k, jmat) * xv
    o_ref[...] = (yn + bonus) * gate


def _rwkv(proj, mu, vec, wb, ab, gb, jmat, tri, B, S, nch):
    T = proj.shape[0]
    W = RW_W
    TC = nch * RW_CH
    nblk = S // TC
    c0 = COL_RW // W
    row_map = lambda col: (lambda b, i: (b * nblk + i, col))
    const = lambda b, i: (0, 0)
    return pl.pallas_call(
        functools.partial(_rwkv_kernel, nch=nch),
        out_shape=jax.ShapeDtypeStruct((T, W), F32),
        grid=(B, nblk),
        in_specs=[
            pl.BlockSpec((TC, W), row_map(c0)),
            pl.BlockSpec((TC, W), row_map(c0 + 1)),
            pl.BlockSpec((TC, W), row_map(c0 + 2)),
            pl.BlockSpec((TC, W), row_map(c0 + 3)),
            pl.BlockSpec(mu.shape, const),
            pl.BlockSpec(vec.shape, const),
            pl.BlockSpec(wb.shape, const),
            pl.BlockSpec(ab.shape, const),
            pl.BlockSpec(gb.shape, const),
            pl.BlockSpec(jmat.shape, const),
            pl.BlockSpec(tri.shape, const),
        ],
        out_specs=pl.BlockSpec((TC, W), lambda b, i: (b * nblk + i, 0)),
        scratch_shapes=[pltpu.VMEM((8, W), F32), pltpu.VMEM((W // LANES, LANES, LANES), F32)],
        compiler_params=_cparams(("parallel", "arbitrary")),
        name="rwkv7",
    )(proj, proj, proj, proj, mu, vec, wb, ab, gb, jmat, tri)


def _rwkv_params(mu, w0, wB, a0, aB, gB, k_k, k_a, r_k, ln_w, ln_b):
    W = RW_W
    mu4 = jnp.stack([mu[0:W], mu[W:2 * W], mu[2 * W:3 * W], jnp.pad(mu[3 * W:], (0, W - (mu.shape[0] - 3 * W)))])
    vec = jnp.stack([w0, a0, k_k, k_a, ln_w, ln_b, r_k.reshape(-1), jnp.zeros_like(w0)])
    wb = jnp.pad(wB, ((0, LANES - wB.shape[0]), (0, 0))).astype(BF16)
    ab = jnp.pad(aB, ((LANES - aB.shape[0], 0), (0, 0))).astype(BF16)
    return mu4, vec, wb, ab, gB.astype(BF16)


HG_HD = 64
HG_SUB = 16


def _hgrn_kernel(q_ref, f_ref, i_ref, g_ref, lbp_ref, ng_ref, j_ref, tri_ref, tot_ref, o_ref, st_ref,
                 *, layer):
    TC, W = q_ref.shape
    npair = W // LANES
    nsub = TC // HG_SUB

    @pl.when(pl.program_id(1) == 0)
    def _():
        st_ref[...] = jnp.zeros_like(st_ref)

    z = f_ref[...]
    log_f = -_softplus(-z)
    k = _sigmoid(-z)
    if layer > 0:
        lbp = lbp_ref[...]
        e = jnp.exp(lbp - jnp.max(lbp, axis=0, keepdims=True))
        sm = e / jnp.sum(e, axis=0, keepdims=True)
        lb = sm[1:2, :]
        for j in range(2, layer + 1):
            lb = lb + sm[j:j + 1, :]
        lb = jnp.maximum(lb, 0.0)
        t2 = jnp.log(lb) - _softplus(z)
        log_f = jnp.maximum(log_f, t2) + jnp.log1p(jnp.exp(-jnp.abs(log_f - t2)))
        k = (1.0 - lb) * k
    q = _silu(q_ref[...])
    v = i_ref[...]
    b = _cumsum_rows(tri_ref[...], log_f)
    bend = _cumsum_rows(tot_ref[...], log_f)
    qe = q * jnp.exp(b)
    kg = k * jnp.exp(bend - b)
    jmat = j_ref[...]
    j128 = jmat[0:LANES, 0:LANES]

    rowb = lax.broadcasted_iota(jnp.int32, (TC, LANES), 0)
    trow = lax.broadcasted_iota(jnp.int32, (HG_SUB, W), 0)
    r128 = lax.broadcasted_iota(jnp.int32, (LANES, LANES), 0)
    c128 = lax.broadcasted_iota(jnp.int32, (LANES, LANES), 1)
    bd_mask = (r128 // HG_HD) == (c128 // HG_HD)

    def bf(x):
        return x.astype(BF16)

    vts = [bf(v[:, p * LANES:(p + 1) * LANES].T) for p in range(npair)]
    outs = []
    for i in range(nsub):
        rs = slice(i * HG_SUB, (i + 1) * HG_SUB)
        b_i, q_i, k_i, v_i = b[rs], q[rs], k[rs], v[rs]
        xs = []
        for s in range(HG_SUB):
            dec = jnp.exp(jnp.minimum(b_i - b_i[s:s + 1, :], 0.0))
            xs.append(jnp.where(trow >= s, q_i * (k_i[s:s + 1, :] * dec), 0.0))
        x = bf(jnp.concatenate(xs, axis=0))
        g_end = jnp.exp(bend[i * HG_SUB:i * HG_SUB + 1, :])
        op = []
        for p in range(npair):
            ls = slice(p * LANES, (p + 1) * LANES)
            pm = _dot(x[:, ls], j128)
            od = pm[0:HG_SUB] * v_i[0:1, ls]
            for s in range(1, HG_SUB):
                od = od + pm[s * HG_SUB:(s + 1) * HG_SUB] * v_i[s:s + 1, ls]
            st = st_ref[p]
            oi = lax.dot_general(bf(qe[rs, ls]), bf(st), (((1,), (1,)), ((), ())), preferred_element_type=F32)
            kgm = jnp.where((rowb >= i * HG_SUB) & (rowb < (i + 1) * HG_SUB), kg[:, ls], 0.0)
            st_ref[p] = st * g_end[:, ls] + jnp.where(bd_mask, _dot(vts[p], bf(kgm)), 0.0)
            op.append(od + oi)
        outs.append(jnp.concatenate(op, axis=1))
    o = jnp.concatenate(outs, axis=0)
    ms = _segsum(o * o, jmat) * (1.0 / HG_HD)
    o_ref[...] = o * lax.rsqrt(ms + RMS_EPS) * ng_ref[...] * _silu(g_ref[...])


def _hgrn(proj, lbp, ng, jmat, tri, tot, B, S, layer, tc):
    T = proj.shape[0]
    W = HG_W
    nblk = S // tc
    c0 = COL_HG // W
    row_map = lambda col: (lambda b, i: (b * nblk + i, col))
    const = lambda b, i: (0, 0)
    return pl.pallas_call(
        functools.partial(_hgrn_kernel, layer=layer),
        out_shape=jax.ShapeDtypeStruct((T, W), F32),
        grid=(B, nblk),
        in_specs=[
            pl.BlockSpec((tc, W), row_map(c0)),
            pl.BlockSpec((tc, W), row_map(c0 + 1)),
            pl.BlockSpec((tc, W), row_map(c0 + 2)),
            pl.BlockSpec((tc, W), row_map(c0 + 3)),
            pl.BlockSpec(lbp.shape, const),
            pl.BlockSpec(ng.shape, const),
            pl.BlockSpec(jmat.shape, const),
            pl.BlockSpec(tri.shape, const),
            pl.BlockSpec(tot.shape, const),
        ],
        out_specs=pl.BlockSpec((tc, W), lambda b, i: (b * nblk + i, 0)),
        scratch_shapes=[pltpu.VMEM((W // LANES, LANES, LANES), F32)],
        compiler_params=_cparams(("parallel", "arbitrary")),
        name="hgrn2",
    )(proj, proj, proj, proj, lbp, ng, jmat, tri, tot)


NSA_HD = 64
NSA_HEADS = 8
NSA_TQ = 128
CMP_STRIDE = 16
CMP_LEN = 32
SLC_BLOCK = 64
SLC_TOPN = 8
NEG_INF = float("-inf")
V_ROWS = NSA_HD + 16


def _nsa_kernel(q_ref, kv1_ref, kv2_ref, gt_ref, w1_ref, pe_ref, w2_ref, biasc_ref, bt_ref, cov_ref, egt_ref,
                o_ref, kc_ref, vct_ref, k_ref, vt_ref, am_ref, m_ref, acc_ref, cx_ref):
    TQ = NSA_TQ
    S = kv1_ref.shape[0]
    nkc = S // TQ
    ncmp = S // CMP_STRIDE
    nslc = S // SLC_BLOCK
    GW = (NSA_HEADS // 2) * TQ
    i = pl.program_id(1)

    def with_ones(vt_g):
        return jnp.concatenate([vt_g, jnp.ones((V_ROWS - NSA_HD, vt_g.shape[1]), F32)], axis=0).astype(BF16)

    @pl.when(i == 0)
    def _():
        k_ref[0] = kv1_ref[:, 2 * LANES:3 * LANES].astype(BF16)
        k_ref[1] = kv2_ref[:, 0:LANES].astype(BF16)
        for src, (ref, c0) in enumerate(((kv1_ref, 3), (kv2_ref, 1))):
            for c in range(nkc):
                vt = ref[c * TQ:(c + 1) * TQ, c0 * LANES:(c0 + 1) * LANES].T
                for g in range(2):
                    vt_ref[src, g, c] = with_ones(vt[g * NSA_HD:(g + 1) * NSA_HD])
        for t in range(2):
            cx_ref[...] = kv1_ref[:, t * LANES:(t + 1) * LANES]
            acc_a = jnp.zeros((ncmp, 2 * LANES), F32)
            acc_b = jnp.zeros((ncmp, 2 * LANES), F32)
            for l in range(CMP_STRIDE):
                xl = cx_ref[pl.ds(l, ncmp, stride=CMP_STRIDE), :]
                acc_a += _dot((xl + pe_ref[t, l:l + 1, :]).astype(BF16), w1_ref[t, l])
                acc_b += _dot((xl + pe_ref[t, CMP_STRIDE + l:CMP_STRIDE + l + 1, :]).astype(BF16),
                              w1_ref[t, CMP_STRIDE + l])
            hid = _silu(acc_a + pltpu.roll(acc_b, ncmp - 1, axis=0))
            cmp = _dot(hid.astype(BF16), w2_ref[t])
            if t == 0:
                kc_ref[...] = cmp.astype(BF16)
            else:
                ct = cmp.T
                for g in range(2):
                    vct_ref[g] = with_ones(ct[g * NSA_HD:(g + 1) * NSA_HD])

    qs = q_ref[...] * (NSA_HD ** -0.5)
    zero = jnp.zeros((NSA_HD, TQ), F32)
    cols = []
    for pp in range(NSA_HEADS // 2):
        qt = qs[:, pp * LANES:(pp + 1) * LANES].T
        for e in range(2):
            qh = qt[e * NSA_HD:(e + 1) * NSA_HD]
            cols.append(jnp.concatenate([qh, zero] if pp < 2 else [zero, qh], axis=0))
    wq = jnp.concatenate(cols, axis=1).astype(BF16)

    def group_rows(o_t):
        return jnp.concatenate([o_t[:, hh * TQ:(hh + 1) * TQ] for hh in range(NSA_HEADS // 2)], axis=0)

    sc = _dot(kc_ref[...], wq) + biasc_ref[0]
    mx = jnp.max(sc, axis=0, keepdims=True)
    ex = jnp.exp(sc - jnp.where(mx == NEG_INF, 0.0, mx))
    p = ex * (1.0 / jnp.maximum(jnp.sum(ex, axis=0, keepdims=True), 1e-30))
    o_cmp = []
    for g in range(2):
        pg = p[:, g * GW:(g + 1) * GW]
        o_cmp.append(group_rows(_dot(vct_ref[g, 0:NSA_HD, :], pg.astype(BF16))))
        psum = pg[:, 0:TQ]
        for hh in range(1, NSA_HEADS // 2):
            psum = psum + pg[:, hh * TQ:(hh + 1) * TQ]
        hi, lw = _split2(psum)
        imp_t = _dot(cov_ref[...], hi) + _dot(cov_ref[...], lw)
        blk = lax.broadcasted_iota(jnp.int32, (nslc, TQ), 0)
        cur = (i * TQ + lax.broadcasted_iota(jnp.int32, (nslc, TQ), 1)) // SLC_BLOCK
        forced = (blk == 0) | (blk == cur) | (blk == cur - 1)
        score = jnp.where(forced, jnp.inf, jnp.where(blk <= cur, imp_t, NEG_INF))
        cnt = jnp.zeros((nslc, TQ), F32)
        for mp in range(nslc):
            sm = score[mp:mp + 1, :]
            ahead = (sm > score) | ((sm == score) & (blk > mp))
            cnt = cnt + jnp.where(ahead, 1.0, 0.0)
        am_ref[g] = jnp.where(cnt < SLC_TOPN, 0.0, NEG_INF)

    def reset(br):
        m_ref[br] = jnp.full(m_ref.shape[1:], NEG_INF, F32)
        acc_ref[br] = jnp.zeros(acc_ref.shape[1:], F32)

    def chunks_step(br, kcs, bidxs, masked):
        ss = []
        for kc, bidx in zip(kcs, bidxs):
            rows = pl.ds(pl.multiple_of(kc * TQ, TQ), TQ)
            s = _dot(k_ref[br, rows, :], wq) + bt_ref[bidx]
            if masked:
                mk = []
                for g in range(2):
                    halves = [jnp.broadcast_to(am_ref[g, pl.ds(2 * kc + u, 1), :], (SLC_BLOCK, TQ)) for u in range(2)]
                    mk.append(jnp.concatenate(halves, axis=0))
                s = s + jnp.concatenate([mk[0]] * (NSA_HEADS // 2) + [mk[1]] * (NSA_HEADS // 2), axis=1)
            ss.append(s)
        m_prev = m_ref[br]
        m_new = m_prev
        for s in ss:
            m_new = jnp.maximum(m_new, jnp.max(s, axis=0, keepdims=True))
        m_safe = jnp.where(m_new == NEG_INF, 0.0, m_new)
        alpha = jnp.exp(m_prev - m_safe)
        pts = [jnp.exp(s - m_safe).astype(BF16) for s in ss]
        for g in range(2):
            upd = alpha[:, g * GW:(g + 1) * GW] * acc_ref[br, g]
            for kc, pt in zip(kcs, pts):
                upd = upd + _dot(vt_ref[br, g, kc], pt[:, g * GW:(g + 1) * GW])
            acc_ref[br, g] = upd
        m_ref[br] = m_new

    def result(br):
        outs = []
        for g in range(2):
            acc = acc_ref[br, g]
            inv = 1.0 / jnp.maximum(acc[NSA_HD:NSA_HD + 1, :], 1e-30)
            outs.append(group_rows(acc[0:NSA_HD, :] * inv))
        return outs

    reset(0)

    def pair(pr, carry):
        kcs = [2 * pr, 2 * pr + 1]
        chunks_step(0, kcs, [jnp.clip(i - kc, -1, 2) + 1 for kc in kcs], True)
        return carry

    lax.fori_loop(0, (i + 2) // 2, pair, 0)
    reset(1)
    win = [(i - 2, 4), (i - 1, 2), (i, 1)]
    chunks_step(1, [jnp.maximum(kc, 0) for kc, _ in win], [jnp.where(kc < 0, 0, tile) for kc, tile in win], False)

    o_slc = result(0)
    o_win = result(1)

    sig_hi, sig_lo = _split2(_sigmoid(gt_ref[...]).T)
    out_t = None
    for j, parts in enumerate((o_cmp, o_slc, o_win)):
        gate_t = _dot(egt_ref[j], sig_hi) + _dot(egt_ref[j], sig_lo)
        term = gate_t * jnp.concatenate(parts, axis=0)
        out_t = term if out_t is None else out_t + term
    o_ref[...] = jnp.concatenate([out_t[c * LANES:(c + 1) * LANES].T for c in range(NSA_W // LANES)], axis=1)


def _nsa(proj, w1b, peb, w2b, tabs, B, S):
    T = proj.shape[0]
    W = NSA_W
    TQ = NSA_TQ
    nq = S // TQ
    HW = NSA_HEADS * TQ
    c0 = COL_NSA // W
    cg = (COL_NSA + 2 * W + 2 * LANES) // LANES
    biasc, bt, cov, egt = tabs

    def const(shape):
        return pl.BlockSpec(shape, lambda b, i, _n=len(shape): (0,) * _n)

    return pl.pallas_call(
        _nsa_kernel,
        out_shape=jax.ShapeDtypeStruct((T, W), F32),
        grid=(B, nq),
        in_specs=[
            pl.BlockSpec((TQ, W), lambda b, i: (b * nq + i, c0)),
            pl.BlockSpec((S, W), lambda b, i: (b, c0 + 1)),
            pl.BlockSpec((S, W), lambda b, i: (b, c0 + 2)),
            pl.BlockSpec((TQ, LANES), lambda b, i: (b * nq + i, cg)),
            const(w1b.shape), const(peb.shape), const(w2b.shape),
            pl.BlockSpec((1, S // CMP_STRIDE, HW), lambda b, i: (i, 0, 0)),
            const(bt.shape), const(cov.shape), const(egt.shape),
        ],
        out_specs=pl.BlockSpec((TQ, W), lambda b, i: (b * nq + i, 0)),
        scratch_shapes=[
            pltpu.VMEM((S // CMP_STRIDE, LANES), BF16),
            pltpu.VMEM((2, V_ROWS, S // CMP_STRIDE), BF16),
            pltpu.VMEM((2, S, LANES), BF16),
            pltpu.VMEM((2, 2, S // TQ, V_ROWS, TQ), BF16),
            pltpu.VMEM((2, S // SLC_BLOCK, TQ), F32),
            pltpu.VMEM((2, 1, HW), F32),
            pltpu.VMEM((2, 2, V_ROWS, HW // 2), F32),
            pltpu.VMEM((S, LANES), F32),
        ],
        compiler_params=_cparams(("parallel", "arbitrary")),
        name="nsa",
    )(proj, proj, proj, proj, w1b, peb, w2b, biasc, bt, cov, egt)


def _blockdiag2(w):
    z = jnp.zeros_like(w)
    return jnp.concatenate([jnp.concatenate([w, z], axis=-1), jnp.concatenate([z, w], axis=-1)], axis=-2)


def _nsa_params(pe, w1, w2):
    L, Dh = pe.shape[1], pe.shape[2]
    w1b = _blockdiag2(w1.reshape(2, L, Dh, -1)).astype(BF16)
    w2b = _blockdiag2(w2).astype(BF16)
    peb = jnp.concatenate([pe, pe], axis=-1)
    return w1b, peb, w2b


def _t5_bucket_np(n):
    n = np.maximum(n, 0)
    nf = np.maximum(n, 1).astype(np.float64)
    large = 16 + (np.log(nf / 16.0) / math.log(128 / 16) * 16).astype(np.int64)
    return np.where(n < 16, n, np.minimum(large, 31))


def _nsa_tables(rel_bias, S):
    TQ = NSA_TQ
    nq = S // TQ
    rb = rel_bias.astype(F32)

    def tile(dist, valid):
        vals = jnp.where(jnp.asarray(valid)[..., None], rb[jnp.asarray(_t5_bucket_np(dist))], NEG_INF)
        vals = jnp.swapaxes(vals, -1, -2)
        return vals.reshape(vals.shape[:-2] + (NSA_HEADS * TQ,))

    ncmp = S // CMP_STRIDE
    end = np.arange(ncmp) * CMP_STRIDE + CMP_LEN - 1
    qpos = np.arange(S).reshape(nq, 1, TQ)
    dist_c = qpos - end[None, :, None]
    valid_c = (dist_c >= 0) & (np.arange(ncmp) < ncmp - 1)[None, :, None]
    biasc = tile(dist_c, valid_c)
    t = np.arange(TQ)[None, :]
    j = np.arange(TQ)[:, None]
    every = np.ones((TQ, TQ), bool)
    bt = jnp.stack([
        tile(t - j, ~every),
        tile(t - j, t >= j),
        tile(t - j + TQ, every),
        tile(t - j + 2 * TQ, every),
        tile(t - j + 2 * TQ, j > t),
    ])
    nslc = S // SLC_BLOCK
    s_lo = np.arange(nslc) * SLC_BLOCK
    start = np.arange(ncmp) * CMP_STRIDE
    cover = ((start[:, None] <= (s_lo + SLC_BLOCK - 1)[None, :]) & (end[:, None] >= s_lo[None, :]))
    cover &= (np.arange(ncmp) < ncmp - 1)[:, None]
    egt = np.zeros((3, NSA_W, LANES), np.float32)
    for jj in range(3):
        egt[jj, np.arange(NSA_W), (np.arange(NSA_W) // NSA_HD) * 3 + jj] = 1.0
    return biasc, bt, jnp.asarray(cover.T.astype(np.float32), BF16), jnp.asarray(egt, BF16)


def _tiles(T, n_ff):
    tm = 512 if T % 512 == 0 else T
    tf = n_ff // 2 if (n_ff // 2) % LANES == 0 else n_ff
    tn = N_PROJ // 4
    return tm, tf, tn


def _pack_w_in(w):
    hg_end = 4 * HG_W
    nsa_end = hg_end + NSA_W + 6 * 2 * NSA_HD + 3 * 8
    rw_end = nsa_end + 3 * RW_W + 256
    d = w.shape[0]
    parts = [w[:, rw_end:], w[:, :hg_end], w[:, hg_end:nsa_end],
             jnp.zeros((d, COL_RW - COL_NSA - (nsa_end - hg_end)), w.dtype), w[:, nsa_end:rw_end],
             jnp.zeros((d, N_PROJ - COL_RW - (rw_end - nsa_end)), w.dtype)]
    return jnp.concatenate(parts, axis=1).astype(BF16)


def kernel(x, p, ffn1_norm, ffn1_wgu, ffn1_wd, mix_norm, w_in, hg_lb, hg_norm, cmp_pe, cmp_w1, cmp_w2, rel_bias, rw_mu, rw_w0, rw_wB, rw_a0, rw_aB, rw_gB, rw_kk, rw_ka, rw_rk, rw_ln_w, rw_ln_b, w_branch, w_out, ffn2_norm, ffn2_wgu, ffn2_wd, ple_norm, ple_gate_w, ple_w, final_norm):
    B, S, D = x.shape
    depth = ffn1_norm.shape[0]
    T = B * S
    assert D == D_MODEL and S // CMP_STRIDE == LANES and S % NSA_TQ == 0
    assert w_in.shape[2] - 3 * D_MODEL == 4 * HG_W + NSA_W + 12 * NSA_HD + 24 + 3 * RW_W + 256
    tm, tf, tn = _tiles(T, ffn1_wd.shape[1])
    consts = _consts()
    tabs = _nsa_tables(rel_bias, S)
    row = lambda v: v.reshape(1, -1)
    h = x.reshape(T, D)
    for i in range(depth):
        h = _ffn(h, row(ffn1_norm[i]), ffn1_wgu[i].astype(BF16), ffn1_wd[i].astype(BF16), tm, tf)
        proj = _proj(h, row(mix_norm[i]), _pack_w_in(w_in[i]), tm, tn)
        o_hg = _hgrn(proj, hg_lb, row(hg_norm[i]), consts["j512"], consts["tri16"], consts["tot16"], B, S, i, 128)
        o_ns = _nsa(proj, *_nsa_params(cmp_pe[i], cmp_w1[i], cmp_w2[i]), tabs, B, S)
        o_rw = _rwkv(proj, *_rwkv_params(rw_mu[i], rw_w0[i], rw_wB[i], rw_a0[i], rw_aB[i], rw_gB[i], rw_kk[i],
                                         rw_ka[i], rw_rk[i], rw_ln_w[i], rw_ln_b[i]),
                     consts["j512"], consts["tri64"], B, S, 4)
        h = _merge(h, proj, o_hg, o_ns, o_rw, w_branch[i].astype(BF16), w_out[i].astype(BF16), tm)
        h = _ffn(h, row(ffn2_norm[i]), ffn2_wgu[i].astype(BF16), ffn2_wd[i].astype(BF16), tm, tf)
        h = _ple(h, row(ple_norm[i]), ple_gate_w[i].astype(BF16), p[i].reshape(T, -1), ple_w[i].astype(BF16),
                 row(final_norm), tm, i == depth - 1)
    return h.reshape(B, S, D)
```

```python
import functools
import math

import jax
import jax.numpy as jnp
import numpy as np
from jax import lax
from jax.experimental import pallas as pl
from jax.experimental.pallas import tpu as pltpu

F32 = jnp.float32
BF16 = jnp.bfloat16

RMS_EPS = 1e-6
LANES = 128
VMEM_LIMIT = 48 * 1024 * 1024

HG_W = 512
NSA_W = 512
RW_W = 512
D_MODEL = 1024
COL_MG = 0
COL_HG = 3072
COL_NSA = 5120
COL_RW = 6656
N_PROJ = 8704


def _cparams(sem):
    return pltpu.CompilerParams(dimension_semantics=sem, vmem_limit_bytes=VMEM_LIMIT)


def _rms(x, g):
    return x * lax.rsqrt(jnp.mean(x * x, axis=-1, keepdims=True) + RMS_EPS) * g


def _sigmoid(x):
    return 1.0 / (1.0 + jnp.exp(-x))


def _silu(x):
    return x * _sigmoid(x)


def _dot(a, b):
    return jnp.dot(a, b, preferred_element_type=F32)


def _nt(a, b):
    return lax.dot_general(a, b, (((1,), (1,)), ((), ())), preferred_element_type=F32)


def _ffn_kernel(h_ref, g_ref, wg_ref, wu_ref, wd_ref, o_ref, xn_ref, acc_ref):
    j = pl.program_id(1)

    @pl.when(j == 0)
    def _():
        xn_ref[...] = _rms(h_ref[...], g_ref[...]).astype(BF16)
        acc_ref[...] = jnp.zeros_like(acc_ref)

    xn = xn_ref[...]
    gate = _dot(xn, wg_ref[...])
    up = _dot(xn, wu_ref[...])
    act = (_silu(gate) * up).astype(BF16)
    acc_ref[...] += _dot(act, wd_ref[...])

    @pl.when(j == pl.num_programs(1) - 1)
    def _():
        o_ref[...] = h_ref[...] + 0.5 * acc_ref[...]


def _ffn(h, g, wgu, wd, tm, tf):
    T, D = h.shape
    FF = wd.shape[0]
    nf = FF // tf
    return pl.pallas_call(
        _ffn_kernel,
        out_shape=jax.ShapeDtypeStruct((T, D), F32),
        grid=(T // tm, nf),
        in_specs=[
            pl.BlockSpec((tm, D), lambda i, j: (i, 0)),
            pl.BlockSpec((1, D), lambda i, j: (0, 0)),
            pl.BlockSpec((D, tf), lambda i, j: (0, j)),
            pl.BlockSpec((D, tf), lambda i, j: (0, j + nf)),
            pl.BlockSpec((tf, D), lambda i, j: (j, 0)),
        ],
        out_specs=pl.BlockSpec((tm, D), lambda i, j: (i, 0)),
        scratch_shapes=[pltpu.VMEM((tm, D), BF16), pltpu.VMEM((tm, D), F32)],
        compiler_params=_cparams(("parallel", "arbitrary")),
        name="ffn",
    )(h, g, wgu, wgu, wd)


def _proj_kernel(h_ref, g_ref, w_ref, o_ref):
    o_ref[...] = _dot(_rms(h_ref[...], g_ref[...]).astype(BF16), w_ref[...])


def _proj(h, g, w, tm, tn):
    T, D = h.shape
    N = w.shape[1]
    return pl.pallas_call(
        _proj_kernel,
        out_shape=jax.ShapeDtypeStruct((T, N), F32),
        grid=(N // tn, T // tm),
        in_specs=[
            pl.BlockSpec((tm, D), lambda j, i: (i, 0)),
            pl.BlockSpec((1, D), lambda j, i: (0, 0)),
            pl.BlockSpec((D, tn), lambda j, i: (0, j)),
        ],
        out_specs=pl.BlockSpec((tm, tn), lambda j, i: (i, j)),
        compiler_params=_cparams(("parallel", "parallel")),
        name="in_proj",
    )(h, g, w)


def _merge_kernel(h_ref, m0_ref, m1_ref, m2_ref, a_ref, b_ref, c_ref, wb_ref, wo_ref, o_ref):
    merged = _sigmoid(m0_ref[...]) * _dot(a_ref[...].astype(BF16), wb_ref[0])
    merged += _sigmoid(m1_ref[...]) * _dot(b_ref[...].astype(BF16), wb_ref[1])
    merged += _sigmoid(m2_ref[...]) * _dot(c_ref[...].astype(BF16), wb_ref[2])
    o_ref[...] = h_ref[...] + _dot(merged.astype(BF16), wo_ref[...])


def _merge(h, proj, o_hg, o_ns, o_rw, wb, wo, tm):
    T, D = h.shape
    W = o_hg.shape[1]
    mg0 = COL_MG // D
    return pl.pallas_call(
        _merge_kernel,
        out_shape=jax.ShapeDtypeStruct((T, D), F32),
        grid=(T // tm,),
        in_specs=[
            pl.BlockSpec((tm, D), lambda i: (i, 0)),
            pl.BlockSpec((tm, D), lambda i: (i, mg0)),
            pl.BlockSpec((tm, D), lambda i: (i, mg0 + 1)),
            pl.BlockSpec((tm, D), lambda i: (i, mg0 + 2)),
            pl.BlockSpec((tm, W), lambda i: (i, 0)),
            pl.BlockSpec((tm, W), lambda i: (i, 0)),
            pl.BlockSpec((tm, W), lambda i: (i, 0)),
            pl.BlockSpec((3, W, D), lambda i: (0, 0, 0)),
            pl.BlockSpec((D, D), lambda i: (0, 0)),
        ],
        out_specs=pl.BlockSpec((tm, D), lambda i: (i, 0)),
        compiler_params=_cparams(("parallel",)),
        name="merge",
    )(h, proj, proj, proj, o_hg, o_ns, o_rw, wb, wo)


def _ple_kernel(h_ref, g_ref, wg_ref, p_ref, wp_ref, fg_ref, o_ref, *, final):
    h = h_ref[...]
    gate = _sigmoid(_dot(_rms(h, g_ref[...]).astype(BF16), wg_ref[...]))
    out = h + gate * _dot(p_ref[...].astype(BF16), wp_ref[...])
    if final:
        out = _rms(out, fg_ref[...])
    o_ref[...] = out


def _ple(h, g, wg, p, wp, fg, tm, final):
    T, D = h.shape
    P = p.shape[1]
    return pl.pallas_call(
        functools.partial(_ple_kernel, final=final),
        out_shape=jax.ShapeDtypeStruct((T, D), F32),
        grid=(T // tm,),
        in_specs=[
            pl.BlockSpec((tm, D), lambda i: (i, 0)),
            pl.BlockSpec((1, D), lambda i: (0, 0)),
            pl.BlockSpec((D, D), lambda i: (0, 0)),
            pl.BlockSpec((tm, P), lambda i: (i, 0)),
            pl.BlockSpec((P, D), lambda i: (0, 0)),
            pl.BlockSpec((1, D), lambda i: (0, 0)),
        ],
        out_specs=pl.BlockSpec((tm, D), lambda i: (i, 0)),
        compiler_params=_cparams(("parallel",)),
        name="ple",
    )(h, g, wg, p, wp, fg)


def _softplus(x):
    return jnp.maximum(x, 0.0) + jnp.log(1.0 + jnp.exp(-jnp.abs(x)))


def _split2(x):
    hi = x.astype(BF16)
    lo = (x - hi.astype(F32)).astype(BF16)
    return hi, lo


def _split3(x):
    hi = x.astype(BF16)
    r1 = x - hi.astype(F32)
    mid = r1.astype(BF16)
    lo = (r1 - mid.astype(F32)).astype(BF16)
    return hi, mid, lo


def _dot3(a, b):
    ah, al = _split2(a)
    bh, bl = _split2(b)
    return _dot(ah, bh) + (_dot(ah, bl) + _dot(al, bh))


def _segsum(x, j):
    hi, lo = _split2(x)
    return _dot(hi, j) + _dot(lo, j)


def _cumsum_rows(tri, x):
    hi, mid, lo = _split3(x)
    return _dot(tri, hi) + (_dot(tri, mid) + _dot(tri, lo))


def _pair_stack(x, lo_mask):
    return jnp.concatenate([jnp.where(lo_mask, x, 0.0), jnp.where(lo_mask, 0.0, x)], axis=0)


def _consts():
    i512 = np.arange(512)
    j512 = (i512[:, None] // 64 == i512[None, :] // 64).astype(np.float32)
    i64 = np.arange(64)
    tri64 = (i64[:, None] >= i64[None, :]).astype(np.float32)
    i128 = np.arange(128)
    same = i128[:, None] // 16 == i128[None, :] // 16
    tri16 = (same & (i128[:, None] >= i128[None, :])).astype(np.float32)
    tot16 = same.astype(np.float32)
    return {"j512": jnp.asarray(j512, BF16), "tri64": jnp.asarray(tri64, BF16),
            "tri16": jnp.asarray(tri16, BF16), "tot16": jnp.asarray(tot16, BF16)}


RW_HD = 64
RW_CH = 64
RW_GN_EPS = 64e-5


def _rwkv_kernel(r_ref, k_ref, v_ref, l_ref, mu_ref, vec_ref, wb_ref, ab_ref, gb_ref, j_ref,
                 tri_ref, o_ref, carry_ref, st_ref, *, nch):
    C = RW_CH
    TC = nch * C
    W = r_ref.shape[1]
    npair = W // LANES

    @pl.when(pl.program_id(1) == 0)
    def _():
        carry_ref[...] = jnp.zeros_like(carry_ref)
        st_ref[...] = jnp.zeros_like(st_ref)

    row = lax.broadcasted_iota(jnp.int32, (TC, W), 0)

    def shift(x_ref, idx):
        x = x_ref[...]
        prev = jnp.where(row == 0, carry_ref[idx:idx + 1, :], pltpu.roll(x, 1, axis=0))
        carry_ref[idx:idx + 1, :] = x[TC - 1:TC, :]
        return x + (prev - x) * mu_ref[idx:idx + 1, :]

    xr = shift(r_ref, 0)
    xk = shift(k_ref, 1)
    xv = shift(v_ref, 2)
    xl = shift(l_ref, 3)
    w0, a0, k_k, k_a = (vec_ref[i:i + 1, :] for i in range(4))
    ln_w, ln_b, r_k = (vec_ref[i:i + 1, :] for i in range(4, 7))
    jmat = j_ref[...]

    wlal = xl[:, 0:LANES]
    w_pre = w0 + _dot(jnp.tanh(wlal).astype(BF16), wb_ref[...])
    a_pre = a0 + _dot(wlal.astype(BF16), ab_ref[...])
    gate = _dot(_sigmoid(xl[:, LANES:2 * LANES]).astype(BF16), gb_ref[...])
    logw = -jnp.exp(-_softplus(-w_pre) - 0.5)
    a = _sigmoid(a_pre)
    kkr = xk * k_k
    kk = kkr / jnp.maximum(jnp.sqrt(_segsum(kkr * kkr, jmat)), 1e-12)
    k2 = xk * (1.0 + (a - 1.0) * k_a)
    ka = kk * a

    lane = lax.broadcasted_iota(jnp.int32, (C, LANES), 1)
    trow = lax.broadcasted_iota(jnp.int32, (C, LANES), 0)
    lo_mask = lane < RW_HD
    scol = lane & (RW_HD - 1)
    strict = trow > scol
    incl = trow >= scol
    eye2 = (trow == scol).astype(F32)
    r128 = lax.broadcasted_iota(jnp.int32, (LANES, LANES), 0)
    c128 = lax.broadcasted_iota(jnp.int32, (LANES, LANES), 1)
    bd_mask = (r128 // RW_HD) == (c128 // RW_HD)
    diag_mask = r128 == c128
    tri = tri_ref[...]

    def bf(x):
        return x.astype(BF16)

    def stack(x):
        return _pair_stack(x, lo_mask)

    ops = []
    for c in range(nch):
        rs = slice(c * C, (c + 1) * C)
        lw = logw[rs]
        b = _cumsum_rows(tri, lw)
        bend = b[C - 1:C, :]
        enb = jnp.exp(-b)
        egc = jnp.exp(bend - b)
        g_end = jnp.exp(bend)
        full = (xr[rs] * jnp.exp(b), k2[rs] * enb, ka[rs] * enb, kk[rs] * jnp.exp(b - lw), k2[rs] * egc,
                ka[rs] * egc, xv[rs], jnp.broadcast_to(g_end, (C, W)))
        for p in range(npair):
            ops.append(tuple(t[:, p * LANES:(p + 1) * LANES] for t in full))
    n = len(ops)
    gms = [_nt(bf(jnp.concatenate([bt, rt], axis=0)), bf(jnp.concatenate([stack(at), stack(kt)], axis=0)))
           for rt, kt, at, bt, _, _, _, _ in ops]
    a_ba = [jnp.where(strict, gm[0:C, 0:LANES], 0.0) for gm in gms]
    a_bk = [jnp.where(strict, gm[0:C, LANES:], 0.0) for gm in gms]
    a_ra = [jnp.where(incl, gm[C:, 0:LANES], 0.0) for gm in gms]
    a_rk = [jnp.where(incl, gm[C:, LANES:], 0.0) for gm in gms]
    pw = [-a for a in a_ba]
    ti = [eye2 + x for x in pw]
    pw = [_dot(bf(x), bf(stack(x))) for x in pw]
    nsq = int(math.log2(C)) - 1
    for k in range(1, nsq):
        both = [_dot(bf(jnp.concatenate([x, t], axis=0)), bf(stack(x))) for x, t in zip(pw, ti)]
        pw = [m[0:C] for m in both]
        ti = [t + m[C:] for t, m in zip(ti, both)]
    ti = [t + _dot(bf(t), bf(stack(x))) for t, x in zip(ti, pw)]
    tib = [bf(t) for t in ti]
    wm = [_dot(tib[j], bf(stack(ops[j][3]))) for j in range(n)]
    av = [_dot(bf(a_bk[j]), bf(stack(ops[j][6]))) for j in range(n)]
    u0 = [_dot(tib[j], bf(stack(av[j]))) for j in range(n)]
    y0 = [_dot(bf(jnp.concatenate([a_rk[j], a_ra[j]], axis=1)),
               bf(jnp.concatenate([stack(ops[j][6]), -stack(u0[j])], axis=0))) for j in range(n)]
    rw = [ops[j][0] - _dot(bf(a_ra[j]), bf(stack(wm[j]))) for j in range(n)]
    m2 = [jnp.where(diag_mask, jnp.concatenate([ops[j][7], ops[j][7]], axis=0), 0.0)
          - jnp.where(bd_mask, _dot(bf(ops[j][5].T), bf(wm[j])), 0.0) for j in range(n)]
    n2 = [jnp.where(bd_mask, _dot(bf(jnp.concatenate([ops[j][4], ops[j][5]], axis=0).T),
                                  bf(jnp.concatenate([ops[j][6], -u0[j]], axis=0))), 0.0) for j in range(n)]
    s2 = [st_ref[p] for p in range(npair)]
    ys = []
    for c in range(nch):
        js = [c * npair + p for p in range(npair)]
        ys.append(jnp.concatenate([y0[j] + _dot(bf(rw[j]), bf(s2[p])) for p, j in enumerate(js)], axis=1))
        s2 = [_dot3(m2[j], s2[p]) + n2[j] for p, j in enumerate(js)]
    for p in range(npair):
        st_ref[p] = s2[p]
    y = jnp.concatenate(ys, axis=0) if nch > 1 else ys[0]

    inv_n = 1.0 / RW_HD
    mean = _segsum(y, jmat) * inv_n
    yc = y - mean
    var = _segsum(yc * yc, jmat) * inv_n
    yn = yc * lax.rsqrt(var + RW_GN_EPS) * ln_w + ln_b
    bonus = _segsum(xr * k2 * r_k, jmat) * xv
    o_ref[...] = (yn + bonus) * gate


def _rwkv(proj, mu, vec, wb, ab, gb, jmat, tri, B, S, nch):
    T = proj.shape[0]
    W = RW_W
    TC = nch * RW_CH
    nblk = S // TC
    c0 = COL_RW // W
    row_map = lambda col: (lambda b, i: (b * nblk + i, col))
    const = lambda b, i: (0, 0)
    return pl.pallas_call(
        functools.partial(_rwkv_kernel, nch=nch),
        out_shape=jax.ShapeDtypeStruct((T, W), F32),
        grid=(B, nblk),
        in_specs=[
            pl.BlockSpec((TC, W), row_map(c0)),
            pl.BlockSpec((TC, W), row_map(c0 + 1)),
            pl.BlockSpec((TC, W), row_map(c0 + 2)),
            pl.BlockSpec((TC, W), row_map(c0 + 3)),
            pl.BlockSpec(mu.shape, const),
            pl.BlockSpec(vec.shape, const),
            pl.BlockSpec(wb.shape, const),
            pl.BlockSpec(ab.shape, const),
            pl.BlockSpec(gb.shape, const),
            pl.BlockSpec(jmat.shape, const),
            pl.BlockSpec(tri.shape, const),
        ],
        out_specs=pl.BlockSpec((TC, W), lambda b, i: (b * nblk + i, 0)),
        scratch_shapes=[pltpu.VMEM((8, W), F32), pltpu.VMEM((W // LANES, LANES, LANES), F32)],
        compiler_params=_cparams(("parallel", "arbitrary")),
        name="rwkv7",
    )(proj, proj, proj, proj, mu, vec, wb, ab, gb, jmat, tri)


def _rwkv_params(mu, w0, wB, a0, aB, gB, k_k, k_a, r_k, ln_w, ln_b):
    W = RW_W
    mu4 = jnp.stack([mu[0:W], mu[W:2 * W], mu[2 * W:3 * W], jnp.pad(mu[3 * W:], (0, W - (mu.shape[0] - 3 * W)))])
    vec = jnp.stack([w0, a0, k_k, k_a, ln_w, ln_b, r_k.reshape(-1), jnp.zeros_like(w0)])
    wb = jnp.pad(wB, ((0, LANES - wB.shape[0]), (0, 0))).astype(BF16)
    ab = jnp.pad(aB, ((LANES - aB.shape[0], 0), (0, 0))).astype(BF16)
    return mu4, vec, wb, ab, gB.astype(BF16)


HG_HD = 64
HG_SUB = 16


def _hgrn_kernel(q_ref, f_ref, i_ref, g_ref, lbp_ref, ng_ref, j_ref, tri_ref, tot_ref, o_ref, st_ref,
                 *, layer):
    TC, W = q_ref.shape
    npair = W // LANES
    nsub = TC // HG_SUB

    @pl.when(pl.program_id(1) == 0)
    def _():
        st_ref[...] = jnp.zeros_like(st_ref)

    z = f_ref[...]
    log_f = -_softplus(-z)
    k = _sigmoid(-z)
    if layer > 0:
        lbp = lbp_ref[...]
        e = jnp.exp(lbp - jnp.max(lbp, axis=0, keepdims=True))
        sm = e / jnp.sum(e, axis=0, keepdims=True)
        lb = sm[1:2, :]
        for j in range(2, layer + 1):
            lb = lb + sm[j:j + 1, :]
        lb = jnp.maximum(lb, 0.0)
        t2 = jnp.log(lb) - _softplus(z)
        log_f = jnp.maximum(log_f, t2) + jnp.log(1.0 + jnp.exp(-jnp.abs(log_f - t2)))
        k = (1.0 - lb) * k
    q = _silu(q_ref[...])
    v = i_ref[...]
    b = _cumsum_rows(tri_ref[...], log_f)
    bend = _cumsum_rows(tot_ref[...], log_f)
    qe = q * jnp.exp(b)
    kg = k * jnp.exp(bend - b)
    jmat = j_ref[...]
    j128 = jmat[0:LANES, 0:LANES]

    rowb = lax.broadcasted_iota(jnp.int32, (TC, LANES), 0)
    trow = lax.broadcasted_iota(jnp.int32, (HG_SUB // 2, W), 0)
    r128 = lax.broadcasted_iota(jnp.int32, (LANES, LANES), 0)
    c128 = lax.broadcasted_iota(jnp.int32, (LANES, LANES), 1)
    bd_mask = (r128 // HG_HD) == (c128 // HG_HD)

    def bf(x):
        return x.astype(BF16)

    vts = [bf(v[:, p * LANES:(p + 1) * LANES].T) for p in range(npair)]
    outs = []
    H8 = HG_SUB // 2
    for i in range(nsub):
        rs = slice(i * HG_SUB, (i + 1) * HG_SUB)
        r0 = i * HG_SUB
        (b_a, q_a, k_a, v_a), (b_b, q_b, k_b, v_b) = (
            tuple(t[r0 + h * H8:r0 + (h + 1) * H8] for t in (b, q, k, v)) for h in range(2))
        xs = []
        for s in range(H8):
            bs, ks = b_a[s:s + 1, :], k_a[s:s + 1, :]
            xs.append(jnp.where(trow >= s, q_a * (ks * jnp.exp(b_a - bs)), 0.0))
            xs.append(q_b * (ks * jnp.exp(b_b - bs)))
        for s in range(H8):
            bs, ks = b_b[s:s + 1, :], k_b[s:s + 1, :]
            xs.append(jnp.where(trow >= s, q_b * (ks * jnp.exp(b_b - bs)), 0.0))
        x = bf(jnp.concatenate(xs, axis=0))
        g_end = jnp.exp(bend[i * HG_SUB:i * HG_SUB + 1, :])
        op = []
        for p in range(npair):
            ls = slice(p * LANES, (p + 1) * LANES)
            pm = _dot(x[:, ls], j128)
            od_a = pm[0:H8] * v_a[0:1, ls]
            od_b = pm[H8:HG_SUB] * v_a[0:1, ls]
            for s in range(1, H8):
                od_a = od_a + pm[s * HG_SUB:s * HG_SUB + H8] * v_a[s:s + 1, ls]
                od_b = od_b + pm[s * HG_SUB + H8:(s + 1) * HG_SUB] * v_a[s:s + 1, ls]
            for s in range(H8):
                od_b = od_b + pm[(HG_SUB + s) * H8:(HG_SUB + s + 1) * H8] * v_b[s:s + 1, ls]
            od = jnp.concatenate([od_a, od_b], axis=0)
            st = st_ref[p]
            oi = lax.dot_general(bf(qe[rs, ls]), bf(st), (((1,), (1,)), ((), ())), preferred_element_type=F32)
            kgm = jnp.where((rowb >= i * HG_SUB) & (rowb < (i + 1) * HG_SUB), kg[:, ls], 0.0)
            st_ref[p] = st * g_end[:, ls] + jnp.where(bd_mask, _dot(vts[p], bf(kgm)), 0.0)
            op.append(od + oi)
        outs.append(jnp.concatenate(op, axis=1))
    o = jnp.concatenate(outs, axis=0)
    ms = _segsum(o * o, jmat) * (1.0 / HG_HD)
    o_ref[...] = o * lax.rsqrt(ms + RMS_EPS) * ng_ref[...] * _silu(g_ref[...])


def _hgrn(proj, lbp, ng, jmat, tri, tot, B, S, layer, tc):
    T = proj.shape[0]
    W = HG_W
    nblk = S // tc
    c0 = COL_HG // W
    row_map = lambda col: (lambda b, i: (b * nblk + i, col))
    const = lambda b, i: (0, 0)
    return pl.pallas_call(
        functools.partial(_hgrn_kernel, layer=layer),
        out_shape=jax.ShapeDtypeStruct((T, W), F32),
        grid=(B, nblk),
        in_specs=[
            pl.BlockSpec((tc, W), row_map(c0)),
            pl.BlockSpec((tc, W), row_map(c0 + 1)),
            pl.BlockSpec((tc, W), row_map(c0 + 2)),
            pl.BlockSpec((tc, W), row_map(c0 + 3)),
            pl.BlockSpec(lbp.shape, const),
            pl.BlockSpec(ng.shape, const),
            pl.BlockSpec(jmat.shape, const),
            pl.BlockSpec(tri.shape, const),
            pl.BlockSpec(tot.shape, const),
        ],
        out_specs=pl.BlockSpec((tc, W), lambda b, i: (b * nblk + i, 0)),
        scratch_shapes=[pltpu.VMEM((W // LANES, LANES, LANES), F32)],
        compiler_params=_cparams(("parallel", "arbitrary")),
        name="hgrn2",
    )(proj, proj, proj, proj, lbp, ng, jmat, tri, tot)


NSA_HD = 64
NSA_HEADS = 8
NSA_TQ = 128
CMP_STRIDE = 16
CMP_LEN = 32
SLC_BLOCK = 64
SLC_TOPN = 8
NEG_INF = float("-inf")
V_ROWS = NSA_HD + 16


def _nsa_kernel(q_ref, kv1_ref, kv2_ref, gt_ref, w1_ref, pe_ref, w2_ref, biasc_ref, bt_ref, cov_ref, egt_ref,
                o_ref, kc_ref, vct_ref, k_ref, vt_ref, am_ref, m_ref, acc_ref, cx_ref, s_ref, p_ref, al_ref):
    TQ = NSA_TQ
    S = kv1_ref.shape[0]
    nkc = S // TQ
    ncmp = S // CMP_STRIDE
    nslc = S // SLC_BLOCK
    GW = (NSA_HEADS // 2) * TQ
    i = pl.program_id(1)

    def with_ones(vt_g):
        return jnp.concatenate([vt_g, jnp.ones((V_ROWS - NSA_HD, vt_g.shape[1]), F32)], axis=0).astype(BF16)

    @pl.when(i == 0)
    def _():
        k_ref[0] = kv1_ref[:, 2 * LANES:3 * LANES].astype(BF16)
        k_ref[1] = kv2_ref[:, 0:LANES].astype(BF16)
        for src, (ref, c0) in enumerate(((kv1_ref, 3), (kv2_ref, 1))):
            for c in range(nkc):
                vt = ref[c * TQ:(c + 1) * TQ, c0 * LANES:(c0 + 1) * LANES].T
                for g in range(2):
                    vt_ref[src, g, c] = with_ones(vt[g * NSA_HD:(g + 1) * NSA_HD])
        for t in range(2):
            cx_ref[...] = kv1_ref[:, t * LANES:(t + 1) * LANES]
            acc_a = jnp.zeros((ncmp, 2 * LANES), F32)
            acc_b = jnp.zeros((ncmp, 2 * LANES), F32)
            for l in range(CMP_STRIDE):
                xl = cx_ref[pl.ds(l, ncmp, stride=CMP_STRIDE), :]
                acc_a += _dot((xl + pe_ref[t, l:l + 1, :]).astype(BF16), w1_ref[t, l])
                acc_b += _dot((xl + pe_ref[t, CMP_STRIDE + l:CMP_STRIDE + l + 1, :]).astype(BF16),
                              w1_ref[t, CMP_STRIDE + l])
            hid = _silu(acc_a + pltpu.roll(acc_b, ncmp - 1, axis=0))
            cmp = _dot(hid.astype(BF16), w2_ref[t])
            if t == 0:
                kc_ref[...] = cmp.astype(BF16)
            else:
                ct = cmp.T
                for g in range(2):
                    vct_ref[g] = with_ones(ct[g * NSA_HD:(g + 1) * NSA_HD])

    qs = q_ref[...] * (NSA_HD ** -0.5)
    zero = jnp.zeros((NSA_HD, TQ), F32)
    cols = []
    for pp in range(NSA_HEADS // 2):
        qt = qs[:, pp * LANES:(pp + 1) * LANES].T
        for e in range(2):
            qh = qt[e * NSA_HD:(e + 1) * NSA_HD]
            cols.append(jnp.concatenate([qh, zero] if pp < 2 else [zero, qh], axis=0))
    wq = jnp.concatenate(cols, axis=1).astype(BF16)

    def group_rows(o_t):
        return jnp.concatenate([o_t[:, hh * TQ:(hh + 1) * TQ] for hh in range(NSA_HEADS // 2)], axis=0)

    cpq = TQ // CMP_STRIDE
    sc = _dot(kc_ref[...], wq) + biasc_ref[pl.ds(pl.multiple_of(cpq * (pl.num_programs(1) - 1 - i), cpq), ncmp), :]
    mx = jnp.max(sc, axis=0, keepdims=True)
    ex = jnp.exp(sc - jnp.where(mx == NEG_INF, 0.0, mx))
    p = ex * (1.0 / jnp.maximum(jnp.sum(ex, axis=0, keepdims=True), 1e-30))
    o_cmp = []
    for g in range(2):
        pg = p[:, g * GW:(g + 1) * GW]
        o_cmp.append(group_rows(_dot(vct_ref[g, 0:NSA_HD, :], pg.astype(BF16))))
        psum = pg[:, 0:TQ]
        for hh in range(1, NSA_HEADS // 2):
            psum = psum + pg[:, hh * TQ:(hh + 1) * TQ]
        hi, lw = _split2(psum)
        imp_t = _dot(cov_ref[...], hi) + _dot(cov_ref[...], lw)
        blk = lax.broadcasted_iota(jnp.int32, (nslc, TQ), 0)
        cur = (i * TQ + lax.broadcasted_iota(jnp.int32, (nslc, TQ), 1)) // SLC_BLOCK
        forced = (blk == 0) | (blk == cur) | (blk == cur - 1)
        score = jnp.where(forced, jnp.inf, jnp.where(blk <= cur, imp_t, NEG_INF))
        cnt = jnp.zeros((nslc, TQ), F32)
        for mp in range(nslc):
            sm = score[mp:mp + 1, :]
            ahead = (sm > score) | ((sm == score) & (blk > mp))
            cnt = cnt + jnp.where(ahead, 1.0, 0.0)
        am_ref[g] = jnp.where(cnt < SLC_TOPN, 0.0, NEG_INF)

    def reset(br):
        m_ref[br] = jnp.full(m_ref.shape[1:], NEG_INF, F32)
        acc_ref[br] = jnp.zeros(acc_ref.shape[1:], F32)

    def scores(br, kc, bidx, masked):
        rows = pl.ds(pl.multiple_of(kc * TQ, TQ), TQ)
        s = _dot(k_ref[br, rows, :], wq) + bt_ref[bidx]
        if masked:
            mk = []
            for g in range(2):
                halves = [jnp.broadcast_to(am_ref[g, pl.ds(2 * kc + u, 1), :], (SLC_BLOCK, TQ)) for u in range(2)]
                mk.append(jnp.concatenate(halves, axis=0))
            s = s + jnp.concatenate([mk[0]] * (NSA_HEADS // 2) + [mk[1]] * (NSA_HEADS // 2), axis=1)
        return s

    def softmax_update(br, ss):
        m_prev = m_ref[br]
        m_new = m_prev
        for s in ss:
            m_new = jnp.maximum(m_new, jnp.max(s, axis=0, keepdims=True))
        m_safe = jnp.where(m_new == NEG_INF, 0.0, m_new)
        m_ref[br] = m_new
        return jnp.exp(m_prev - m_safe), [jnp.exp(s - m_safe).astype(BF16) for s in ss]

    def value_update(br, alpha, kcs, pts):
        for g in range(2):
            upd = alpha[:, g * GW:(g + 1) * GW] * acc_ref[br, g]
            for kc, pt in zip(kcs, pts):
                upd = upd + _dot(vt_ref[br, g, kc], pt[:, g * GW:(g + 1) * GW])
            acc_ref[br, g] = upd

    def result(br):
        outs = []
        for g in range(2):
            acc = acc_ref[br, g]
            inv = 1.0 / jnp.maximum(acc[NSA_HD:NSA_HD + 1, :], 1e-30)
            outs.append(group_rows(acc[0:NSA_HD, :] * inv))
        return outs

    reset(1)
    win = [(i - 2, 4), (i - 1, 2), (i, 1)]
    win_kcs = [jnp.maximum(kc, 0) for kc, _ in win]
    alpha, pts = softmax_update(1, [scores(1, kc, jnp.where(kcr < 0, 0, tile), False)
                                    for kc, (kcr, tile) in zip(win_kcs, win)])
    value_update(1, alpha, win_kcs, pts)

    reset(0)
    npair = (i + 2) // 2

    def pair_scores(pr, slot):
        for u in range(2):
            kc = 2 * pr + u
            s_ref[slot, u] = scores(0, jnp.minimum(kc, nkc - 1), jnp.clip(i - kc, -1, 2) + 1, True)

    def pair_values(pr, slot):
        value_update(0, al_ref[slot], [jnp.clip(2 * pr + u, 0, nkc - 1) for u in range(2)],
                     [p_ref[slot, u] for u in range(2)])

    def pair_step(pr, slot):
        pair_scores(pr + 1, 1 - slot)
        pair_values(pr - 1, 1 - slot)
        alpha, pts = softmax_update(0, [s_ref[slot, u] for u in range(2)])
        al_ref[slot] = alpha
        for u in range(2):
            p_ref[slot, u] = pts[u]

    pair_scores(0, 0)
    p_ref[1] = jnp.zeros(p_ref.shape[1:], BF16)
    al_ref[1] = jnp.ones(al_ref.shape[1:], F32)

    def two_pairs(q2, carry):
        pair_step(2 * q2, 0)
        pair_step(2 * q2 + 1, 1)
        return carry

    ntrip = (npair + 1) // 2
    lax.fori_loop(0, ntrip, two_pairs, 0)
    pair_values(2 * ntrip - 1, 1)

    o_slc = result(0)
    o_win = result(1)

    sig_hi, sig_lo = _split2(_sigmoid(gt_ref[...]).T)
    out_t = None
    for j, parts in enumerate((o_cmp, o_slc, o_win)):
        gate_t = _dot(egt_ref[j], sig_hi) + _dot(egt_ref[j], sig_lo)
        term = gate_t * jnp.concatenate(parts, axis=0)
        out_t = term if out_t is None else out_t + term
    o_ref[...] = jnp.concatenate([out_t[c * LANES:(c + 1) * LANES].T for c in range(NSA_W // LANES)], axis=1)


def _nsa(proj, w1b, peb, w2b, tabs, B, S):
    T = proj.shape[0]
    W = NSA_W
    TQ = NSA_TQ
    nq = S // TQ
    HW = NSA_HEADS * TQ
    c0 = COL_NSA // W
    cg = (COL_NSA + 2 * W + 2 * LANES) // LANES
    biasc, bt, cov, egt = tabs

    def const(shape):
        return pl.BlockSpec(shape, lambda b, i, _n=len(shape): (0,) * _n)

    return pl.pallas_call(
        _nsa_kernel,
        out_shape=jax.ShapeDtypeStruct((T, W), F32),
        grid=(B, nq),
        in_specs=[
            pl.BlockSpec((TQ, W), lambda b, i: (b * nq + i, c0)),
            pl.BlockSpec((S, W), lambda b, i: (b, c0 + 1)),
            pl.BlockSpec((S, W), lambda b, i: (b, c0 + 2)),
            pl.BlockSpec((TQ, LANES), lambda b, i: (b * nq + i, cg)),
            const(w1b.shape), const(peb.shape), const(w2b.shape),
            const(biasc.shape), const(bt.shape), const(cov.shape), const(egt.shape),
        ],
        out_specs=pl.BlockSpec((TQ, W), lambda b, i: (b * nq + i, 0)),
        scratch_shapes=[
            pltpu.VMEM((S // CMP_STRIDE, LANES), BF16),
            pltpu.VMEM((2, V_ROWS, S // CMP_STRIDE), BF16),
            pltpu.VMEM((2, S, LANES), BF16),
            pltpu.VMEM((2, 2, S // TQ, V_ROWS, TQ), BF16),
            pltpu.VMEM((2, S // SLC_BLOCK, TQ), F32),
            pltpu.VMEM((2, 1, HW), F32),
            pltpu.VMEM((2, 2, V_ROWS, HW // 2), F32),
            pltpu.VMEM((S, LANES), F32),
            pltpu.VMEM((2, 2, TQ, HW), F32),
            pltpu.VMEM((2, 2, TQ, HW), BF16),
            pltpu.VMEM((2, 1, HW), F32),
        ],
        compiler_params=_cparams(("parallel", "arbitrary")),
        name="nsa",
    )(proj, proj, proj, proj, w1b, peb, w2b, biasc, bt, cov, egt)


def _blockdiag2(w):
    z = jnp.zeros_like(w)
    return jnp.concatenate([jnp.concatenate([w, z], axis=-1), jnp.concatenate([z, w], axis=-1)], axis=-2)


def _nsa_params(pe, w1, w2):
    L, Dh = pe.shape[1], pe.shape[2]
    w1b = _blockdiag2(w1.reshape(2, L, Dh, -1)).astype(BF16)
    w2b = _blockdiag2(w2).astype(BF16)
    peb = jnp.concatenate([pe, pe], axis=-1)
    return w1b, peb, w2b


def _t5_bucket_np(n):
    n = np.maximum(n, 0)
    nf = np.maximum(n, 1).astype(np.float64)
    large = 16 + (np.log(nf / 16.0) / math.log(128 / 16) * 16).astype(np.int64)
    return np.where(n < 16, n, np.minimum(large, 31))


def _nsa_tables(rel_bias, S):
    TQ = NSA_TQ
    nq = S // TQ
    rb = rel_bias.astype(F32)

    def tile(dist, valid):
        vals = jnp.where(jnp.asarray(valid)[..., None], rb[jnp.asarray(_t5_bucket_np(dist))], NEG_INF)
        vals = jnp.swapaxes(vals, -1, -2)
        return vals.reshape(vals.shape[:-2] + (NSA_HEADS * TQ,))

    ncmp = S // CMP_STRIDE
    end = np.arange(ncmp) * CMP_STRIDE + CMP_LEN - 1
    cpq = TQ // CMP_STRIDE
    rel_blk = np.arange(cpq * (nq - 1) + ncmp) - cpq * (nq - 1)
    dist_c = np.arange(TQ)[None, :] - (rel_blk[:, None] * CMP_STRIDE + CMP_LEN - 1)
    biasc = tile(dist_c, dist_c >= 0)
    t = np.arange(TQ)[None, :]
    j = np.arange(TQ)[:, None]
    every = np.ones((TQ, TQ), bool)
    bt = jnp.stack([
        tile(t - j, ~every),
        tile(t - j, t >= j),
        tile(t - j + TQ, every),
        tile(t - j + 2 * TQ, every),
        tile(t - j + 2 * TQ, j > t),
    ])
    nslc = S // SLC_BLOCK
    s_lo = np.arange(nslc) * SLC_BLOCK
    start = np.arange(ncmp) * CMP_STRIDE
    cover = ((start[:, None] <= (s_lo + SLC_BLOCK - 1)[None, :]) & (end[:, None] >= s_lo[None, :]))
    cover &= (np.arange(ncmp) < ncmp - 1)[:, None]
    egt = np.zeros((3, NSA_W, LANES), np.float32)
    for jj in range(3):
        egt[jj, np.arange(NSA_W), (np.arange(NSA_W) // NSA_HD) * 3 + jj] = 1.0
    return biasc, bt, jnp.asarray(cover.T.astype(np.float32), BF16), jnp.asarray(egt, BF16)


def _tiles(T, n_ff):
    tm = 512 if T % 512 == 0 else T
    tf = n_ff // 2 if (n_ff // 2) % LANES == 0 else n_ff
    tn = N_PROJ // 4
    return tm, tf, tn


def _pack_w_in(w):
    hg_end = 4 * HG_W
    nsa_end = hg_end + NSA_W + 6 * 2 * NSA_HD + 3 * 8
    rw_end = nsa_end + 3 * RW_W + 256
    d = w.shape[0]
    parts = [w[:, rw_end:], w[:, :hg_end], w[:, hg_end:nsa_end],
             jnp.zeros((d, COL_RW - COL_NSA - (nsa_end - hg_end)), w.dtype), w[:, nsa_end:rw_end],
             jnp.zeros((d, N_PROJ - COL_RW - (rw_end - nsa_end)), w.dtype)]
    return jnp.concatenate(parts, axis=1).astype(BF16)


def kernel(x, p, ffn1_norm, ffn1_wgu, ffn1_wd, mix_norm, w_in, hg_lb, hg_norm, cmp_pe, cmp_w1, cmp_w2, rel_bias, rw_mu, rw_w0, rw_wB, rw_a0, rw_aB, rw_gB, rw_kk, rw_ka, rw_rk, rw_ln_w, rw_ln_b, w_branch, w_out, ffn2_norm, ffn2_wgu, ffn2_wd, ple_norm, ple_gate_w, ple_w, final_norm):
    B, S, D = x.shape
    depth = ffn1_norm.shape[0]
    T = B * S
    assert D == D_MODEL and S // CMP_STRIDE == LANES and S % NSA_TQ == 0
    assert w_in.shape[2] - 3 * D_MODEL == 4 * HG_W + NSA_W + 12 * NSA_HD + 24 + 3 * RW_W + 256
    tm, tf, tn = _tiles(T, ffn1_wd.shape[1])
    consts = _consts()
    tabs = _nsa_tables(rel_bias, S)
    row = lambda v: v.reshape(1, -1)
    h = x.reshape(T, D)
    for i in range(depth):
        h = _ffn(h, row(ffn1_norm[i]), ffn1_wgu[i].astype(BF16), ffn1_wd[i].astype(BF16), tm, tf)
        proj = _proj(h, row(mix_norm[i]), _pack_w_in(w_in[i]), tm, tn)
        o_hg = _hgrn(proj, hg_lb, row(hg_norm[i]), consts["j512"], consts["tri16"], consts["tot16"], B, S, i, 128)
        o_ns = _nsa(proj, *_nsa_params(cmp_pe[i], cmp_w1[i], cmp_w2[i]), tabs, B, S)
        o_rw = _rwkv(proj, *_rwkv_params(rw_mu[i], rw_w0[i], rw_wB[i], rw_a0[i], rw_aB[i], rw_gB[i], rw_kk[i],
                                         rw_ka[i], rw_rk[i], rw_ln_w[i], rw_ln_b[i]),
                     consts["j512"], consts["tri64"], B, S, 4)
        h = _merge(h, proj, o_hg, o_ns, o_rw, w_branch[i].astype(BF16), w_out[i].astype(BF16), tm)
        h = _ffn(h, row(ffn2_norm[i]), ffn2_wgu[i].astype(BF16), ffn2_wd[i].astype(BF16), tm, tf)
        h = _ple(h, row(ple_norm[i]), ple_gate_w[i].astype(BF16), p[i].reshape(T, -1), ple_w[i].astype(BF16),
                 row(final_norm), tm, i == depth - 1)
    return h.reshape(B, S, D)
```

```python
import functools
import math

import jax
import jax.numpy as jnp
import numpy as np
from jax import lax
from jax.experimental import pallas as pl
from jax.experimental.pallas import tpu as pltpu

F32 = jnp.float32
BF16 = jnp.bfloat16

RMS_EPS = 1e-6
LANES = 128
VMEM_LIMIT = 48 * 1024 * 1024

HG_W = 512
NSA_W = 512
RW_W = 512
D_MODEL = 1024
COL_MG = 0
COL_HG = 3072
COL_NSA = 5120
COL_RW = 6656
N_PROJ = 8704


def _cparams(sem):
    return pltpu.CompilerParams(dimension_semantics=sem, vmem_limit_bytes=VMEM_LIMIT)


def _rms(x, g):
    return x * lax.rsqrt(jnp.mean(x * x, axis=-1, keepdims=True) + RMS_EPS) * g


def _sigmoid(x):
    return 1.0 / (1.0 + jnp.exp(-x))


def _silu(x):
    return x * _sigmoid(x)


def _dot(a, b):
    return jnp.dot(a, b, preferred_element_type=F32)


def _nt(a, b):
    return lax.dot_general(a, b, (((1,), (1,)), ((), ())), preferred_element_type=F32)


def _ffn_kernel(h_ref, g_ref, wg_ref, wu_ref, wd_ref, o_ref, xn_ref, acc_ref):
    j = pl.program_id(1)

    @pl.when(j == 0)
    def _():
        xn_ref[...] = _rms(h_ref[...], g_ref[...]).astype(BF16)
        acc_ref[...] = jnp.zeros_like(acc_ref)

    xn = xn_ref[...]
    gate = _dot(xn, wg_ref[...])
    up = _dot(xn, wu_ref[...])
    act = (_silu(gate) * up).astype(BF16)
    acc_ref[...] += _dot(act, wd_ref[...])

    @pl.when(j == pl.num_programs(1) - 1)
    def _():
        o_ref[...] = h_ref[...] + 0.5 * acc_ref[...]


def _ffn(h, g, wgu, wd, tm, tf):
    T, D = h.shape
    FF = wd.shape[0]
    nf = FF // tf
    return pl.pallas_call(
        _ffn_kernel,
        out_shape=jax.ShapeDtypeStruct((T, D), F32),
        grid=(T // tm, nf),
        in_specs=[
            pl.BlockSpec((tm, D), lambda i, j: (i, 0)),
            pl.BlockSpec((1, D), lambda i, j: (0, 0)),
            pl.BlockSpec((D, tf), lambda i, j: (0, j)),
            pl.BlockSpec((D, tf), lambda i, j: (0, j + nf)),
            pl.BlockSpec((tf, D), lambda i, j: (j, 0)),
        ],
        out_specs=pl.BlockSpec((tm, D), lambda i, j: (i, 0)),
        scratch_shapes=[pltpu.VMEM((tm, D), BF16), pltpu.VMEM((tm, D), F32)],
        compiler_params=_cparams(("parallel", "arbitrary")),
        name="ffn",
    )(h, g, wgu, wgu, wd)


def _proj_kernel(h_ref, g_ref, w_ref, o_ref):
    o_ref[...] = _dot(_rms(h_ref[...], g_ref[...]).astype(BF16), w_ref[...])


def _proj(h, g, w, tm, tn):
    T, D = h.shape
    N = w.shape[1]
    return pl.pallas_call(
        _proj_kernel,
        out_shape=jax.ShapeDtypeStruct((T, N), F32),
        grid=(N // tn, T // tm),
        in_specs=[
            pl.BlockSpec((tm, D), lambda j, i: (i, 0)),
            pl.BlockSpec((1, D), lambda j, i: (0, 0)),
            pl.BlockSpec((D, tn), lambda j, i: (0, j)),
        ],
        out_specs=pl.BlockSpec((tm, tn), lambda j, i: (i, j)),
        compiler_params=_cparams(("parallel", "parallel")),
        name="in_proj",
    )(h, g, w)


def _merge_kernel(h_ref, m0_ref, m1_ref, m2_ref, a_ref, b_ref, c_ref, wb_ref, wo_ref, o_ref):
    merged = _sigmoid(m0_ref[...]) * _dot(a_ref[...].astype(BF16), wb_ref[0])
    merged += _sigmoid(m1_ref[...]) * _dot(b_ref[...].astype(BF16), wb_ref[1])
    merged += _sigmoid(m2_ref[...]) * _dot(c_ref[...].astype(BF16), wb_ref[2])
    o_ref[...] = h_ref[...] + _dot(merged.astype(BF16), wo_ref[...])


def _merge(h, proj, o_hg, o_ns, o_rw, wb, wo, tm):
    T, D = h.shape
    W = o_hg.shape[1]
    mg0 = COL_MG // D
    return pl.pallas_call(
        _merge_kernel,
        out_shape=jax.ShapeDtypeStruct((T, D), F32),
        grid=(T // tm,),
        in_specs=[
            pl.BlockSpec((tm, D), lambda i: (i, 0)),
            pl.BlockSpec((tm, D), lambda i: (i, mg0)),
            pl.BlockSpec((tm, D), lambda i: (i, mg0 + 1)),
            pl.BlockSpec((tm, D), lambda i: (i, mg0 + 2)),
            pl.BlockSpec((tm, W), lambda i: (i, 0)),
            pl.BlockSpec((tm, W), lambda i: (i, 0)),
            pl.BlockSpec((tm, W), lambda i: (i, 0)),
            pl.BlockSpec((3, W, D), lambda i: (0, 0, 0)),
            pl.BlockSpec((D, D), lambda i: (0, 0)),
        ],
        out_specs=pl.BlockSpec((tm, D), lambda i: (i, 0)),
        compiler_params=_cparams(("parallel",)),
        name="merge",
    )(h, proj, proj, proj, o_hg, o_ns, o_rw, wb, wo)


def _ple_kernel(h_ref, g_ref, wg_ref, p_ref, wp_ref, fg_ref, o_ref, *, final):
    h = h_ref[...]
    gate = _sigmoid(_dot(_rms(h, g_ref[...]).astype(BF16), wg_ref[...]))
    out = h + gate * _dot(p_ref[...].astype(BF16), wp_ref[...])
    if final:
        out = _rms(out, fg_ref[...])
    o_ref[...] = out


def _ple(h, g, wg, p, wp, fg, tm, final):
    T, D = h.shape
    P = p.shape[1]
    return pl.pallas_call(
        functools.partial(_ple_kernel, final=final),
        out_shape=jax.ShapeDtypeStruct((T, D), F32),
        grid=(T // tm,),
        in_specs=[
            pl.BlockSpec((tm, D), lambda i: (i, 0)),
            pl.BlockSpec((1, D), lambda i: (0, 0)),
            pl.BlockSpec((D, D), lambda i: (0, 0)),
            pl.BlockSpec((tm, P), lambda i: (i, 0)),
            pl.BlockSpec((P, D), lambda i: (0, 0)),
            pl.BlockSpec((1, D), lambda i: (0, 0)),
        ],
        out_specs=pl.BlockSpec((tm, D), lambda i: (i, 0)),
        compiler_params=_cparams(("parallel",)),
        name="ple",
    )(h, g, wg, p, wp, fg)


def _softplus(x):
    return jnp.maximum(x, 0.0) + jnp.log(1.0 + jnp.exp(-jnp.abs(x)))


def _split2(x):
    hi = x.astype(BF16)
    lo = (x - hi.astype(F32)).astype(BF16)
    return hi, lo


def _split3(x):
    hi = x.astype(BF16)
    r1 = x - hi.astype(F32)
    mid = r1.astype(BF16)
    lo = (r1 - mid.astype(F32)).astype(BF16)
    return hi, mid, lo


def _dot3(a, b):
    ah, al = _split2(a)
    bh, bl = _split2(b)
    return _dot(ah, bh) + (_dot(ah, bl) + _dot(al, bh))


def _segsum(x, j):
    hi, lo = _split2(x)
    return _dot(hi, j) + _dot(lo, j)


def _cumsum_rows(tri, x):
    hi, mid, lo = _split3(x)
    return _dot(tri, hi) + (_dot(tri, mid) + _dot(tri, lo))


def _pair_stack(x, lo_mask):
    return jnp.concatenate([jnp.where(lo_mask, x, 0.0), jnp.where(lo_mask, 0.0, x)], axis=0)


def _consts():
    i512 = np.arange(512)
    j512 = (i512[:, None] // 64 == i512[None, :] // 64).astype(np.float32)
    i64 = np.arange(64)
    tri64 = (i64[:, None] >= i64[None, :]).astype(np.float32)
    i128 = np.arange(128)
    same = i128[:, None] // 16 == i128[None, :] // 16
    tri16 = (same & (i128[:, None] >= i128[None, :])).astype(np.float32)
    tot16 = same.astype(np.float32)
    return {"j512": jnp.asarray(j512, BF16), "tri64": jnp.asarray(tri64, BF16),
            "tri16": jnp.asarray(tri16, BF16), "tot16": jnp.asarray(tot16, BF16)}


RW_HD = 64
RW_CH = 64
RW_GN_EPS = 64e-5


def _rwkv_kernel(r_ref, k_ref, v_ref, l_ref, mu_ref, vec_ref, wb_ref, ab_ref, gb_ref, j_ref,
                 tri_ref, o_ref, carry_ref, st_ref, *, nch):
    C = RW_CH
    TC = nch * C
    W = r_ref.shape[1]
    npair = W // LANES

    @pl.when(pl.program_id(1) == 0)
    def _():
        carry_ref[...] = jnp.zeros_like(carry_ref)
        st_ref[...] = jnp.zeros_like(st_ref)

    row = lax.broadcasted_iota(jnp.int32, (TC, W), 0)

    def shift(x_ref, idx):
        x = x_ref[...]
        prev = jnp.where(row == 0, carry_ref[idx:idx + 1, :], pltpu.roll(x, 1, axis=0))
        carry_ref[idx:idx + 1, :] = x[TC - 1:TC, :]
        return x + (prev - x) * mu_ref[idx:idx + 1, :]

    xr = shift(r_ref, 0)
    xk = shift(k_ref, 1)
    xv = shift(v_ref, 2)
    xl = shift(l_ref, 3)
    w0, a0, k_k, k_a = (vec_ref[i:i + 1, :] for i in range(4))
    ln_w, ln_b, r_k = (vec_ref[i:i + 1, :] for i in range(4, 7))
    jmat = j_ref[...]

    wlal = xl[:, 0:LANES]
    w_pre = w0 + _dot(jnp.tanh(wlal).astype(BF16), wb_ref[...])
    a_pre = a0 + _dot(wlal.astype(BF16), ab_ref[...])
    gate = _dot(_sigmoid(xl[:, LANES:2 * LANES]).astype(BF16), gb_ref[...])
    logw = -jnp.exp(-_softplus(-w_pre) - 0.5)
    a = _sigmoid(a_pre)
    kkr = xk * k_k
    kk = kkr / jnp.maximum(jnp.sqrt(_segsum(kkr * kkr, jmat)), 1e-12)
    k2 = xk * (1.0 + (a - 1.0) * k_a)
    ka = kk * a

    lane = lax.broadcasted_iota(jnp.int32, (C, LANES), 1)
    trow = lax.broadcasted_iota(jnp.int32, (C, LANES), 0)
    lo_mask = lane < RW_HD
    scol = lane & (RW_HD - 1)
    strict = trow > scol
    incl = trow >= scol
    eye2 = (trow == scol).astype(F32)
    r128 = lax.broadcasted_iota(jnp.int32, (LANES, LANES), 0)
    c128 = lax.broadcasted_iota(jnp.int32, (LANES, LANES), 1)
    bd_mask = (r128 // RW_HD) == (c128 // RW_HD)
    diag_mask = r128 == c128
    tri = tri_ref[...]

    def bf(x):
        return x.astype(BF16)

    def stack(x):
        return _pair_stack(x, lo_mask)

    ops = []
    for c in range(nch):
        rs = slice(c * C, (c + 1) * C)
        lw = logw[rs]
        b = _cumsum_rows(tri, lw)
        bend = b[C - 1:C, :]
        enb = jnp.exp(-b)
        egc = jnp.exp(bend - b)
        g_end = jnp.exp(bend)
        full = (xr[rs] * jnp.exp(b), k2[rs] * enb, ka[rs] * enb, kk[rs] * jnp.exp(b - lw), k2[rs] * egc,
                ka[rs] * egc, xv[rs], jnp.broadcast_to(g_end, (C, W)))
        for p in range(npair):
            ops.append(tuple(t[:, p * LANES:(p + 1) * LANES] for t in full))
    n = len(ops)
    gms = [_nt(bf(jnp.concatenate([bt, rt], axis=0)), bf(jnp.concatenate([stack(at), stack(kt)], axis=0)))
           for rt, kt, at, bt, _, _, _, _ in ops]
    a_ba = [jnp.where(strict, gm[0:C, 0:LANES], 0.0) for gm in gms]
    a_bk = [jnp.where(strict, gm[0:C, LANES:], 0.0) for gm in gms]
    a_ra = [jnp.where(incl, gm[C:, 0:LANES], 0.0) for gm in gms]
    a_rk = [jnp.where(incl, gm[C:, LANES:], 0.0) for gm in gms]
    pw = [-a for a in a_ba]
    ti = [eye2 + x for x in pw]
    pw = [_dot(bf(x), bf(stack(x))) for x in pw]
    nsq = int(math.log2(C)) - 1
    for k in range(1, nsq):
        both = [_dot(bf(jnp.concatenate([x, t], axis=0)), bf(stack(x))) for x, t in zip(pw, ti)]
        pw = [m[0:C] for m in both]
        ti = [t + m[C:] for t, m in zip(ti, both)]
    ti = [t + _dot(bf(t), bf(stack(x))) for t, x in zip(ti, pw)]
    tib = [bf(t) for t in ti]
    wm = [_dot(tib[j], bf(stack(ops[j][3]))) for j in range(n)]
    av = [_dot(bf(a_bk[j]), bf(stack(ops[j][6]))) for j in range(n)]
    u0 = [_dot(tib[j], bf(stack(av[j]))) for j in range(n)]
    y0 = [_dot(bf(jnp.concatenate([a_rk[j], a_ra[j]], axis=1)),
               bf(jnp.concatenate([stack(ops[j][6]), -stack(u0[j])], axis=0))) for j in range(n)]
    rw = [ops[j][0] - _dot(bf(a_ra[j]), bf(stack(wm[j]))) for j in range(n)]
    m2 = [jnp.where(diag_mask, jnp.concatenate([ops[j][7], ops[j][7]], axis=0), 0.0)
          - jnp.where(bd_mask, _dot(bf(ops[j][5].T), bf(wm[j])), 0.0) for j in range(n)]
    n2 = [jnp.where(bd_mask, _dot(bf(jnp.concatenate([ops[j][4], ops[j][5]], axis=0).T),
                                  bf(jnp.concatenate([ops[j][6], -u0[j]], axis=0))), 0.0) for j in range(n)]
    s2 = [st_ref[p] for p in range(npair)]
    ys = []
    for c in range(nch):
        js = [c * npair + p for p in range(npair)]
        ys.append(jnp.concatenate([y0[j] + _dot(bf(rw[j]), bf(s2[p])) for p, j in enumerate(js)], axis=1))
        s2 = [_dot3(m2[j], s2[p]) + n2[j] for p, j in enumerate(js)]
    for p in range(npair):
        st_ref[p] = s2[p]
    y = jnp.concatenate(ys, axis=0) if nch > 1 else ys[0]

    inv_n = 1.0 / RW_HD
    mean = _segsum(y, jmat) * inv_n
    yc = y - mean
    var = _segsum(yc * yc, jmat) * inv_n
    yn = yc * lax.rsqrt(var + RW_GN_EPS) * ln_w + ln_b
    bonus = _segsum(xr * k2 * r_k, jmat) * xv
    o_ref[...] = (yn + bonus) * gate


def _rwkv(proj, mu, vec, wb, ab, gb, jmat, tri, B, S, nch):
    T = proj.shape[0]
    W = RW_W
    TC = nch * RW_CH
    nblk = S // TC
    c0 = COL_RW // W
    row_map = lambda col: (lambda b, i: (b * nblk + i, col))
    const = lambda b, i: (0, 0)
    return pl.pallas_call(
        functools.partial(_rwkv_kernel, nch=nch),
        out_shape=jax.ShapeDtypeStruct((T, W), F32),
        grid=(B, nblk),
        in_specs=[
            pl.BlockSpec((TC, W), row_map(c0)),
            pl.BlockSpec((TC, W), row_map(c0 + 1)),
            pl.BlockSpec((TC, W), row_map(c0 + 2)),
            pl.BlockSpec((TC, W), row_map(c0 + 3)),
            pl.BlockSpec(mu.shape, const),
            pl.BlockSpec(vec.shape, const),
            pl.BlockSpec(wb.shape, const),
            pl.BlockSpec(ab.shape, const),
            pl.BlockSpec(gb.shape, const),
            pl.BlockSpec(jmat.shape, const),
            pl.BlockSpec(tri.shape, const),
        ],
        out_specs=pl.BlockSpec((TC, W), lambda b, i: (b * nblk + i, 0)),
        scratch_shapes=[pltpu.VMEM((8, W), F32), pltpu.VMEM((W // LANES, LANES, LANES), F32)],
        compiler_params=_cparams(("parallel", "arbitrary")),
        name="rwkv7",
    )(proj, proj, proj, proj, mu, vec, wb, ab, gb, jmat, tri)


def _rwkv_params(mu, w0, wB, a0, aB, gB, k_k, k_a, r_k, ln_w, ln_b):
    W = RW_W
    mu4 = jnp.stack([mu[0:W], mu[W:2 * W], mu[2 * W:3 * W], jnp.pad(mu[3 * W:], (0, W - (mu.shape[0] - 3 * W)))])
    vec = jnp.stack([w0, a0, k_k, k_a, ln_w, ln_b, r_k.reshape(-1), jnp.zeros_like(w0)])
    wb = jnp.pad(wB, ((0, LANES - wB.shape[0]), (0, 0))).astype(BF16)
    ab = jnp.pad(aB, ((LANES - aB.shape[0], 0), (0, 0))).astype(BF16)
    return mu4, vec, wb, ab, gB.astype(BF16)


HG_HD = 64
HG_SUB = 16


def _hgrn_kernel(q_ref, f_ref, i_ref, g_ref, lbp_ref, ng_ref, j_ref, tri_ref, tot_ref, o_ref, st_ref,
                 *, layer):
    TC, W = q_ref.shape
    npair = W // LANES
    nsub = TC // HG_SUB

    @pl.when(pl.program_id(1) == 0)
    def _():
        st_ref[...] = jnp.zeros_like(st_ref)

    z = f_ref[...]
    log_f = -_softplus(-z)
    k = _sigmoid(-z)
    if layer > 0:
        lbp = lbp_ref[...]
        e = jnp.exp(lbp - jnp.max(lbp, axis=0, keepdims=True))
        sm = e / jnp.sum(e, axis=0, keepdims=True)
        lb = sm[1:2, :]
        for j in range(2, layer + 1):
            lb = lb + sm[j:j + 1, :]
        lb = jnp.maximum(lb, 0.0)
        t2 = jnp.log(lb) - _softplus(z)
        log_f = jnp.maximum(log_f, t2) + jnp.log(1.0 + jnp.exp(-jnp.abs(log_f - t2)))
        k = (1.0 - lb) * k
    q = _silu(q_ref[...])
    v = i_ref[...]
    b = _cumsum_rows(tri_ref[...], log_f)
    bend = _cumsum_rows(tot_ref[...], log_f)
    qe = q * jnp.exp(b)
    kg = k * jnp.exp(bend - b)
    jmat = j_ref[...]
    j128 = jmat[0:LANES, 0:LANES]

    rowb = lax.broadcasted_iota(jnp.int32, (TC, LANES), 0)
    trow = lax.broadcasted_iota(jnp.int32, (HG_SUB // 2, W), 0)
    r128 = lax.broadcasted_iota(jnp.int32, (LANES, LANES), 0)
    c128 = lax.broadcasted_iota(jnp.int32, (LANES, LANES), 1)
    bd_mask = (r128 // HG_HD) == (c128 // HG_HD)

    def bf(x):
        return x.astype(BF16)

    vts = [bf(v[:, p * LANES:(p + 1) * LANES].T) for p in range(npair)]
    outs = []
    H8 = HG_SUB // 2
    for i in range(nsub):
        rs = slice(i * HG_SUB, (i + 1) * HG_SUB)
        r0 = i * HG_SUB
        (b_a, q_a, k_a, v_a), (b_b, q_b, k_b, v_b) = (
            tuple(t[r0 + h * H8:r0 + (h + 1) * H8] for t in (b, q, k, v)) for h in range(2))
        xs = []
        for s in range(H8):
            bs, ks = b_a[s:s + 1, :], k_a[s:s + 1, :]
            xs.append(jnp.where(trow >= s, q_a * (ks * jnp.exp(b_a - bs)), 0.0))
            xs.append(q_b * (ks * jnp.exp(b_b - bs)))
        for s in range(H8):
            bs, ks = b_b[s:s + 1, :], k_b[s:s + 1, :]
            xs.append(jnp.where(trow >= s, q_b * (ks * jnp.exp(b_b - bs)), 0.0))
        x = bf(jnp.concatenate(xs, axis=0))
        g_end = jnp.exp(bend[i * HG_SUB:i * HG_SUB + 1, :])
        op = []
        for p in range(npair):
            ls = slice(p * LANES, (p + 1) * LANES)
            pm = _dot(x[:, ls], j128)
            od_a = pm[0:H8] * v_a[0:1, ls]
            od_b = pm[H8:HG_SUB] * v_a[0:1, ls]
            for s in range(1, H8):
                od_a = od_a + pm[s * HG_SUB:s * HG_SUB + H8] * v_a[s:s + 1, ls]
                od_b = od_b + pm[s * HG_SUB + H8:(s + 1) * HG_SUB] * v_a[s:s + 1, ls]
            for s in range(H8):
                od_b = od_b + pm[(HG_SUB + s) * H8:(HG_SUB + s + 1) * H8] * v_b[s:s + 1, ls]
            od = jnp.concatenate([od_a, od_b], axis=0)
            st = st_ref[p]
            oi = lax.dot_general(bf(qe[rs, ls]), bf(st), (((1,), (1,)), ((), ())), preferred_element_type=F32)
            kgm = jnp.where((rowb >= i * HG_SUB) & (rowb < (i + 1) * HG_SUB), kg[:, ls], 0.0)
            st_ref[p] = st * g_end[:, ls] + jnp.where(bd_mask, _dot(vts[p], bf(kgm)), 0.0)
            op.append(od + oi)
        outs.append(jnp.concatenate(op, axis=1))
    o = jnp.concatenate(outs, axis=0)
    ms = _segsum(o * o, jmat) * (1.0 / HG_HD)
    o_ref[...] = o * lax.rsqrt(ms + RMS_EPS) * ng_ref[...] * _silu(g_ref[...])


def _hgrn(proj, lbp, ng, jmat, tri, tot, B, S, layer, tc):
    T = proj.shape[0]
    W = HG_W
    nblk = S // tc
    c0 = COL_HG // W
    row_map = lambda col: (lambda b, i: (b * nblk + i, col))
    const = lambda b, i: (0, 0)
    return pl.pallas_call(
        functools.partial(_hgrn_kernel, layer=layer),
        out_shape=jax.ShapeDtypeStruct((T, W), F32),
        grid=(B, nblk),
        in_specs=[
            pl.BlockSpec((tc, W), row_map(c0)),
            pl.BlockSpec((tc, W), row_map(c0 + 1)),
            pl.BlockSpec((tc, W), row_map(c0 + 2)),
            pl.BlockSpec((tc, W), row_map(c0 + 3)),
            pl.BlockSpec(lbp.shape, const),
            pl.BlockSpec(ng.shape, const),
            pl.BlockSpec(jmat.shape, const),
            pl.BlockSpec(tri.shape, const),
            pl.BlockSpec(tot.shape, const),
        ],
        out_specs=pl.BlockSpec((tc, W), lambda b, i: (b * nblk + i, 0)),
        scratch_shapes=[pltpu.VMEM((W // LANES, LANES, LANES), F32)],
        compiler_params=_cparams(("parallel", "arbitrary")),
        name="hgrn2",
    )(proj, proj, proj, proj, lbp, ng, jmat, tri, tot)


NSA_HD = 64
NSA_HEADS = 8
NSA_TQ = 128
CMP_STRIDE = 16
CMP_LEN = 32
SLC_BLOCK = 64
SLC_TOPN = 8
NEG_INF = float("-inf")
LOG2E = math.log2(math.e)
V_ROWS = NSA_HD + 16


def _nsa_kernel(q_ref, kv1_ref, kv2_ref, gt_ref, w1_ref, pe_ref, w2_ref, biasc_ref, bt_ref, cov_ref, egt_ref,
                o_ref, kc_ref, vct_ref, k_ref, vt_ref, am_ref, m_ref, acc_ref, cx_ref, s_ref, p_ref, al_ref, sn_ref):
    TQ = NSA_TQ
    S = kv1_ref.shape[0]
    nkc = S // TQ
    ncmp = S // CMP_STRIDE
    nslc = S // SLC_BLOCK
    GW = (NSA_HEADS // 2) * TQ
    i = pl.program_id(1)

    def with_ones(vt_g):
        return jnp.concatenate([vt_g, jnp.ones((V_ROWS - NSA_HD, vt_g.shape[1]), F32)], axis=0).astype(BF16)

    @pl.when(i == 0)
    def _():
        k_ref[0] = kv1_ref[:, 2 * LANES:3 * LANES].astype(BF16)
        k_ref[1] = kv2_ref[:, 0:LANES].astype(BF16)
        for src, (ref, c0) in enumerate(((kv1_ref, 3), (kv2_ref, 1))):
            for c in range(nkc):
                vt = ref[c * TQ:(c + 1) * TQ, c0 * LANES:(c0 + 1) * LANES].T
                for g in range(2):
                    vt_ref[src, g, c] = with_ones(vt[g * NSA_HD:(g + 1) * NSA_HD])
        for t in range(2):
            cx_ref[...] = kv1_ref[:, t * LANES:(t + 1) * LANES]
            acc_a = jnp.zeros((ncmp, 2 * LANES), F32)
            acc_b = jnp.zeros((ncmp, 2 * LANES), F32)
            for l in range(CMP_STRIDE):
                xl = cx_ref[pl.ds(l, ncmp, stride=CMP_STRIDE), :]
                acc_a += _dot((xl + pe_ref[t, l:l + 1, :]).astype(BF16), w1_ref[t, l])
                acc_b += _dot((xl + pe_ref[t, CMP_STRIDE + l:CMP_STRIDE + l + 1, :]).astype(BF16),
                              w1_ref[t, CMP_STRIDE + l])
            hid = _silu(acc_a + pltpu.roll(acc_b, ncmp - 1, axis=0))
            cmp = _dot(hid.astype(BF16), w2_ref[t])
            if t == 0:
                kc_ref[...] = cmp.astype(BF16)
            else:
                ct = cmp.T
                for g in range(2):
                    vct_ref[g] = with_ones(ct[g * NSA_HD:(g + 1) * NSA_HD])

    qs = q_ref[...] * (LOG2E * NSA_HD ** -0.5)
    zero = jnp.zeros((NSA_HD, TQ), F32)
    cols = []
    for pp in range(NSA_HEADS // 2):
        qt = qs[:, pp * LANES:(pp + 1) * LANES].T
        for e in range(2):
            qh = qt[e * NSA_HD:(e + 1) * NSA_HD]
            cols.append(jnp.concatenate([qh, zero] if pp < 2 else [zero, qh], axis=0))
    wq = jnp.concatenate(cols, axis=1).astype(BF16)

    def group_rows(o_t):
        return jnp.concatenate([o_t[:, hh * TQ:(hh + 1) * TQ] for hh in range(NSA_HEADS // 2)], axis=0)

    def raw_scores(br, kc):
        return _dot(k_ref[br, pl.ds(pl.multiple_of(kc * TQ, TQ), TQ), :], wq)

    cpq = TQ // CMP_STRIDE
    sc = _dot(kc_ref[...], wq) + biasc_ref[pl.ds(pl.multiple_of(cpq * (pl.num_programs(1) - 1 - i), cpq), ncmp), :]
    near = [i - 2, i - 1, i]
    near_kc = [jnp.maximum(kc, 0) for kc in near]
    for t, kc in enumerate(near_kc):
        for br in range(2):
            sn_ref[br, t] = raw_scores(br, kc)
    for u in range(2):
        s_ref[0, u] = raw_scores(0, u)

    mx = jnp.max(sc, axis=0, keepdims=True)
    ex = jnp.exp2(sc - jnp.where(mx == NEG_INF, 0.0, mx))
    p = ex * (1.0 / jnp.maximum(jnp.sum(ex, axis=0, keepdims=True), 1e-30))
    o_cmp = []
    for g in range(2):
        pg = p[:, g * GW:(g + 1) * GW]
        o_cmp.append(group_rows(_dot(vct_ref[g, 0:NSA_HD, :], pg.astype(BF16))))
        psum = pg[:, 0:TQ]
        for hh in range(1, NSA_HEADS // 2):
            psum = psum + pg[:, hh * TQ:(hh + 1) * TQ]
        hi, lw = _split2(psum)
        imp_t = _dot(cov_ref[...], hi) + _dot(cov_ref[...], lw)
        blk = lax.broadcasted_iota(jnp.int32, (nslc, TQ), 0)
        cur = (i * TQ + lax.broadcasted_iota(jnp.int32, (nslc, TQ), 1)) // SLC_BLOCK
        forced = (blk == 0) | (blk == cur) | (blk == cur - 1)
        score = jnp.where(forced, jnp.inf, jnp.where(blk <= cur, imp_t, NEG_INF))
        cnt = jnp.zeros((nslc, TQ), F32)
        for mp in range(nslc):
            sm = score[mp:mp + 1, :]
            ahead = (sm > score) | ((sm == score) & (blk > mp))
            cnt = cnt + jnp.where(ahead, 1.0, 0.0)
        am_ref[g, 0:nslc] = jnp.where(cnt < SLC_TOPN, 0.0, NEG_INF)
        am_ref[g, nslc:] = jnp.full((am_ref.shape[1] - nslc, TQ), NEG_INF, F32)

    def reset(br):
        m_ref[br] = jnp.full(m_ref.shape[1:], NEG_INF, F32)
        acc_ref[br] = jnp.zeros(acc_ref.shape[1:], F32)

    def block_mask(r0):
        mk = []
        for g in range(2):
            halves = [jnp.broadcast_to(am_ref[g, pl.ds(r0 + u, 1), :], (SLC_BLOCK, TQ)) for u in range(2)]
            mk.append(jnp.concatenate(halves, axis=0))
        return jnp.concatenate([mk[0]] * (NSA_HEADS // 2) + [mk[1]] * (NSA_HEADS // 2), axis=1)

    def softmax_update(br, ss):
        m_prev = m_ref[br]
        m_new = m_prev
        for s in ss:
            m_new = jnp.maximum(m_new, jnp.max(s, axis=0, keepdims=True))
        m_safe = jnp.where(m_new == NEG_INF, 0.0, m_new)
        m_ref[br] = m_new
        return jnp.exp2(m_prev - m_safe), [jnp.exp2(s - m_safe).astype(BF16) for s in ss]

    def value_update(br, alpha, kcs, pts):
        for g in range(2):
            upd = alpha[:, g * GW:(g + 1) * GW] * acc_ref[br, g]
            for kc, pt in zip(kcs, pts):
                upd = upd + _dot(vt_ref[br, g, kc], pt[:, g * GW:(g + 1) * GW])
            acc_ref[br, g] = upd

    def result(br):
        outs = []
        for g in range(2):
            acc = acc_ref[br, g]
            inv = 1.0 / jnp.maximum(acc[NSA_HD:NSA_HD + 1, :], 1e-30)
            outs.append(group_rows(acc[0:NSA_HD, :] * inv))
        return outs

    reset(1)
    win_tiles = [jnp.where(kc < 0, 0, tile) for kc, tile in zip(near, (4, 2, 1))]
    alpha, pts = softmax_update(1, [sn_ref[1, t] + bt_ref[win_tiles[t]] for t in range(3)])
    value_update(1, alpha, near_kc, pts)

    reset(0)
    nfar = jnp.maximum(i - 1, 0)
    nfp = nfar // 2
    tail_tiles = [jnp.where((near[0] >= 0) & (nfar - 2 * nfp == 1), 3, 0), jnp.where(near[1] >= 0, 2, 0), 1]
    alpha, pts = softmax_update(0, [sn_ref[0, t] + bt_ref[tail_tiles[t]] + block_mask(2 * near_kc[t])
                                    for t in range(3)])
    value_update(0, alpha, near_kc, pts)

    def pair_scores(pr, slot):
        for u in range(2):
            s_ref[slot, u] = raw_scores(0, jnp.clip(2 * pr + u, 0, nkc - 1))

    def pair_values(pr, slot):
        value_update(0, al_ref[slot], [jnp.clip(2 * pr + u, 0, nkc - 1) for u in range(2)],
                     [p_ref[slot, u] for u in range(2)])

    def pair_step(pr, slot):
        pair_scores(pr + 1, 1 - slot)
        pair_values(pr - 1, 1 - slot)
        rows = [jnp.where(pr < nfp, 2 * (2 * pr + u), nslc) for u in range(2)]
        alpha, pts = softmax_update(0, [s_ref[slot, u] + block_mask(rows[u]) for u in range(2)])
        al_ref[slot] = alpha
        for u in range(2):
            p_ref[slot, u] = pts[u]

    p_ref[1] = jnp.zeros(p_ref.shape[1:], BF16)
    al_ref[1] = jnp.ones(al_ref.shape[1:], F32)

    def two_pairs(q2, carry):
        pair_step(2 * q2, 0)
        pair_step(2 * q2 + 1, 1)
        return carry

    ntrip = (nfp + 1) // 2
    lax.fori_loop(0, ntrip, two_pairs, 0)
    pair_values(2 * ntrip - 1, 1)

    o_slc = result(0)
    o_win = result(1)

    sig_hi, sig_lo = _split2(_sigmoid(gt_ref[...]).T)
    out_t = None
    for j, parts in enumerate((o_cmp, o_slc, o_win)):
        gate_t = _dot(egt_ref[j], sig_hi) + _dot(egt_ref[j], sig_lo)
        term = gate_t * jnp.concatenate(parts, axis=0)
        out_t = term if out_t is None else out_t + term
    o_ref[...] = jnp.concatenate([out_t[c * LANES:(c + 1) * LANES].T for c in range(NSA_W // LANES)], axis=1)


def _nsa(proj, w1b, peb, w2b, tabs, B, S):
    T = proj.shape[0]
    W = NSA_W
    TQ = NSA_TQ
    nq = S // TQ
    HW = NSA_HEADS * TQ
    c0 = COL_NSA // W
    cg = (COL_NSA + 2 * W + 2 * LANES) // LANES
    biasc, bt, cov, egt = tabs

    def const(shape):
        return pl.BlockSpec(shape, lambda b, i, _n=len(shape): (0,) * _n)

    return pl.pallas_call(
        _nsa_kernel,
        out_shape=jax.ShapeDtypeStruct((T, W), F32),
        grid=(B, nq),
        in_specs=[
            pl.BlockSpec((TQ, W), lambda b, i: (b * nq + i, c0)),
            pl.BlockSpec((S, W), lambda b, i: (b, c0 + 1)),
            pl.BlockSpec((S, 2 * LANES), lambda b, i: (b, (COL_NSA + 2 * W) // (2 * LANES))),
            pl.BlockSpec((TQ, LANES), lambda b, i: (b * nq + i, cg)),
            const(w1b.shape), const(peb.shape), const(w2b.shape),
            const(biasc.shape), const(bt.shape), const(cov.shape), const(egt.shape),
        ],
        out_specs=pl.BlockSpec((TQ, W), lambda b, i: (b * nq + i, 0)),
        scratch_shapes=[
            pltpu.VMEM((S // CMP_STRIDE, LANES), BF16),
            pltpu.VMEM((2, V_ROWS, S // CMP_STRIDE), BF16),
            pltpu.VMEM((2, S, LANES), BF16),
            pltpu.VMEM((2, 2, S // TQ, V_ROWS, TQ), BF16),
            pltpu.VMEM((2, S // SLC_BLOCK + 8, TQ), F32),
            pltpu.VMEM((2, 1, HW), F32),
            pltpu.VMEM((2, 2, V_ROWS, HW // 2), F32),
            pltpu.VMEM((S, LANES), F32),
            pltpu.VMEM((2, 2, TQ, HW), F32),
            pltpu.VMEM((2, 2, TQ, HW), BF16),
            pltpu.VMEM((2, 1, HW), F32),
            pltpu.VMEM((2, 3, TQ, HW), F32),
        ],
        compiler_params=_cparams(("parallel", "arbitrary")),
        name="nsa",
    )(proj, proj, proj, proj, w1b, peb, w2b, biasc, bt, cov, egt)


def _blockdiag2(w):
    z = jnp.zeros_like(w)
    return jnp.concatenate([jnp.concatenate([w, z], axis=-1), jnp.concatenate([z, w], axis=-1)], axis=-2)


def _nsa_params(pe, w1, w2):
    L, Dh = pe.shape[1], pe.shape[2]
    w1b = _blockdiag2(w1.reshape(2, L, Dh, -1)).astype(BF16)
    w2b = _blockdiag2(w2).astype(BF16)
    peb = jnp.concatenate([pe, pe], axis=-1)
    return w1b, peb, w2b


def _t5_bucket_np(n):
    n = np.maximum(n, 0)
    nf = np.maximum(n, 1).astype(np.float64)
    large = 16 + (np.log(nf / 16.0) / math.log(128 / 16) * 16).astype(np.int64)
    return np.where(n < 16, n, np.minimum(large, 31))


def _nsa_tables(rel_bias, S):
    TQ = NSA_TQ
    nq = S // TQ
    rb = rel_bias.astype(F32) * LOG2E

    def tile(dist, valid, shift=False):
        vals = rb[jnp.asarray(_t5_bucket_np(dist))]
        if shift:
            vals = vals - rb[_t5_bucket_np(np.asarray(2 * TQ))]
        vals = jnp.where(jnp.asarray(valid)[..., None], vals, NEG_INF)
        vals = jnp.swapaxes(vals, -1, -2)
        return vals.reshape(vals.shape[:-2] + (NSA_HEADS * TQ,))

    ncmp = S // CMP_STRIDE
    end = np.arange(ncmp) * CMP_STRIDE + CMP_LEN - 1
    cpq = TQ // CMP_STRIDE
    rel_blk = np.arange(cpq * (nq - 1) + ncmp) - cpq * (nq - 1)
    dist_c = np.arange(TQ)[None, :] - (rel_blk[:, None] * CMP_STRIDE + CMP_LEN - 1)
    biasc = tile(dist_c, dist_c >= 0)
    t = np.arange(TQ)[None, :]
    j = np.arange(TQ)[:, None]
    every = np.ones((TQ, TQ), bool)
    bt = jnp.stack([
        tile(t - j, ~every, True),
        tile(t - j, t >= j, True),
        tile(t - j + TQ, every, True),
        tile(t - j + 2 * TQ, every, True),
        tile(t - j + 2 * TQ, j > t, True),
    ])
    assert _t5_bucket_np(np.asarray(TQ + 1)) == _t5_bucket_np(np.asarray(S))
    nslc = S // SLC_BLOCK
    s_lo = np.arange(nslc) * SLC_BLOCK
    start = np.arange(ncmp) * CMP_STRIDE
    cover = ((start[:, None] <= (s_lo + SLC_BLOCK - 1)[None, :]) & (end[:, None] >= s_lo[None, :]))
    cover &= (np.arange(ncmp) < ncmp - 1)[:, None]
    egt = np.zeros((3, NSA_W, LANES), np.float32)
    for jj in range(3):
        egt[jj, np.arange(NSA_W), (np.arange(NSA_W) // NSA_HD) * 3 + jj] = 1.0
    return biasc, bt, jnp.asarray(cover.T.astype(np.float32), BF16), jnp.asarray(egt, BF16)


def _tiles(T, n_ff):
    tm = 512 if T % 512 == 0 else T
    tf = n_ff // 2 if (n_ff // 2) % LANES == 0 else n_ff
    tn = N_PROJ // 4
    return tm, tf, tn


def _pack_w_in(w):
    hg_end = 4 * HG_W
    nsa_end = hg_end + NSA_W + 6 * 2 * NSA_HD + 3 * 8
    rw_end = nsa_end + 3 * RW_W + 256
    d = w.shape[0]
    parts = [w[:, rw_end:], w[:, :hg_end], w[:, hg_end:nsa_end],
             jnp.zeros((d, COL_RW - COL_NSA - (nsa_end - hg_end)), w.dtype), w[:, nsa_end:rw_end],
             jnp.zeros((d, N_PROJ - COL_RW - (rw_end - nsa_end)), w.dtype)]
    return jnp.concatenate(parts, axis=1).astype(BF16)


def kernel(x, p, ffn1_norm, ffn1_wgu, ffn1_wd, mix_norm, w_in, hg_lb, hg_norm, cmp_pe, cmp_w1, cmp_w2, rel_bias, rw_mu, rw_w0, rw_wB, rw_a0, rw_aB, rw_gB, rw_kk, rw_ka, rw_rk, rw_ln_w, rw_ln_b, w_branch, w_out, ffn2_norm, ffn2_wgu, ffn2_wd, ple_norm, ple_gate_w, ple_w, final_norm):
    B, S, D = x.shape
    depth = ffn1_norm.shape[0]
    T = B * S
    assert D == D_MODEL and S // CMP_STRIDE == LANES and S % NSA_TQ == 0
    assert w_in.shape[2] - 3 * D_MODEL == 4 * HG_W + NSA_W + 12 * NSA_HD + 24 + 3 * RW_W + 256
    tm, tf, tn = _tiles(T, ffn1_wd.shape[1])
    consts = _consts()
    tabs = _nsa_tables(rel_bias, S)
    row = lambda v: v.reshape(1, -1)
    h = x.reshape(T, D)
    for i in range(depth):
        h = _ffn(h, row(ffn1_norm[i]), ffn1_wgu[i].astype(BF16), ffn1_wd[i].astype(BF16), tm, tf)
        proj = _proj(h, row(mix_norm[i]), _pack_w_in(w_in[i]), tm, tn)
        o_hg = _hgrn(proj, hg_lb, row(hg_norm[i]), consts["j512"], consts["tri16"], consts["tot16"], B, S, i, 128)
        o_ns = _nsa(proj, *_nsa_params(cmp_pe[i], cmp_w1[i], cmp_w2[i]), tabs, B, S)
        o_rw = _rwkv(proj, *_rwkv_params(rw_mu[i], rw_w0[i], rw_wB[i], rw_a0[i], rw_aB[i], rw_gB[i], rw_kk[i],
                                         rw_ka[i], rw_rk[i], rw_ln_w[i], rw_ln_b[i]),
                     consts["j512"], consts["tri64"], B, S, 4)
        h = _merge(h, proj, o_hg, o_ns, o_rw, w_branch[i].astype(BF16), w_out[i].astype(BF16), tm)
        h = _ffn(h, row(ffn2_norm[i]), ffn2_wgu[i].astype(BF16), ffn2_wd[i].astype(BF16), tm, tf)
        h = _ple(h, row(ple_norm[i]), ple_gate_w[i].astype(BF16), p[i].reshape(T, -1), ple_w[i].astype(BF16),
                 row(final_norm), tm, i == depth - 1)
    return h.reshape(B, S, D)
```

```python
import functools
import math

import jax
import jax.numpy as jnp
import numpy as np
from jax import lax
from jax.experimental import pallas as pl
from jax.experimental.pallas import tpu as pltpu

F32 = jnp.float32
BF16 = jnp.bfloat16

RMS_EPS = 1e-6
LOG2E = math.log2(math.e)
LANES = 128
VMEM_LIMIT = 48 * 1024 * 1024

HG_W = 512
NSA_W = 512
RW_W = 512
D_MODEL = 1024
COL_MG = 0
COL_HG = 3072
COL_NSA = 5120
COL_RW = 6656
N_PROJ = 8704


def _cparams(sem):
    return pltpu.CompilerParams(dimension_semantics=sem, vmem_limit_bytes=VMEM_LIMIT)


def _rms(x, g):
    return x * lax.rsqrt(jnp.mean(x * x, axis=-1, keepdims=True) + RMS_EPS) * g


def _sigmoid(x):
    return 1.0 / (1.0 + jnp.exp(-x))


def _silu(x):
    return x * _sigmoid(x)


def _dot(a, b):
    return jnp.dot(a, b, preferred_element_type=F32)


def _nt(a, b):
    return lax.dot_general(a, b, (((1,), (1,)), ((), ())), preferred_element_type=F32)


def _ffn_kernel(h_ref, g_ref, wg_ref, wu_ref, wd_ref, o_ref, xn_ref, acc_ref):
    j = pl.program_id(1)

    @pl.when(j == 0)
    def _():
        xn_ref[...] = _rms(h_ref[...], g_ref[...]).astype(BF16)
        acc_ref[...] = jnp.zeros_like(acc_ref)

    xn = xn_ref[...]
    gate = _dot(xn, wg_ref[...])
    up = _dot(xn, wu_ref[...])
    act = (_silu(gate) * up).astype(BF16)
    acc_ref[...] += _dot(act, wd_ref[...])

    @pl.when(j == pl.num_programs(1) - 1)
    def _():
        o_ref[...] = h_ref[...] + 0.5 * acc_ref[...]


def _ffn(h, g, wgu, wd, layer, tm, tf):
    T, D = h.shape
    FF = wd.shape[1]
    nf = FF // tf
    return pl.pallas_call(
        _ffn_kernel,
        out_shape=jax.ShapeDtypeStruct((T, D), F32),
        grid=(T // tm, nf),
        in_specs=[
            pl.BlockSpec((tm, D), lambda i, j: (i, 0)),
            pl.BlockSpec((1, D), lambda i, j: (0, 0)),
            pl.BlockSpec((None, D, tf), lambda i, j: (layer, 0, j)),
            pl.BlockSpec((None, D, tf), lambda i, j: (layer, 0, j + nf)),
            pl.BlockSpec((None, tf, D), lambda i, j: (layer, j, 0)),
        ],
        out_specs=pl.BlockSpec((tm, D), lambda i, j: (i, 0)),
        scratch_shapes=[pltpu.VMEM((tm, D), BF16), pltpu.VMEM((tm, D), F32)],
        compiler_params=_cparams(("parallel", "arbitrary")),
        name="ffn",
    )(h, g, wgu, wgu, wd)


def _proj_kernel(h_ref, g_ref, w_ref, o_ref):
    o_ref[...] = _dot(_rms(h_ref[...], g_ref[...]).astype(BF16), w_ref[...])


def _proj(h, g, w, layer, tm, tn):
    T, D = h.shape
    N = w.shape[2]
    return pl.pallas_call(
        _proj_kernel,
        out_shape=jax.ShapeDtypeStruct((T, N), F32),
        grid=(N // tn, T // tm),
        in_specs=[
            pl.BlockSpec((tm, D), lambda j, i: (i, 0)),
            pl.BlockSpec((1, D), lambda j, i: (0, 0)),
            pl.BlockSpec((None, D, tn), lambda j, i: (layer, 0, j)),
        ],
        out_specs=pl.BlockSpec((tm, tn), lambda j, i: (i, j)),
        compiler_params=_cparams(("parallel", "parallel")),
        name="in_proj",
    )(h, g, w)


def _merge_kernel(h_ref, m0_ref, m1_ref, m2_ref, a_ref, b_ref, c_ref, wb_ref, wo_ref, o_ref):
    merged = _sigmoid(m0_ref[...]) * _dot(a_ref[...].astype(BF16), wb_ref[0])
    merged += _sigmoid(m1_ref[...]) * _dot(b_ref[...].astype(BF16), wb_ref[1])
    merged += _sigmoid(m2_ref[...]) * _dot(c_ref[...].astype(BF16), wb_ref[2])
    o_ref[...] = h_ref[...] + _dot(merged.astype(BF16), wo_ref[...])


def _merge(h, proj, o_hg, o_ns, o_rw, wb, wo, layer, tm):
    T, D = h.shape
    W = o_hg.shape[1]
    mg0 = COL_MG // D
    return pl.pallas_call(
        _merge_kernel,
        out_shape=jax.ShapeDtypeStruct((T, D), F32),
        grid=(T // tm,),
        in_specs=[
            pl.BlockSpec((tm, D), lambda i: (i, 0)),
            pl.BlockSpec((tm, D), lambda i: (i, mg0)),
            pl.BlockSpec((tm, D), lambda i: (i, mg0 + 1)),
            pl.BlockSpec((tm, D), lambda i: (i, mg0 + 2)),
            pl.BlockSpec((tm, W), lambda i: (i, 0)),
            pl.BlockSpec((tm, W), lambda i: (i, 0)),
            pl.BlockSpec((tm, W), lambda i: (i, 0)),
            pl.BlockSpec((None, 3, W, D), lambda i: (layer, 0, 0, 0)),
            pl.BlockSpec((None, D, D), lambda i: (layer, 0, 0)),
        ],
        out_specs=pl.BlockSpec((tm, D), lambda i: (i, 0)),
        compiler_params=_cparams(("parallel",)),
        name="merge",
    )(h, proj, proj, proj, o_hg, o_ns, o_rw, wb, wo)


def _ple_kernel(h_ref, g_ref, wg_ref, p_ref, wp_ref, fg_ref, o_ref, *, final):
    h = h_ref[...]
    gate = _sigmoid(_dot(_rms(h, g_ref[...]).astype(BF16), wg_ref[...]))
    out = h + gate * _dot(p_ref[...].astype(BF16), wp_ref[...])
    if final:
        out = _rms(out, fg_ref[...])
    o_ref[...] = out


def _ple(h, g, wg, p, wp, fg, layer, tm, final):
    T, D = h.shape
    P = p.shape[2]
    return pl.pallas_call(
        functools.partial(_ple_kernel, final=final),
        out_shape=jax.ShapeDtypeStruct((T, D), F32),
        grid=(T // tm,),
        in_specs=[
            pl.BlockSpec((tm, D), lambda i: (i, 0)),
            pl.BlockSpec((1, D), lambda i: (0, 0)),
            pl.BlockSpec((None, D, D), lambda i: (layer, 0, 0)),
            pl.BlockSpec((None, tm, P), lambda i: (layer, i, 0)),
            pl.BlockSpec((None, P, D), lambda i: (layer, 0, 0)),
            pl.BlockSpec((1, D), lambda i: (0, 0)),
        ],
        out_specs=pl.BlockSpec((tm, D), lambda i: (i, 0)),
        compiler_params=_cparams(("parallel",)),
        name="ple",
    )(h, g, wg, p, wp, fg)


def _softplus(x):
    return jnp.maximum(x, 0.0) + jnp.log(1.0 + jnp.exp(-jnp.abs(x)))


def _split2(x):
    hi = x.astype(BF16)
    lo = (x - hi.astype(F32)).astype(BF16)
    return hi, lo


def _split3(x):
    hi = x.astype(BF16)
    r1 = x - hi.astype(F32)
    mid = r1.astype(BF16)
    lo = (r1 - mid.astype(F32)).astype(BF16)
    return hi, mid, lo


def _segsum(x, j):
    hi, lo = _split2(x)
    return _dot(hi, j) + _dot(lo, j)


def _cumsum_rows(tri, x):
    hi, mid, lo = _split3(x)
    return _dot(tri, hi) + (_dot(tri, mid) + _dot(tri, lo))


def _pair_stack(x, lo_mask):
    return jnp.concatenate([jnp.where(lo_mask, x, 0.0), jnp.where(lo_mask, 0.0, x)], axis=0)


def _consts():
    i512 = np.arange(512)
    j512 = (i512[:, None] // 64 == i512[None, :] // 64).astype(np.float32)
    i64 = np.arange(64)
    tri64 = (i64[:, None] >= i64[None, :]).astype(np.float32)
    i128 = np.arange(128)
    same = i128[:, None] // 16 == i128[None, :] // 16
    tri16 = (same & (i128[:, None] >= i128[None, :])).astype(np.float32)
    tot16 = same.astype(np.float32)
    return {"j512": jnp.asarray(j512, BF16), "tri64": jnp.asarray(tri64, BF16),
            "tri16": jnp.asarray(tri16, BF16), "tot16": jnp.asarray(tot16, BF16)}


RW_HD = 64
RW_CH = 64
RW_GN_EPS = 64e-5


def _rwkv_kernel(r_ref, k_ref, v_ref, l_ref, mu_ref, vec_ref, wb_ref, ab_ref, gb_ref, j_ref,
                 tri_ref, o_ref, carry_ref, st_ref, *, nch):
    C = RW_CH
    TC = nch * C
    W = r_ref.shape[1]
    npair = W // LANES

    @pl.when(pl.program_id(1) == 0)
    def _():
        carry_ref[...] = jnp.zeros_like(carry_ref)
        st_ref[...] = jnp.zeros_like(st_ref)

    row = lax.broadcasted_iota(jnp.int32, (TC, W), 0)

    def shift(x_ref, idx):
        x = x_ref[...]
        prev = jnp.where(row == 0, carry_ref[idx:idx + 1, :], pltpu.roll(x, 1, axis=0))
        carry_ref[idx:idx + 1, :] = x[TC - 1:TC, :]
        return x + (prev - x) * mu_ref[idx:idx + 1, :]

    xr = shift(r_ref, 0)
    xk = shift(k_ref, 1)
    xv = shift(v_ref, 2)
    xl = shift(l_ref, 3)
    w0, a0, k_k, k_a = (vec_ref[i:i + 1, :] for i in range(4))
    ln_w, ln_b, r_k = (vec_ref[i:i + 1, :] for i in range(4, 7))
    jmat = j_ref[...]

    wlal = xl[:, 0:LANES]
    w_pre = w0 + _dot(jnp.tanh(wlal).astype(BF16), wb_ref[...])
    a_pre = a0 + _dot(wlal.astype(BF16), ab_ref[...])
    gate = _dot(_sigmoid(xl[:, LANES:2 * LANES]).astype(BF16), gb_ref[...])
    logw = -jnp.exp(-_softplus(-w_pre) - 0.5)
    a = _sigmoid(a_pre)
    kkr = xk * k_k
    kk = kkr / jnp.maximum(jnp.sqrt(_segsum(kkr * kkr, jmat)), 1e-12)
    k2 = xk * (1.0 + (a - 1.0) * k_a)
    ka = kk * a

    lane = lax.broadcasted_iota(jnp.int32, (C, LANES), 1)
    trow = lax.broadcasted_iota(jnp.int32, (C, LANES), 0)
    lo_mask = lane < RW_HD
    scol = lane & (RW_HD - 1)
    strict = trow > scol
    incl = trow >= scol
    eye2 = (trow == scol).astype(F32)
    r128 = lax.broadcasted_iota(jnp.int32, (LANES, LANES), 0)
    c128 = lax.broadcasted_iota(jnp.int32, (LANES, LANES), 1)
    bd_mask = (r128 // RW_HD) == (c128 // RW_HD)
    diag_mask = r128 == c128
    tri = tri_ref[...]

    def bf(x):
        return x.astype(BF16)

    def stack(x):
        return _pair_stack(x, lo_mask)

    ops = []
    for c in range(nch):
        rs = slice(c * C, (c + 1) * C)
        lw = logw[rs] * LOG2E
        b = _cumsum_rows(tri, lw)
        bend = b[C - 1:C, :]
        enb = jnp.exp2(-b)
        egc = jnp.exp2(bend - b)
        g_end = jnp.exp2(bend)
        full = (xr[rs] * jnp.exp2(b), k2[rs] * enb, ka[rs] * enb, kk[rs] * jnp.exp2(b - lw), k2[rs] * egc,
                ka[rs] * egc, xv[rs], jnp.broadcast_to(g_end, (C, W)))
        for p in range(npair):
            ops.append(tuple(t[:, p * LANES:(p + 1) * LANES] for t in full))
    n = len(ops)
    gms = [_nt(bf(jnp.concatenate([bt, rt], axis=0)), bf(jnp.concatenate([stack(at), stack(kt)], axis=0)))
           for rt, kt, at, bt, _, _, _, _ in ops]
    a_ba = [jnp.where(strict, gm[0:C, 0:LANES], 0.0) for gm in gms]
    a_bk = [jnp.where(strict, gm[0:C, LANES:], 0.0) for gm in gms]
    a_ra = [jnp.where(incl, gm[C:, 0:LANES], 0.0) for gm in gms]
    a_rk = [jnp.where(incl, gm[C:, LANES:], 0.0) for gm in gms]
    pw = [-a for a in a_ba]
    ti = [eye2 + x for x in pw]
    pw = [_dot(bf(x), bf(stack(x))) for x in pw]
    nsq = int(math.log2(C)) - 1
    for k in range(1, nsq):
        both = [_dot(bf(jnp.concatenate([x, t], axis=0)), bf(stack(x))) for x, t in zip(pw, ti)]
        pw = [m[0:C] for m in both]
        ti = [t + m[C:] for t, m in zip(ti, both)]
    ti = [t + _dot(bf(t), bf(stack(x))) for t, x in zip(ti, pw)]
    tib = [bf(t) for t in ti]
    wm = [_dot(tib[j], bf(stack(ops[j][3]))) for j in range(n)]
    av = [_dot(bf(a_bk[j]), bf(stack(ops[j][6]))) for j in range(n)]
    u0 = [_dot(tib[j], bf(stack(av[j]))) for j in range(n)]
    y0 = [_dot(bf(jnp.concatenate([a_rk[j], a_ra[j]], axis=1)),
               bf(jnp.concatenate([stack(ops[j][6]), -stack(u0[j])], axis=0))) for j in range(n)]
    rw = [ops[j][0] - _dot(bf(a_ra[j]), bf(stack(wm[j]))) for j in range(n)]
    m2 = [jnp.where(diag_mask, jnp.concatenate([ops[j][7], ops[j][7]], axis=0), 0.0)
          - jnp.where(bd_mask, _dot(bf(ops[j][5].T), bf(wm[j])), 0.0) for j in range(n)]
    n2 = [jnp.where(bd_mask, _dot(bf(jnp.concatenate([ops[j][4], ops[j][5]], axis=0).T),
                                  bf(jnp.concatenate([ops[j][6], -u0[j]], axis=0))), 0.0) for j in range(n)]
    s2 = [st_ref[p] for p in range(npair)]
    ys = []
    for c in range(nch):
        js = [c * npair + p for p in range(npair)]
        ys.append(jnp.concatenate([y0[j] + _dot(bf(rw[j]), bf(s2[p])) for p, j in enumerate(js)], axis=1))
        s2 = [_dot(bf(m2[j]), bf(s2[p])) + n2[j] for p, j in enumerate(js)]
    for p in range(npair):
        st_ref[p] = s2[p]
    y = jnp.concatenate(ys, axis=0) if nch > 1 else ys[0]

    inv_n = 1.0 / RW_HD
    mean = _segsum(y, jmat) * inv_n
    yc = y - mean
    var = _segsum(yc * yc, jmat) * inv_n
    yn = yc * lax.rsqrt(var + RW_GN_EPS) * ln_w + ln_b
    bonus = _segsum(xr * k2 * r_k, jmat) * xv
    o_ref[...] = (yn + bonus) * gate


def _rwkv(proj, mu, vec, wb, ab, gb, jmat, tri, B, S, nch):
    T = proj.shape[0]
    W = RW_W
    TC = nch * RW_CH
    nblk = S // TC
    c0 = COL_RW // W
    row_map = lambda col: (lambda b, i: (b * nblk + i, col))
    const = lambda b, i: (0, 0)
    return pl.pallas_call(
        functools.partial(_rwkv_kernel, nch=nch),
        out_shape=jax.ShapeDtypeStruct((T, W), F32),
        grid=(B, nblk),
        in_specs=[
            pl.BlockSpec((TC, W), row_map(c0)),
            pl.BlockSpec((TC, W), row_map(c0 + 1)),
            pl.BlockSpec((TC, W), row_map(c0 + 2)),
            pl.BlockSpec((TC, W), row_map(c0 + 3)),
            pl.BlockSpec(mu.shape, const),
            pl.BlockSpec(vec.shape, const),
            pl.BlockSpec(wb.shape, const),
            pl.BlockSpec(ab.shape, const),
            pl.BlockSpec(gb.shape, const),
            pl.BlockSpec(jmat.shape, const),
            pl.BlockSpec(tri.shape, const),
        ],
        out_specs=pl.BlockSpec((TC, W), lambda b, i: (b * nblk + i, 0)),
        scratch_shapes=[pltpu.VMEM((8, W), F32), pltpu.VMEM((W // LANES, LANES, LANES), F32)],
        compiler_params=_cparams(("parallel", "arbitrary")),
        name="rwkv7",
    )(proj, proj, proj, proj, mu, vec, wb, ab, gb, jmat, tri)


def _rwkv_params(mu, w0, wB, a0, aB, gB, k_k, k_a, r_k, ln_w, ln_b):
    W = RW_W
    mu4 = jnp.stack([mu[0:W], mu[W:2 * W], mu[2 * W:3 * W], jnp.pad(mu[3 * W:], (0, W - (mu.shape[0] - 3 * W)))])
    vec = jnp.stack([w0, a0, k_k, k_a, ln_w, ln_b, r_k.reshape(-1), jnp.zeros_like(w0)])
    wb = jnp.pad(wB, ((0, LANES - wB.shape[0]), (0, 0))).astype(BF16)
    ab = jnp.pad(aB, ((LANES - aB.shape[0], 0), (0, 0))).astype(BF16)
    return mu4, vec, wb, ab, gB.astype(BF16)


HG_HD = 64
HG_SUB = 16


def _hgrn_kernel(q_ref, f_ref, i_ref, g_ref, lbp_ref, ng_ref, j_ref, tri_ref, tot_ref, o_ref, st_ref,
                 *, layer):
    TC, W = q_ref.shape
    npair = W // LANES
    nsub = TC // HG_SUB

    @pl.when(pl.program_id(1) == 0)
    def _():
        st_ref[...] = jnp.zeros_like(st_ref)

    z = f_ref[...]
    log_f = -_softplus(-z)
    k = _sigmoid(-z)
    if layer > 0:
        lbp = lbp_ref[...]
        e = jnp.exp(lbp - jnp.max(lbp, axis=0, keepdims=True))
        sm = e / jnp.sum(e, axis=0, keepdims=True)
        lb = sm[1:2, :]
        for j in range(2, layer + 1):
            lb = lb + sm[j:j + 1, :]
        lb = jnp.maximum(lb, 0.0)
        t2 = jnp.log(lb) - _softplus(z)
        log_f = jnp.maximum(log_f, t2) + jnp.log(1.0 + jnp.exp(-jnp.abs(log_f - t2)))
        k = (1.0 - lb) * k
    q = _silu(q_ref[...])
    v = i_ref[...]
    log_f = log_f * LOG2E
    b = _cumsum_rows(tri_ref[...], log_f)
    bend = _cumsum_rows(tot_ref[...], log_f)
    qe = q * jnp.exp2(b)
    kg = k * jnp.exp2(bend - b)
    jmat = j_ref[...]
    j128 = jmat[0:LANES, 0:LANES]

    rowb = lax.broadcasted_iota(jnp.int32, (TC, LANES), 0)
    trow = lax.broadcasted_iota(jnp.int32, (HG_SUB // 2, W), 0)
    r128 = lax.broadcasted_iota(jnp.int32, (LANES, LANES), 0)
    c128 = lax.broadcasted_iota(jnp.int32, (LANES, LANES), 1)
    bd_mask = (r128 // HG_HD) == (c128 // HG_HD)

    def bf(x):
        return x.astype(BF16)

    vts = [bf(v[:, p * LANES:(p + 1) * LANES].T) for p in range(npair)]
    outs = []
    H8 = HG_SUB // 2
    for i in range(nsub):
        rs = slice(i * HG_SUB, (i + 1) * HG_SUB)
        r0 = i * HG_SUB
        (b_a, q_a, k_a, v_a), (b_b, q_b, k_b, v_b) = (
            tuple(t[r0 + h * H8:r0 + (h + 1) * H8] for t in (b, q, k, v)) for h in range(2))
        xs = []
        for s in range(H8):
            bs, ks = b_a[s:s + 1, :], k_a[s:s + 1, :]
            xs.append(jnp.where(trow >= s, q_a * (ks * jnp.exp2(b_a - bs)), 0.0))
            xs.append(q_b * (ks * jnp.exp2(b_b - bs)))
        for s in range(H8):
            bs, ks = b_b[s:s + 1, :], k_b[s:s + 1, :]
            xs.append(jnp.where(trow >= s, q_b * (ks * jnp.exp2(b_b - bs)), 0.0))
        x = bf(jnp.concatenate(xs, axis=0))
        g_end = jnp.exp2(bend[i * HG_SUB:i * HG_SUB + 1, :])
        op = []
        for p in range(npair):
            ls = slice(p * LANES, (p + 1) * LANES)
            pm = _dot(x[:, ls], j128)
            od_a = pm[0:H8] * v_a[0:1, ls]
            od_b = pm[H8:HG_SUB] * v_a[0:1, ls]
            for s in range(1, H8):
                od_a = od_a + pm[s * HG_SUB:s * HG_SUB + H8] * v_a[s:s + 1, ls]
                od_b = od_b + pm[s * HG_SUB + H8:(s + 1) * HG_SUB] * v_a[s:s + 1, ls]
            for s in range(H8):
                od_b = od_b + pm[(HG_SUB + s) * H8:(HG_SUB + s + 1) * H8] * v_b[s:s + 1, ls]
            od = jnp.concatenate([od_a, od_b], axis=0)
            st = st_ref[p]
            oi = lax.dot_general(bf(qe[rs, ls]), bf(st), (((1,), (1,)), ((), ())), preferred_element_type=F32)
            kgm = jnp.where((rowb >= i * HG_SUB) & (rowb < (i + 1) * HG_SUB), kg[:, ls], 0.0)
            st_ref[p] = st * g_end[:, ls] + jnp.where(bd_mask, _dot(vts[p], bf(kgm)), 0.0)
            op.append(od + oi)
        outs.append(jnp.concatenate(op, axis=1))
    o = jnp.concatenate(outs, axis=0)
    ms = _segsum(o * o, jmat) * (1.0 / HG_HD)
    o_ref[...] = o * lax.rsqrt(ms + RMS_EPS) * ng_ref[...] * _silu(g_ref[...])


def _hgrn(proj, lbp, ng, jmat, tri, tot, B, S, layer, tc):
    T = proj.shape[0]
    W = HG_W
    nblk = S // tc
    c0 = COL_HG // W
    row_map = lambda col: (lambda b, i: (b * nblk + i, col))
    const = lambda b, i: (0, 0)
    return pl.pallas_call(
        functools.partial(_hgrn_kernel, layer=layer),
        out_shape=jax.ShapeDtypeStruct((T, W), F32),
        grid=(B, nblk),
        in_specs=[
            pl.BlockSpec((tc, W), row_map(c0)),
            pl.BlockSpec((tc, W), row_map(c0 + 1)),
            pl.BlockSpec((tc, W), row_map(c0 + 2)),
            pl.BlockSpec((tc, W), row_map(c0 + 3)),
            pl.BlockSpec(lbp.shape, const),
            pl.BlockSpec(ng.shape, const),
            pl.BlockSpec(jmat.shape, const),
            pl.BlockSpec(tri.shape, const),
            pl.BlockSpec(tot.shape, const),
        ],
        out_specs=pl.BlockSpec((tc, W), lambda b, i: (b * nblk + i, 0)),
        scratch_shapes=[pltpu.VMEM((W // LANES, LANES, LANES), F32)],
        compiler_params=_cparams(("parallel", "arbitrary")),
        name="hgrn2",
    )(proj, proj, proj, proj, lbp, ng, jmat, tri, tot)


NSA_HD = 64
NSA_HEADS = 8
NSA_TQ = 128
CMP_STRIDE = 16
CMP_LEN = 32
SLC_BLOCK = 64
SLC_TOPN = 8
NEG_INF = float("-inf")
V_ROWS = NSA_HD + 16


def _nsa_kernel(q_ref, kv1_ref, kv2_ref, gt_ref, w1_ref, pe_ref, w2_ref, biasc_ref, bt_ref, cov_ref, egt_ref,
                o_ref, kc_ref, vct_ref, k_ref, vt_ref, am_ref, m_ref, acc_ref, cx_ref, s_ref, p_ref, al_ref, sn_ref):
    TQ = NSA_TQ
    S = kv1_ref.shape[0]
    nkc = S // TQ
    ncmp = S // CMP_STRIDE
    nslc = S // SLC_BLOCK
    GW = (NSA_HEADS // 2) * TQ
    i = pl.program_id(1)

    def with_ones(vt_g):
        return jnp.concatenate([vt_g, jnp.ones((V_ROWS - NSA_HD, vt_g.shape[1]), F32)], axis=0).astype(BF16)

    @pl.when(i == 0)
    def _():
        k_ref[0] = kv1_ref[:, 2 * LANES:3 * LANES].astype(BF16)
        k_ref[1] = kv2_ref[:, 0:LANES].astype(BF16)
        for src, (ref, c0) in enumerate(((kv1_ref, 3), (kv2_ref, 1))):
            for c in range(nkc):
                vt = ref[c * TQ:(c + 1) * TQ, c0 * LANES:(c0 + 1) * LANES].T
                for g in range(2):
                    vt_ref[src, g, c] = with_ones(vt[g * NSA_HD:(g + 1) * NSA_HD])
        for t in range(2):
            cx_ref[...] = kv1_ref[:, t * LANES:(t + 1) * LANES]
            acc_a = jnp.zeros((ncmp, 2 * LANES), F32)
            acc_b = jnp.zeros((ncmp, 2 * LANES), F32)
            for l in range(CMP_STRIDE):
                xl = cx_ref[pl.ds(l, ncmp, stride=CMP_STRIDE), :]
                acc_a += _dot((xl + pe_ref[t, l:l + 1, :]).astype(BF16), w1_ref[t, l])
                acc_b += _dot((xl + pe_ref[t, CMP_STRIDE + l:CMP_STRIDE + l + 1, :]).astype(BF16),
                              w1_ref[t, CMP_STRIDE + l])
            hid = _silu(acc_a + pltpu.roll(acc_b, ncmp - 1, axis=0))
            cmp = _dot(hid.astype(BF16), w2_ref[t])
            if t == 0:
                kc_ref[...] = cmp.astype(BF16)
            else:
                ct = cmp.T
                for g in range(2):
                    vct_ref[g] = with_ones(ct[g * NSA_HD:(g + 1) * NSA_HD])

    qs = q_ref[...] * (LOG2E * NSA_HD ** -0.5)
    zero = jnp.zeros((NSA_HD, TQ), F32)
    cols = []
    for pp in range(NSA_HEADS // 2):
        qt = qs[:, pp * LANES:(pp + 1) * LANES].T
        for e in range(2):
            qh = qt[e * NSA_HD:(e + 1) * NSA_HD]
            cols.append(jnp.concatenate([qh, zero] if pp < 2 else [zero, qh], axis=0))
    wq = jnp.concatenate(cols, axis=1).astype(BF16)

    def group_rows(o_t):
        return jnp.concatenate([o_t[:, hh * TQ:(hh + 1) * TQ] for hh in range(NSA_HEADS // 2)], axis=0)

    def raw_scores(br, kc):
        return _dot(k_ref[br, pl.ds(pl.multiple_of(kc * TQ, TQ), TQ), :], wq)

    cpq = TQ // CMP_STRIDE
    sc = _dot(kc_ref[...], wq) + biasc_ref[pl.ds(pl.multiple_of(cpq * (pl.num_programs(1) - 1 - i), cpq), ncmp), :]
    near = [i - 2, i - 1, i]
    near_kc = [jnp.maximum(kc, 0) for kc in near]
    for t, kc in enumerate(near_kc):
        for br in range(2):
            sn_ref[br, t] = raw_scores(br, kc)
    for u in range(2):
        s_ref[0, u] = raw_scores(0, u)

    mx = jnp.max(sc, axis=0, keepdims=True)
    ex = jnp.exp2(sc - jnp.where(mx == NEG_INF, 0.0, mx))
    p = ex * (1.0 / jnp.maximum(jnp.sum(ex, axis=0, keepdims=True), 1e-30))
    o_cmp = []
    for g in range(2):
        pg = p[:, g * GW:(g + 1) * GW]
        o_cmp.append(group_rows(_dot(vct_ref[g, 0:NSA_HD, :], pg.astype(BF16))))
        psum = pg[:, 0:TQ]
        for hh in range(1, NSA_HEADS // 2):
            psum = psum + pg[:, hh * TQ:(hh + 1) * TQ]
        hi, lw = _split2(psum)
        imp_t = _dot(cov_ref[...], hi) + _dot(cov_ref[...], lw)
        blk = lax.broadcasted_iota(jnp.int32, (nslc, TQ), 0)
        cur = (i * TQ + lax.broadcasted_iota(jnp.int32, (nslc, TQ), 1)) // SLC_BLOCK
        forced = (blk == 0) | (blk == cur) | (blk == cur - 1)
        score = jnp.where(forced, jnp.inf, jnp.where(blk <= cur, imp_t, NEG_INF))
        cnt = jnp.zeros((nslc, TQ), F32)
        for mp in range(nslc):
            sm = score[mp:mp + 1, :]
            ahead = (sm > score) | ((sm == score) & (blk > mp))
            cnt = cnt + jnp.where(ahead, 1.0, 0.0)
        am_ref[g, 0:nslc] = jnp.where(cnt < SLC_TOPN, 0.0, NEG_INF)
        am_ref[g, nslc:] = jnp.full((am_ref.shape[1] - nslc, TQ), NEG_INF, F32)

    def reset(br):
        m_ref[br] = jnp.full(m_ref.shape[1:], NEG_INF, F32)
        acc_ref[br] = jnp.zeros(acc_ref.shape[1:], F32)

    def block_mask(r0):
        mk = []
        for g in range(2):
            halves = [jnp.broadcast_to(am_ref[g, pl.ds(r0 + u, 1), :], (SLC_BLOCK, TQ)) for u in range(2)]
            mk.append(jnp.concatenate(halves, axis=0))
        return jnp.concatenate([mk[0]] * (NSA_HEADS // 2) + [mk[1]] * (NSA_HEADS // 2), axis=1)

    def softmax_update(br, ss):
        m_prev = m_ref[br]
        m_new = m_prev
        for s in ss:
            m_new = jnp.maximum(m_new, jnp.max(s, axis=0, keepdims=True))
        m_safe = jnp.where(m_new == NEG_INF, 0.0, m_new)
        m_ref[br] = m_new
        return jnp.exp2(m_prev - m_safe), [jnp.exp2(s - m_safe).astype(BF16) for s in ss]

    def value_update(br, alpha, kcs, pts):
        for g in range(2):
            upd = alpha[:, g * GW:(g + 1) * GW] * acc_ref[br, g]
            for kc, pt in zip(kcs, pts):
                upd = upd + _dot(vt_ref[br, g, kc], pt[:, g * GW:(g + 1) * GW])
            acc_ref[br, g] = upd

    def result(br):
        outs = []
        for g in range(2):
            acc = acc_ref[br, g]
            inv = 1.0 / jnp.maximum(acc[NSA_HD:NSA_HD + 1, :], 1e-30)
            outs.append(group_rows(acc[0:NSA_HD, :] * inv))
        return outs

    reset(1)
    win_tiles = [jnp.where(kc < 0, 0, tile) for kc, tile in zip(near, (4, 2, 1))]
    alpha, pts = softmax_update(1, [sn_ref[1, t] + bt_ref[win_tiles[t]] for t in range(3)])
    value_update(1, alpha, near_kc, pts)

    reset(0)
    nfar = jnp.maximum(i - 1, 0)
    nfp = nfar // 2
    tail_tiles = [jnp.where((near[0] >= 0) & (nfar - 2 * nfp == 1), 3, 0), jnp.where(near[1] >= 0, 2, 0), 1]
    alpha, pts = softmax_update(0, [sn_ref[0, t] + bt_ref[tail_tiles[t]] + block_mask(2 * near_kc[t])
                                    for t in range(3)])
    value_update(0, alpha, near_kc, pts)

    def pair_scores(pr, slot):
        for u in range(2):
            s_ref[slot, u] = raw_scores(0, jnp.clip(2 * pr + u, 0, nkc - 1))

    def pair_values(pr, slot):
        value_update(0, al_ref[slot], [jnp.clip(2 * pr + u, 0, nkc - 1) for u in range(2)],
                     [p_ref[slot, u] for u in range(2)])

    def pair_step(pr, slot):
        pair_scores(pr + 1, 1 - slot)
        pair_values(pr - 1, 1 - slot)
        rows = [jnp.where(pr < nfp, 2 * (2 * pr + u), nslc) for u in range(2)]
        alpha, pts = softmax_update(0, [s_ref[slot, u] + block_mask(rows[u]) for u in range(2)])
        al_ref[slot] = alpha
        for u in range(2):
            p_ref[slot, u] = pts[u]

    p_ref[1] = jnp.zeros(p_ref.shape[1:], BF16)
    al_ref[1] = jnp.ones(al_ref.shape[1:], F32)

    def two_pairs(q2, carry):
        pair_step(2 * q2, 0)
        pair_step(2 * q2 + 1, 1)
        return carry

    ntrip = (nfp + 1) // 2
    lax.fori_loop(0, ntrip, two_pairs, 0)
    pair_values(2 * ntrip - 1, 1)

    o_slc = result(0)
    o_win = result(1)

    sig_hi, sig_lo = _split2(_sigmoid(gt_ref[...]).T)
    out_t = None
    for j, parts in enumerate((o_cmp, o_slc, o_win)):
        gate_t = _dot(egt_ref[j], sig_hi) + _dot(egt_ref[j], sig_lo)
        term = gate_t * jnp.concatenate(parts, axis=0)
        out_t = term if out_t is None else out_t + term
    o_ref[...] = jnp.concatenate([out_t[c * LANES:(c + 1) * LANES].T for c in range(NSA_W // LANES)], axis=1)


def _nsa(proj, w1b, peb, w2b, tabs, B, S):
    T = proj.shape[0]
    W = NSA_W
    TQ = NSA_TQ
    nq = S // TQ
    HW = NSA_HEADS * TQ
    c0 = COL_NSA // W
    cg = (COL_NSA + 2 * W + 2 * LANES) // LANES
    biasc, bt, cov, egt = tabs

    def const(shape):
        return pl.BlockSpec(shape, lambda b, i, _n=len(shape): (0,) * _n)

    return pl.pallas_call(
        _nsa_kernel,
        out_shape=jax.ShapeDtypeStruct((T, W), F32),
        grid=(B, nq),
        in_specs=[
            pl.BlockSpec((TQ, W), lambda b, i: (b * nq + i, c0)),
            pl.BlockSpec((S, W), lambda b, i: (b, c0 + 1)),
            pl.BlockSpec((S, 2 * LANES), lambda b, i: (b, (COL_NSA + 2 * W) // (2 * LANES))),
            pl.BlockSpec((TQ, LANES), lambda b, i: (b * nq + i, cg)),
            const(w1b.shape), const(peb.shape), const(w2b.shape),
            const(biasc.shape), const(bt.shape), const(cov.shape), const(egt.shape),
        ],
        out_specs=pl.BlockSpec((TQ, W), lambda b, i: (b * nq + i, 0)),
        scratch_shapes=[
            pltpu.VMEM((S // CMP_STRIDE, LANES), BF16),
            pltpu.VMEM((2, V_ROWS, S // CMP_STRIDE), BF16),
            pltpu.VMEM((2, S, LANES), BF16),
            pltpu.VMEM((2, 2, S // TQ, V_ROWS, TQ), BF16),
            pltpu.VMEM((2, S // SLC_BLOCK + 8, TQ), F32),
            pltpu.VMEM((2, 1, HW), F32),
            pltpu.VMEM((2, 2, V_ROWS, HW // 2), F32),
            pltpu.VMEM((S, LANES), F32),
            pltpu.VMEM((2, 2, TQ, HW), F32),
            pltpu.VMEM((2, 2, TQ, HW), BF16),
            pltpu.VMEM((2, 1, HW), F32),
            pltpu.VMEM((2, 3, TQ, HW), F32),
        ],
        compiler_params=_cparams(("parallel", "arbitrary")),
        name="nsa",
    )(proj, proj, proj, proj, w1b, peb, w2b, biasc, bt, cov, egt)


def _blockdiag2(w):
    z = jnp.zeros_like(w)
    return jnp.concatenate([jnp.concatenate([w, z], axis=-1), jnp.concatenate([z, w], axis=-1)], axis=-2)


def _nsa_params(pe, w1, w2):
    L, Dh = pe.shape[1], pe.shape[2]
    w1b = _blockdiag2(w1.reshape(2, L, Dh, -1)).astype(BF16)
    w2b = _blockdiag2(w2).astype(BF16)
    peb = jnp.concatenate([pe, pe], axis=-1)
    return w1b, peb, w2b


def _t5_bucket_np(n):
    n = np.maximum(n, 0)
    nf = np.maximum(n, 1).astype(np.float64)
    large = 16 + (np.log(nf / 16.0) / math.log(128 / 16) * 16).astype(np.int64)
    return np.where(n < 16, n, np.minimum(large, 31))


def _nsa_tables(rel_bias, S):
    TQ = NSA_TQ
    nq = S // TQ
    rb = rel_bias.astype(F32) * LOG2E

    def tile(dist, valid, shift=False):
        vals = rb[jnp.asarray(_t5_bucket_np(dist))]
        if shift:
            vals = vals - rb[_t5_bucket_np(np.asarray(2 * TQ))]
        vals = jnp.where(jnp.asarray(valid)[..., None], vals, NEG_INF)
        vals = jnp.swapaxes(vals, -1, -2)
        return vals.reshape(vals.shape[:-2] + (NSA_HEADS * TQ,))

    ncmp = S // CMP_STRIDE
    end = np.arange(ncmp) * CMP_STRIDE + CMP_LEN - 1
    cpq = TQ // CMP_STRIDE
    rel_blk = np.arange(cpq * (nq - 1) + ncmp) - cpq * (nq - 1)
    dist_c = np.arange(TQ)[None, :] - (rel_blk[:, None] * CMP_STRIDE + CMP_LEN - 1)
    biasc = tile(dist_c, dist_c >= 0)
    t = np.arange(TQ)[None, :]
    j = np.arange(TQ)[:, None]
    every = np.ones((TQ, TQ), bool)
    bt = jnp.stack([
        tile(t - j, ~every, True),
        tile(t - j, t >= j, True),
        tile(t - j + TQ, every, True),
        tile(t - j + 2 * TQ, every, True),
        tile(t - j + 2 * TQ, j > t, True),
    ])
    assert _t5_bucket_np(np.asarray(TQ + 1)) == _t5_bucket_np(np.asarray(S))
    nslc = S // SLC_BLOCK
    s_lo = np.arange(nslc) * SLC_BLOCK
    start = np.arange(ncmp) * CMP_STRIDE
    cover = ((start[:, None] <= (s_lo + SLC_BLOCK - 1)[None, :]) & (end[:, None] >= s_lo[None, :]))
    cover &= (np.arange(ncmp) < ncmp - 1)[:, None]
    egt = np.zeros((3, NSA_W, LANES), np.float32)
    for jj in range(3):
        egt[jj, np.arange(NSA_W), (np.arange(NSA_W) // NSA_HD) * 3 + jj] = 1.0
    return biasc, bt, jnp.asarray(cover.T.astype(np.float32), BF16), jnp.asarray(egt, BF16)


def _tiles(T, n_ff):
    tm = 512 if T % 512 == 0 else T
    tf = n_ff // 2 if (n_ff // 2) % LANES == 0 else n_ff
    tn = N_PROJ // 4
    return tm, tf, tn


def _pack_w_in(w):
    hg_end = 4 * HG_W
    nsa_end = hg_end + NSA_W + 6 * 2 * NSA_HD + 3 * 8
    rw_end = nsa_end + 3 * RW_W + 256
    w = w.astype(BF16)
    lead = w.shape[:-1]
    parts = [w[..., rw_end:], w[..., :hg_end], w[..., hg_end:nsa_end],
             jnp.zeros(lead + (COL_RW - COL_NSA - (nsa_end - hg_end),), BF16), w[..., nsa_end:rw_end],
             jnp.zeros(lead + (N_PROJ - COL_RW - (rw_end - nsa_end),), BF16)]
    return jnp.concatenate(parts, axis=-1)


def kernel(x, p, ffn1_norm, ffn1_wgu, ffn1_wd, mix_norm, w_in, hg_lb, hg_norm, cmp_pe, cmp_w1, cmp_w2, rel_bias, rw_mu, rw_w0, rw_wB, rw_a0, rw_aB, rw_gB, rw_kk, rw_ka, rw_rk, rw_ln_w, rw_ln_b, w_branch, w_out, ffn2_norm, ffn2_wgu, ffn2_wd, ple_norm, ple_gate_w, ple_w, final_norm):
    B, S, D = x.shape
    depth = ffn1_norm.shape[0]
    T = B * S
    assert D == D_MODEL and S // CMP_STRIDE == LANES and S % NSA_TQ == 0
    assert w_in.shape[2] - 3 * D_MODEL == 4 * HG_W + NSA_W + 12 * NSA_HD + 24 + 3 * RW_W + 256
    tm, tf, tn = _tiles(T, ffn1_wd.shape[1])
    consts = _consts()
    tabs = _nsa_tables(rel_bias, S)
    row = lambda v: v.reshape(1, -1)
    wgu1, wd1, wgu2, wd2 = (w.astype(BF16) for w in (ffn1_wgu, ffn1_wd, ffn2_wgu, ffn2_wd))
    w_proj = _pack_w_in(w_in)
    wb, wo, wpg, wpp = (w.astype(BF16) for w in (w_branch, w_out, ple_gate_w, ple_w))
    p_rows = p.reshape(depth, T, -1)
    h = x.reshape(T, D)
    for i in range(depth):
        h = _ffn(h, row(ffn1_norm[i]), wgu1, wd1, i, tm, tf)
        proj = _proj(h, row(mix_norm[i]), w_proj, i, tm, tn)
        o_hg = _hgrn(proj, hg_lb, row(hg_norm[i]), consts["j512"], consts["tri16"], consts["tot16"], B, S, i, 128)
        o_ns = _nsa(proj, *_nsa_params(cmp_pe[i], cmp_w1[i], cmp_w2[i]), tabs, B, S)
        o_rw = _rwkv(proj, *_rwkv_params(rw_mu[i], rw_w0[i], rw_wB[i], rw_a0[i], rw_aB[i], rw_gB[i], rw_kk[i],
                                         rw_ka[i], rw_rk[i], rw_ln_w[i], rw_ln_b[i]),
                     consts["j512"], consts["tri64"], B, S, 4)
        h = _merge(h, proj, o_hg, o_ns, o_rw, wb, wo, i, tm)
        h = _ffn(h, row(ffn2_norm[i]), wgu2, wd2, i, tm, tf)
        h = _ple(h, row(ple_norm[i]), wpg, p_rows, wpp, row(final_norm), i, tm, i == depth - 1)
    return h.reshape(B, S, D)
```

```python
import functools
import itertools
import math

import jax
import jax.numpy as jnp
import numpy as np
from jax import lax
from jax.experimental import pallas as pl
from jax.experimental.pallas import tpu as pltpu

F32 = jnp.float32
BF16 = jnp.bfloat16

RMS_EPS = 1e-6
LOG2E = math.log2(math.e)
LANES = 128
VMEM_LIMIT = 48 * 1024 * 1024

HG_W = 512
NSA_W = 512
RW_W = 512
D_MODEL = 1024
COL_MG = 0
COL_HG = 3072
COL_NSA = 5120
COL_RW = 6656
N_PROJ = 8704


def _cparams(sem):
    return pltpu.CompilerParams(dimension_semantics=sem, vmem_limit_bytes=VMEM_LIMIT)


def _rms(x, g):
    return x * lax.rsqrt(jnp.mean(x * x, axis=-1, keepdims=True) + RMS_EPS) * g


def _sigmoid(x):
    return 1.0 / (1.0 + jnp.exp(-x))


def _silu(x):
    return x * _sigmoid(x)


def _dot(a, b):
    return jnp.dot(a, b, preferred_element_type=F32)


def _nt(a, b):
    return lax.dot_general(a, b, (((1,), (1,)), ((), ())), preferred_element_type=F32)


def _ffn_kernel(h_ref, g_ref, wg_ref, wu_ref, wd_ref, o_ref, xn_ref, acc_ref):
    j = pl.program_id(1)

    @pl.when(j == 0)
    def _():
        xn_ref[...] = _rms(h_ref[...], g_ref[...]).astype(BF16)
        acc_ref[...] = jnp.zeros_like(acc_ref)

    xn = xn_ref[...]
    gate = _dot(xn, wg_ref[...])
    up = _dot(xn, wu_ref[...])
    act = (_silu(gate) * up).astype(BF16)
    acc_ref[...] += _dot(act, wd_ref[...])

    @pl.when(j == pl.num_programs(1) - 1)
    def _():
        o_ref[...] = h_ref[...] + 0.5 * acc_ref[...]


def _ffn(h, g, wgu, wd, layer, tm, tf):
    T, D = h.shape
    FF = wd.shape[1]
    nf = FF // tf
    return pl.pallas_call(
        _ffn_kernel,
        out_shape=jax.ShapeDtypeStruct((T, D), F32),
        grid=(T // tm, nf),
        in_specs=[
            pl.BlockSpec((tm, D), lambda i, j: (i, 0)),
            pl.BlockSpec((1, D), lambda i, j: (0, 0)),
            pl.BlockSpec((None, D, tf), lambda i, j: (layer, 0, j)),
            pl.BlockSpec((None, D, tf), lambda i, j: (layer, 0, j + nf)),
            pl.BlockSpec((None, tf, D), lambda i, j: (layer, j, 0)),
        ],
        out_specs=pl.BlockSpec((tm, D), lambda i, j: (i, 0)),
        scratch_shapes=[pltpu.VMEM((tm, D), BF16), pltpu.VMEM((tm, D), F32)],
        compiler_params=_cparams(("parallel", "arbitrary")),
        name="ffn",
    )(h, g, wgu, wgu, wd)


def _proj_kernel(h_ref, g_ref, w_ref, o_ref):
    o_ref[...] = _dot(_rms(h_ref[...], g_ref[...]).astype(BF16), w_ref[...])


def _proj(h, g, w, layer, tm, tn):
    T, D = h.shape
    N = w.shape[2]
    return pl.pallas_call(
        _proj_kernel,
        out_shape=jax.ShapeDtypeStruct((T, N), F32),
        grid=(N // tn, T // tm),
        in_specs=[
            pl.BlockSpec((tm, D), lambda j, i: (i, 0)),
            pl.BlockSpec((1, D), lambda j, i: (0, 0)),
            pl.BlockSpec((None, D, tn), lambda j, i: (layer, 0, j)),
        ],
        out_specs=pl.BlockSpec((tm, tn), lambda j, i: (i, j)),
        compiler_params=_cparams(("parallel", "parallel")),
        name="in_proj",
    )(h, g, w)


def _merge_kernel(h_ref, m0_ref, m1_ref, m2_ref, a_ref, b_ref, c_ref, wb_ref, wo_ref, o_ref):
    merged = _sigmoid(m0_ref[...]) * _dot(a_ref[...].astype(BF16), wb_ref[0])
    merged += _sigmoid(m1_ref[...]) * _dot(b_ref[...].astype(BF16), wb_ref[1])
    merged += _sigmoid(m2_ref[...]) * _dot(c_ref[...].astype(BF16), wb_ref[2])
    o_ref[...] = h_ref[...] + _dot(merged.astype(BF16), wo_ref[...])


def _merge(h, proj, o_hg, o_ns, o_rw, wb, wo, layer, tm):
    T, D = h.shape
    W = o_hg.shape[1]
    mg0 = COL_MG // D
    return pl.pallas_call(
        _merge_kernel,
        out_shape=jax.ShapeDtypeStruct((T, D), F32),
        grid=(T // tm,),
        in_specs=[
            pl.BlockSpec((tm, D), lambda i: (i, 0)),
            pl.BlockSpec((tm, D), lambda i: (i, mg0)),
            pl.BlockSpec((tm, D), lambda i: (i, mg0 + 1)),
            pl.BlockSpec((tm, D), lambda i: (i, mg0 + 2)),
            pl.BlockSpec((tm, W), lambda i: (i, 0)),
            pl.BlockSpec((tm, W), lambda i: (i, 0)),
            pl.BlockSpec((tm, W), lambda i: (i, 0)),
            pl.BlockSpec((None, 3, W, D), lambda i: (layer, 0, 0, 0)),
            pl.BlockSpec((None, D, D), lambda i: (layer, 0, 0)),
        ],
        out_specs=pl.BlockSpec((tm, D), lambda i: (i, 0)),
        compiler_params=_cparams(("parallel",)),
        name="merge",
    )(h, proj, proj, proj, o_hg, o_ns, o_rw, wb, wo)


def _ple_kernel(h_ref, g_ref, wg_ref, p_ref, wp_ref, fg_ref, o_ref, *, final):
    h = h_ref[...]
    gate = _sigmoid(_dot(_rms(h, g_ref[...]).astype(BF16), wg_ref[...]))
    out = h + gate * _dot(p_ref[...].astype(BF16), wp_ref[...])
    if final:
        out = _rms(out, fg_ref[...])
    o_ref[...] = out


def _ple(h, g, wg, p, wp, fg, layer, tm, final):
    T, D = h.shape
    P = p.shape[2]
    return pl.pallas_call(
        functools.partial(_ple_kernel, final=final),
        out_shape=jax.ShapeDtypeStruct((T, D), F32),
        grid=(T // tm,),
        in_specs=[
            pl.BlockSpec((tm, D), lambda i: (i, 0)),
            pl.BlockSpec((1, D), lambda i: (0, 0)),
            pl.BlockSpec((None, D, D), lambda i: (layer, 0, 0)),
            pl.BlockSpec((None, tm, P), lambda i: (layer, i, 0)),
            pl.BlockSpec((None, P, D), lambda i: (layer, 0, 0)),
            pl.BlockSpec((1, D), lambda i: (0, 0)),
        ],
        out_specs=pl.BlockSpec((tm, D), lambda i: (i, 0)),
        compiler_params=_cparams(("parallel",)),
        name="ple",
    )(h, g, wg, p, wp, fg)


def _softplus(x):
    return jnp.maximum(x, 0.0) + jnp.log(1.0 + jnp.exp(-jnp.abs(x)))


def _split2(x):
    hi = x.astype(BF16)
    lo = (x - hi.astype(F32)).astype(BF16)
    return hi, lo


def _split3(x):
    hi = x.astype(BF16)
    r1 = x - hi.astype(F32)
    mid = r1.astype(BF16)
    lo = (r1 - mid.astype(F32)).astype(BF16)
    return hi, mid, lo


def _segsum(x, j):
    hi, lo = _split2(x)
    return _dot(hi, j) + _dot(lo, j)


def _cumsum_rows(tri, x):
    hi, mid, lo = _split3(x)
    return _dot(tri, hi) + (_dot(tri, mid) + _dot(tri, lo))


def _pair_stack(x, lo_mask):
    return jnp.concatenate([jnp.where(lo_mask, x, 0.0), jnp.where(lo_mask, 0.0, x)], axis=0)


def _consts():
    i512 = np.arange(512)
    j512 = (i512[:, None] // 64 == i512[None, :] // 64).astype(np.float32)
    i64 = np.arange(64)
    tri64 = (i64[:, None] >= i64[None, :]).astype(np.float32)
    i128 = np.arange(128)
    same = i128[:, None] // 16 == i128[None, :] // 16
    tri16 = (same & (i128[:, None] >= i128[None, :])).astype(np.float32)
    tot16 = same.astype(np.float32)
    return {"j512": jnp.asarray(j512, BF16), "tri64": jnp.asarray(tri64, BF16),
            "tri16": jnp.asarray(tri16, BF16), "tot16": jnp.asarray(tot16, BF16)}


RW_HD = 64
RW_CH = 64
RW_GN_EPS = 64e-5


def _rwkv_steps(r_ref, k_ref, v_ref, l_ref, mu_ref, vec_ref, wb_ref, ab_ref, gb_ref, j_ref,
                tri_ref, o_ref, carry_ref, st_ref, *, nch):
    C = RW_CH
    TC = nch * C
    W = r_ref.shape[1]
    npair = W // LANES

    row = lax.broadcasted_iota(jnp.int32, (TC, W), 0)

    def shift(x_ref, idx):
        x = x_ref[...]
        prev = jnp.where(row == 0, carry_ref[idx:idx + 1, :], pltpu.roll(x, 1, axis=0))
        carry_ref[idx:idx + 1, :] = x[TC - 1:TC, :]
        return x + (prev - x) * mu_ref[idx:idx + 1, :]

    xr = shift(r_ref, 0)
    xk = shift(k_ref, 1)
    xv = shift(v_ref, 2)
    xl = shift(l_ref, 3)
    w0, a0, k_k, k_a = (vec_ref[i:i + 1, :] for i in range(4))
    ln_w, ln_b, r_k = (vec_ref[i:i + 1, :] for i in range(4, 7))
    jmat = j_ref[...]

    wlal = xl[:, 0:LANES]
    w_pre = w0 + _dot(jnp.tanh(wlal).astype(BF16), wb_ref[...])
    a_pre = a0 + _dot(wlal.astype(BF16), ab_ref[...])
    gate = _dot(_sigmoid(xl[:, LANES:2 * LANES]).astype(BF16), gb_ref[...])
    logw = -jnp.exp(-_softplus(-w_pre) - 0.5)
    a = _sigmoid(a_pre)
    kkr = xk * k_k
    kk = kkr / jnp.maximum(jnp.sqrt(_segsum(kkr * kkr, jmat)), 1e-12)
    k2 = xk * (1.0 + (a - 1.0) * k_a)
    ka = kk * a
    yield

    lane = lax.broadcasted_iota(jnp.int32, (C, LANES), 1)
    trow = lax.broadcasted_iota(jnp.int32, (C, LANES), 0)
    lo_mask = lane < RW_HD
    scol = lane & (RW_HD - 1)
    strict = trow > scol
    incl = trow >= scol
    eye2 = (trow == scol).astype(F32)
    r128 = lax.broadcasted_iota(jnp.int32, (LANES, LANES), 0)
    c128 = lax.broadcasted_iota(jnp.int32, (LANES, LANES), 1)
    bd_mask = (r128 // RW_HD) == (c128 // RW_HD)
    diag_mask = r128 == c128
    tri = tri_ref[...]

    def bf(x):
        return x.astype(BF16)

    def stack(x):
        return _pair_stack(x, lo_mask)

    ops = []
    for c in range(nch):
        rs = slice(c * C, (c + 1) * C)
        lw = logw[rs] * LOG2E
        b = _cumsum_rows(tri, lw)
        bend = b[C - 1:C, :]
        enb = jnp.exp2(-b)
        egc = jnp.exp2(bend - b)
        g_end = jnp.exp2(bend)
        full = (xr[rs] * jnp.exp2(b), k2[rs] * enb, ka[rs] * enb, kk[rs] * jnp.exp2(b - lw), k2[rs] * egc,
                ka[rs] * egc, xv[rs], jnp.broadcast_to(g_end, (C, W)))
        for p in range(npair):
            ops.append(tuple(t[:, p * LANES:(p + 1) * LANES] for t in full))
        yield
    n = len(ops)
    gms = [_nt(bf(jnp.concatenate([bt, rt], axis=0)), bf(jnp.concatenate([stack(at), stack(kt)], axis=0)))
           for rt, kt, at, bt, _, _, _, _ in ops]
    a_ba = [jnp.where(strict, gm[0:C, 0:LANES], 0.0) for gm in gms]
    a_bk = [jnp.where(strict, gm[0:C, LANES:], 0.0) for gm in gms]
    a_ra = [jnp.where(incl, gm[C:, 0:LANES], 0.0) for gm in gms]
    a_rk = [jnp.where(incl, gm[C:, LANES:], 0.0) for gm in gms]
    yield
    pw = [-a for a in a_ba]
    ti = [eye2 + x for x in pw]
    pw = [_dot(bf(x), bf(stack(x))) for x in pw]
    yield
    nsq = int(math.log2(C)) - 1
    for k in range(1, nsq):
        both = [_dot(bf(jnp.concatenate([x, t], axis=0)), bf(stack(x))) for x, t in zip(pw, ti)]
        pw = [m[0:C] for m in both]
        ti = [t + m[C:] for t, m in zip(ti, both)]
        yield
    ti = [t + _dot(bf(t), bf(stack(x))) for t, x in zip(ti, pw)]
    tib = [bf(t) for t in ti]
    yield
    wm = [_dot(tib[j], bf(stack(ops[j][3]))) for j in range(n)]
    av = [_dot(bf(a_bk[j]), bf(stack(ops[j][6]))) for j in range(n)]
    yield
    u0 = [_dot(tib[j], bf(stack(av[j]))) for j in range(n)]
    yield
    y0 = [_dot(bf(jnp.concatenate([a_rk[j], a_ra[j]], axis=1)),
               bf(jnp.concatenate([stack(ops[j][6]), -stack(u0[j])], axis=0))) for j in range(n)]
    rw = [ops[j][0] - _dot(bf(a_ra[j]), bf(stack(wm[j]))) for j in range(n)]
    yield
    m2 = [jnp.where(diag_mask, jnp.concatenate([ops[j][7], ops[j][7]], axis=0), 0.0)
          - jnp.where(bd_mask, _dot(bf(ops[j][5].T), bf(wm[j])), 0.0) for j in range(n)]
    yield
    n2 = [jnp.where(bd_mask, _dot(bf(jnp.concatenate([ops[j][4], ops[j][5]], axis=0).T),
                                  bf(jnp.concatenate([ops[j][6], -u0[j]], axis=0))), 0.0) for j in range(n)]
    yield
    s2 = [st_ref[p] for p in range(npair)]
    ys = []
    for c in range(nch):
        js = [c * npair + p for p in range(npair)]
        ys.append(jnp.concatenate([y0[j] + _dot(bf(rw[j]), bf(s2[p])) for p, j in enumerate(js)], axis=1))
        s2 = [_dot(bf(m2[j]), bf(s2[p])) + n2[j] for p, j in enumerate(js)]
        yield
    for p in range(npair):
        st_ref[p] = s2[p]
    y = jnp.concatenate(ys, axis=0) if nch > 1 else ys[0]

    inv_n = 1.0 / RW_HD
    mean = _segsum(y, jmat) * inv_n
    yc = y - mean
    var = _segsum(yc * yc, jmat) * inv_n
    yn = yc * lax.rsqrt(var + RW_GN_EPS) * ln_w + ln_b
    bonus = _segsum(xr * k2 * r_k, jmat) * xv
    o_ref[...] = (yn + bonus) * gate


def _rwkv_params(mu, w0, wB, a0, aB, gB, k_k, k_a, r_k, ln_w, ln_b):
    W = RW_W
    mu4 = jnp.stack([mu[0:W], mu[W:2 * W], mu[2 * W:3 * W], jnp.pad(mu[3 * W:], (0, W - (mu.shape[0] - 3 * W)))])
    vec = jnp.stack([w0, a0, k_k, k_a, ln_w, ln_b, r_k.reshape(-1), jnp.zeros_like(w0)])
    wb = jnp.pad(wB, ((0, LANES - wB.shape[0]), (0, 0))).astype(BF16)
    ab = jnp.pad(aB, ((LANES - aB.shape[0], 0), (0, 0))).astype(BF16)
    return mu4, vec, wb, ab, gB.astype(BF16)


HG_HD = 64
HG_SUB = 16


def _hgrn_steps(q_ref, f_ref, i_ref, g_ref, lbp_ref, ng_ref, j_ref, tri_ref, tot_ref, o_ref, st_ref,
                *, layer, rows):
    W = q_ref.shape[1]
    TC = rows.stop - rows.start
    npair = W // LANES
    nsub = TC // HG_SUB

    z = f_ref[rows, :]
    log_f = -_softplus(-z)
    k = _sigmoid(-z)
    if layer > 0:
        lbp = lbp_ref[...]
        e = jnp.exp(lbp - jnp.max(lbp, axis=0, keepdims=True))
        sm = e / jnp.sum(e, axis=0, keepdims=True)
        lb = sm[1:2, :]
        for j in range(2, layer + 1):
            lb = lb + sm[j:j + 1, :]
        lb = jnp.maximum(lb, 0.0)
        t2 = jnp.log(lb) - _softplus(z)
        log_f = jnp.maximum(log_f, t2) + jnp.log(1.0 + jnp.exp(-jnp.abs(log_f - t2)))
        k = (1.0 - lb) * k
    q = _silu(q_ref[rows, :])
    v = i_ref[rows, :]
    log_f = log_f * LOG2E
    b = _cumsum_rows(tri_ref[...], log_f)
    bend = _cumsum_rows(tot_ref[...], log_f)
    qe = q * jnp.exp2(b)
    kg = k * jnp.exp2(bend - b)
    jmat = j_ref[...]
    j128 = jmat[0:LANES, 0:LANES]

    rowb = lax.broadcasted_iota(jnp.int32, (TC, LANES), 0)
    trow = lax.broadcasted_iota(jnp.int32, (HG_SUB // 2, W), 0)
    r128 = lax.broadcasted_iota(jnp.int32, (LANES, LANES), 0)
    c128 = lax.broadcasted_iota(jnp.int32, (LANES, LANES), 1)
    bd_mask = (r128 // HG_HD) == (c128 // HG_HD)

    def bf(x):
        return x.astype(BF16)

    vts = [bf(v[:, p * LANES:(p + 1) * LANES].T) for p in range(npair)]
    yield
    outs = []
    H8 = HG_SUB // 2
    for i in range(nsub):
        rs = slice(i * HG_SUB, (i + 1) * HG_SUB)
        r0 = i * HG_SUB
        (b_a, q_a, k_a, v_a), (b_b, q_b, k_b, v_b) = (
            tuple(t[r0 + h * H8:r0 + (h + 1) * H8] for t in (b, q, k, v)) for h in range(2))
        xs = []
        for s in range(H8):
            bs, ks = b_a[s:s + 1, :], k_a[s:s + 1, :]
            xs.append(jnp.where(trow >= s, q_a * (ks * jnp.exp2(b_a - bs)), 0.0))
            xs.append(q_b * (ks * jnp.exp2(b_b - bs)))
        for s in range(H8):
            bs, ks = b_b[s:s + 1, :], k_b[s:s + 1, :]
            xs.append(jnp.where(trow >= s, q_b * (ks * jnp.exp2(b_b - bs)), 0.0))
        x = bf(jnp.concatenate(xs, axis=0))
        g_end = jnp.exp2(bend[i * HG_SUB:i * HG_SUB + 1, :])
        op = []
        for p in range(npair):
            ls = slice(p * LANES, (p + 1) * LANES)
            pm = _dot(x[:, ls], j128)
            od_a = pm[0:H8] * v_a[0:1, ls]
            od_b = pm[H8:HG_SUB] * v_a[0:1, ls]
            for s in range(1, H8):
                od_a = od_a + pm[s * HG_SUB:s * HG_SUB + H8] * v_a[s:s + 1, ls]
                od_b = od_b + pm[s * HG_SUB + H8:(s + 1) * HG_SUB] * v_a[s:s + 1, ls]
            for s in range(H8):
                od_b = od_b + pm[(HG_SUB + s) * H8:(HG_SUB + s + 1) * H8] * v_b[s:s + 1, ls]
            od = jnp.concatenate([od_a, od_b], axis=0)
            st = st_ref[p]
            oi = lax.dot_general(bf(qe[rs, ls]), bf(st), (((1,), (1,)), ((), ())), preferred_element_type=F32)
            kgm = jnp.where((rowb >= i * HG_SUB) & (rowb < (i + 1) * HG_SUB), kg[:, ls], 0.0)
            st_ref[p] = st * g_end[:, ls] + jnp.where(bd_mask, _dot(vts[p], bf(kgm)), 0.0)
            op.append(od + oi)
        outs.append(jnp.concatenate(op, axis=1))
        yield
    o = jnp.concatenate(outs, axis=0)
    ms = _segsum(o * o, jmat) * (1.0 / HG_HD)
    o_ref[rows, :] = o * lax.rsqrt(ms + RMS_EPS) * ng_ref[...] * _silu(g_ref[rows, :])


def _interleave(*gens):
    gens = list(gens)
    while gens:
        for g in list(gens):
            try:
                next(g)
            except StopIteration:
                gens.remove(g)


def _hgrn_rwkv_kernel(hq_ref, hf_ref, hi_ref, hgt_ref, lbp_ref, ng_ref, tri16_ref, tot16_ref,
                      rr_ref, rk_ref, rv_ref, rl_ref, mu_ref, vec_ref, wb_ref, ab_ref, gb_ref, j_ref, tri64_ref,
                      ohg_ref, orw_ref, hst_ref, carry_ref, rst_ref, *, layer, nch):
    @pl.when(pl.program_id(1) == 0)
    def _():
        hst_ref[...] = jnp.zeros_like(hst_ref)
        carry_ref[...] = jnp.zeros_like(carry_ref)
        rst_ref[...] = jnp.zeros_like(rst_ref)

    th = tri16_ref.shape[0]
    hgrn = itertools.chain(*[
        _hgrn_steps(hq_ref, hf_ref, hi_ref, hgt_ref, lbp_ref, ng_ref, j_ref, tri16_ref, tot16_ref, ohg_ref, hst_ref,
                    layer=layer, rows=slice(r, r + th)) for r in range(0, hq_ref.shape[0], th)])
    rwkv = _rwkv_steps(rr_ref, rk_ref, rv_ref, rl_ref, mu_ref, vec_ref, wb_ref, ab_ref, gb_ref, j_ref, tri64_ref,
                       orw_ref, carry_ref, rst_ref, nch=nch)
    _interleave(hgrn, rwkv)


def _hgrn_rwkv(proj, lbp, ng, tri16, tot16, mu, vec, wb, ab, gb, jmat, tri64, B, S, layer, nch):
    T = proj.shape[0]
    W = HG_W
    TC = nch * RW_CH
    nblk = S // TC
    ch, cr = COL_HG // W, COL_RW // W
    rows = lambda col: pl.BlockSpec((TC, W), lambda b, i: (b * nblk + i, col))
    const = lambda a: pl.BlockSpec(a.shape, lambda b, i: (0, 0))
    out = jax.ShapeDtypeStruct((T, W), F32)
    state = pltpu.VMEM((W // LANES, LANES, LANES), F32)
    return pl.pallas_call(
        functools.partial(_hgrn_rwkv_kernel, layer=layer, nch=nch),
        out_shape=(out, out),
        grid=(B, nblk),
        in_specs=[rows(ch), rows(ch + 1), rows(ch + 2), rows(ch + 3), const(lbp), const(ng), const(tri16), const(tot16),
                  rows(cr), rows(cr + 1), rows(cr + 2), rows(cr + 3), const(mu), const(vec), const(wb), const(ab),
                  const(gb), const(jmat), const(tri64)],
        out_specs=(rows(0), rows(0)),
        scratch_shapes=[state, pltpu.VMEM((8, W), F32), state],
        compiler_params=_cparams(("parallel", "arbitrary")),
        name="hgrn2_rwkv7",
    )(proj, proj, proj, proj, lbp, ng, tri16, tot16, proj, proj, proj, proj, mu, vec, wb, ab, gb, jmat, tri64)


NSA_HD = 64
NSA_HEADS = 8
NSA_TQ = 128
CMP_STRIDE = 16
CMP_LEN = 32
SLC_BLOCK = 64
SLC_TOPN = 8
NEG_INF = float("-inf")
V_ROWS = NSA_HD + 16


def _nsa_kernel(q_ref, kv1_ref, kv2_ref, gt_ref, w1_ref, pe_ref, w2_ref, biasc_ref, bt_ref, cov_ref, egt_ref,
                o_ref, kc_ref, vct_ref, k_ref, vt_ref, am_ref, m_ref, acc_ref, cx_ref, s_ref, p_ref, al_ref, sn_ref):
    TQ = NSA_TQ
    S = kv1_ref.shape[0]
    nkc = S // TQ
    ncmp = S // CMP_STRIDE
    nslc = S // SLC_BLOCK
    GW = (NSA_HEADS // 2) * TQ
    i = pl.program_id(1)

    def with_ones(vt_g):
        return jnp.concatenate([vt_g, jnp.ones((V_ROWS - NSA_HD, vt_g.shape[1]), F32)], axis=0).astype(BF16)

    @pl.when(i == 0)
    def _():
        k_ref[0] = kv1_ref[:, 2 * LANES:3 * LANES].astype(BF16)
        k_ref[1] = kv2_ref[:, 0:LANES].astype(BF16)
        for src, (ref, c0) in enumerate(((kv1_ref, 3), (kv2_ref, 1))):
            for c in range(nkc):
                vt = ref[c * TQ:(c + 1) * TQ, c0 * LANES:(c0 + 1) * LANES].T
                for g in range(2):
                    vt_ref[src, g, c] = with_ones(vt[g * NSA_HD:(g + 1) * NSA_HD])
        for t in range(2):
            cx_ref[...] = kv1_ref[:, t * LANES:(t + 1) * LANES]
            acc_a = jnp.zeros((ncmp, 2 * LANES), F32)
            acc_b = jnp.zeros((ncmp, 2 * LANES), F32)
            for l in range(CMP_STRIDE):
                xl = cx_ref[pl.ds(l, ncmp, stride=CMP_STRIDE), :]
                acc_a += _dot((xl + pe_ref[t, l:l + 1, :]).astype(BF16), w1_ref[t, l])
                acc_b += _dot((xl + pe_ref[t, CMP_STRIDE + l:CMP_STRIDE + l + 1, :]).astype(BF16),
                              w1_ref[t, CMP_STRIDE + l])
            hid = _silu(acc_a + pltpu.roll(acc_b, ncmp - 1, axis=0))
            cmp = _dot(hid.astype(BF16), w2_ref[t])
            if t == 0:
                kc_ref[...] = cmp.astype(BF16)
            else:
                ct = cmp.T
                for g in range(2):
                    vct_ref[g] = with_ones(ct[g * NSA_HD:(g + 1) * NSA_HD])

    qs = q_ref[...] * (LOG2E * NSA_HD ** -0.5)
    zero = jnp.zeros((NSA_HD, TQ), F32)
    cols = []
    for pp in range(NSA_HEADS // 2):
        qt = qs[:, pp * LANES:(pp + 1) * LANES].T
        for e in range(2):
            qh = qt[e * NSA_HD:(e + 1) * NSA_HD]
            cols.append(jnp.concatenate([qh, zero] if pp < 2 else [zero, qh], axis=0))
    wq = jnp.concatenate(cols, axis=1).astype(BF16)

    def group_rows(o_t):
        return jnp.concatenate([o_t[:, hh * TQ:(hh + 1) * TQ] for hh in range(NSA_HEADS // 2)], axis=0)

    def raw_scores(br, kc):
        return _dot(k_ref[br, pl.ds(pl.multiple_of(kc * TQ, TQ), TQ), :], wq)

    cpq = TQ // CMP_STRIDE
    sc = _dot(kc_ref[...], wq) + biasc_ref[pl.ds(pl.multiple_of(cpq * (pl.num_programs(1) - 1 - i), cpq), ncmp), :]
    near = [i - 2, i - 1, i]
    near_kc = [jnp.maximum(kc, 0) for kc in near]
    for t, kc in enumerate(near_kc):
        for br in range(2):
            sn_ref[br, t] = raw_scores(br, kc)
    for u in range(2):
        s_ref[0, u] = raw_scores(0, u)

    mx = jnp.max(sc, axis=0, keepdims=True)
    ex = jnp.exp2(sc - jnp.where(mx == NEG_INF, 0.0, mx))
    p = ex * (1.0 / jnp.maximum(jnp.sum(ex, axis=0, keepdims=True), 1e-30))
    o_cmp = []
    for g in range(2):
        pg = p[:, g * GW:(g + 1) * GW]
        o_cmp.append(group_rows(_dot(vct_ref[g, 0:NSA_HD, :], pg.astype(BF16))))
        psum = pg[:, 0:TQ]
        for hh in range(1, NSA_HEADS // 2):
            psum = psum + pg[:, hh * TQ:(hh + 1) * TQ]
        hi, lw = _split2(psum)
        imp_t = _dot(cov_ref[...], hi) + _dot(cov_ref[...], lw)
        blk = lax.broadcasted_iota(jnp.int32, (nslc, TQ), 0)
        cur = (i * TQ + lax.broadcasted_iota(jnp.int32, (nslc, TQ), 1)) // SLC_BLOCK
        forced = (blk == 0) | (blk == cur) | (blk == cur - 1)
        score = jnp.where(forced, jnp.inf, jnp.where(blk <= cur, imp_t, NEG_INF))
        cnt = jnp.zeros((nslc, TQ), F32)
        for mp in range(nslc):
            sm = score[mp:mp + 1, :]
            ahead = (sm > score) | ((sm == score) & (blk > mp))
            cnt = cnt + jnp.where(ahead, 1.0, 0.0)
        am_ref[g, 0:nslc] = jnp.where(cnt < SLC_TOPN, 0.0, NEG_INF)
        am_ref[g, nslc:] = jnp.full((am_ref.shape[1] - nslc, TQ), NEG_INF, F32)

    def reset(br):
        m_ref[br] = jnp.full(m_ref.shape[1:], NEG_INF, F32)
        acc_ref[br] = jnp.zeros(acc_ref.shape[1:], F32)

    def block_mask(r0):
        mk = []
        for g in range(2):
            halves = [jnp.broadcast_to(am_ref[g, pl.ds(r0 + u, 1), :], (SLC_BLOCK, TQ)) for u in range(2)]
            mk.append(jnp.concatenate(halves, axis=0))
        return jnp.concatenate([mk[0]] * (NSA_HEADS // 2) + [mk[1]] * (NSA_HEADS // 2), axis=1)

    def softmax_update(br, ss):
        m_prev = m_ref[br]
        m_new = m_prev
        for s in ss:
            m_new = jnp.maximum(m_new, jnp.max(s, axis=0, keepdims=True))
        m_safe = jnp.where(m_new == NEG_INF, 0.0, m_new)
        m_ref[br] = m_new
        return jnp.exp2(m_prev - m_safe), [jnp.exp2(s - m_safe).astype(BF16) for s in ss]

    def value_update(br, alpha, kcs, pts):
        for g in range(2):
            upd = alpha[:, g * GW:(g + 1) * GW] * acc_ref[br, g]
            for kc, pt in zip(kcs, pts):
                upd = upd + _dot(vt_ref[br, g, kc], pt[:, g * GW:(g + 1) * GW])
            acc_ref[br, g] = upd

    def result(br):
        outs = []
        for g in range(2):
            acc = acc_ref[br, g]
            inv = 1.0 / jnp.maximum(acc[NSA_HD:NSA_HD + 1, :], 1e-30)
            outs.append(group_rows(acc[0:NSA_HD, :] * inv))
        return outs

    reset(1)
    win_tiles = [jnp.where(kc < 0, 0, tile) for kc, tile in zip(near, (4, 2, 1))]
    alpha, pts = softmax_update(1, [sn_ref[1, t] + bt_ref[win_tiles[t]] for t in range(3)])
    value_update(1, alpha, near_kc, pts)

    reset(0)
    nfar = jnp.maximum(i - 1, 0)
    nfp = nfar // 2
    tail_tiles = [jnp.where((near[0] >= 0) & (nfar - 2 * nfp == 1), 3, 0), jnp.where(near[1] >= 0, 2, 0), 1]
    alpha, pts = softmax_update(0, [sn_ref[0, t] + bt_ref[tail_tiles[t]] + block_mask(2 * near_kc[t])
                                    for t in range(3)])
    value_update(0, alpha, near_kc, pts)

    def pair_scores(pr, slot):
        for u in range(2):
            s_ref[slot, u] = raw_scores(0, jnp.clip(2 * pr + u, 0, nkc - 1))

    def pair_values(pr, slot):
        value_update(0, al_ref[slot], [jnp.clip(2 * pr + u, 0, nkc - 1) for u in range(2)],
                     [p_ref[slot, u] for u in range(2)])

    def pair_step(pr, slot):
        pair_scores(pr + 1, 1 - slot)
        pair_values(pr - 1, 1 - slot)
        rows = [jnp.where(pr < nfp, 2 * (2 * pr + u), nslc) for u in range(2)]
        alpha, pts = softmax_update(0, [s_ref[slot, u] + block_mask(rows[u]) for u in range(2)])
        al_ref[slot] = alpha
        for u in range(2):
            p_ref[slot, u] = pts[u]

    p_ref[1] = jnp.zeros(p_ref.shape[1:], BF16)
    al_ref[1] = jnp.ones(al_ref.shape[1:], F32)

    def two_pairs(q2, carry):
        pair_step(2 * q2, 0)
        pair_step(2 * q2 + 1, 1)
        return carry

    ntrip = (nfp + 1) // 2
    lax.fori_loop(0, ntrip, two_pairs, 0)
    pair_values(2 * ntrip - 1, 1)

    o_slc = result(0)
    o_win = result(1)

    sig_hi, sig_lo = _split2(_sigmoid(gt_ref[...]).T)
    out_t = None
    for j, parts in enumerate((o_cmp, o_slc, o_win)):
        gate_t = _dot(egt_ref[j], sig_hi) + _dot(egt_ref[j], sig_lo)
        term = gate_t * jnp.concatenate(parts, axis=0)
        out_t = term if out_t is None else out_t + term
    o_ref[...] = jnp.concatenate([out_t[c * LANES:(c + 1) * LANES].T for c in range(NSA_W // LANES)], axis=1)


def _nsa(proj, w1b, peb, w2b, tabs, B, S):
    T = proj.shape[0]
    W = NSA_W
    TQ = NSA_TQ
    nq = S // TQ
    HW = NSA_HEADS * TQ
    c0 = COL_NSA // W
    cg = (COL_NSA + 2 * W + 2 * LANES) // LANES
    biasc, bt, cov, egt = tabs

    def const(shape):
        return pl.BlockSpec(shape, lambda b, i, _n=len(shape): (0,) * _n)

    return pl.pallas_call(
        _nsa_kernel,
        out_shape=jax.ShapeDtypeStruct((T, W), F32),
        grid=(B, nq),
        in_specs=[
            pl.BlockSpec((TQ, W), lambda b, i: (b * nq + i, c0)),
            pl.BlockSpec((S, W), lambda b, i: (b, c0 + 1)),
            pl.BlockSpec((S, 2 * LANES), lambda b, i: (b, (COL_NSA + 2 * W) // (2 * LANES))),
            pl.BlockSpec((TQ, LANES), lambda b, i: (b * nq + i, cg)),
            const(w1b.shape), const(peb.shape), const(w2b.shape),
            const(biasc.shape), const(bt.shape), const(cov.shape), const(egt.shape),
        ],
        out_specs=pl.BlockSpec((TQ, W), lambda b, i: (b * nq + i, 0)),
        scratch_shapes=[
            pltpu.VMEM((S // CMP_STRIDE, LANES), BF16),
            pltpu.VMEM((2, V_ROWS, S // CMP_STRIDE), BF16),
            pltpu.VMEM((2, S, LANES), BF16),
            pltpu.VMEM((2, 2, S // TQ, V_ROWS, TQ), BF16),
            pltpu.VMEM((2, S // SLC_BLOCK + 8, TQ), F32),
            pltpu.VMEM((2, 1, HW), F32),
            pltpu.VMEM((2, 2, V_ROWS, HW // 2), F32),
            pltpu.VMEM((S, LANES), F32),
            pltpu.VMEM((2, 2, TQ, HW), F32),
            pltpu.VMEM((2, 2, TQ, HW), BF16),
            pltpu.VMEM((2, 1, HW), F32),
            pltpu.VMEM((2, 3, TQ, HW), F32),
        ],
        compiler_params=_cparams(("parallel", "arbitrary")),
        name="nsa",
    )(proj, proj, proj, proj, w1b, peb, w2b, biasc, bt, cov, egt)


def _blockdiag2(w):
    z = jnp.zeros_like(w)
    return jnp.concatenate([jnp.concatenate([w, z], axis=-1), jnp.concatenate([z, w], axis=-1)], axis=-2)


def _nsa_params(pe, w1, w2):
    L, Dh = pe.shape[1], pe.shape[2]
    w1b = _blockdiag2(w1.reshape(2, L, Dh, -1)).astype(BF16)
    w2b = _blockdiag2(w2).astype(BF16)
    peb = jnp.concatenate([pe, pe], axis=-1)
    return w1b, peb, w2b


def _t5_bucket_np(n):
    n = np.maximum(n, 0)
    nf = np.maximum(n, 1).astype(np.float64)
    large = 16 + (np.log(nf / 16.0) / math.log(128 / 16) * 16).astype(np.int64)
    return np.where(n < 16, n, np.minimum(large, 31))


def _nsa_tables(rel_bias, S):
    TQ = NSA_TQ
    nq = S // TQ
    rb = rel_bias.astype(F32) * LOG2E

    def tile(dist, valid, shift=False):
        vals = rb[jnp.asarray(_t5_bucket_np(dist))]
        if shift:
            vals = vals - rb[_t5_bucket_np(np.asarray(2 * TQ))]
        vals = jnp.where(jnp.asarray(valid)[..., None], vals, NEG_INF)
        vals = jnp.swapaxes(vals, -1, -2)
        return vals.reshape(vals.shape[:-2] + (NSA_HEADS * TQ,))

    ncmp = S // CMP_STRIDE
    end = np.arange(ncmp) * CMP_STRIDE + CMP_LEN - 1
    cpq = TQ // CMP_STRIDE
    rel_blk = np.arange(cpq * (nq - 1) + ncmp) - cpq * (nq - 1)
    dist_c = np.arange(TQ)[None, :] - (rel_blk[:, None] * CMP_STRIDE + CMP_LEN - 1)
    biasc = tile(dist_c, dist_c >= 0)
    t = np.arange(TQ)[None, :]
    j = np.arange(TQ)[:, None]
    every = np.ones((TQ, TQ), bool)
    bt = jnp.stack([
        tile(t - j, ~every, True),
        tile(t - j, t >= j, True),
        tile(t - j + TQ, every, True),
        tile(t - j + 2 * TQ, every, True),
        tile(t - j + 2 * TQ, j > t, True),
    ])
    assert _t5_bucket_np(np.asarray(TQ + 1)) == _t5_bucket_np(np.asarray(S))
    nslc = S // SLC_BLOCK
    s_lo = np.arange(nslc) * SLC_BLOCK
    start = np.arange(ncmp) * CMP_STRIDE
    cover = ((start[:, None] <= (s_lo + SLC_BLOCK - 1)[None, :]) & (end[:, None] >= s_lo[None, :]))
    cover &= (np.arange(ncmp) < ncmp - 1)[:, None]
    egt = np.zeros((3, NSA_W, LANES), np.float32)
    for jj in range(3):
        egt[jj, np.arange(NSA_W), (np.arange(NSA_W) // NSA_HD) * 3 + jj] = 1.0
    return biasc, bt, jnp.asarray(cover.T.astype(np.float32), BF16), jnp.asarray(egt, BF16)


def _tiles(T, n_ff):
    tm = 512 if T % 512 == 0 else T
    tf = n_ff // 2 if (n_ff // 2) % LANES == 0 else n_ff
    tn = N_PROJ // 4
    return tm, tf, tn


def _pack_w_in(w):
    hg_end = 4 * HG_W
    nsa_end = hg_end + NSA_W + 6 * 2 * NSA_HD + 3 * 8
    rw_end = nsa_end + 3 * RW_W + 256
    w = w.astype(BF16)
    lead = w.shape[:-1]
    parts = [w[..., rw_end:], w[..., :hg_end], w[..., hg_end:nsa_end],
             jnp.zeros(lead + (COL_RW - COL_NSA - (nsa_end - hg_end),), BF16), w[..., nsa_end:rw_end],
             jnp.zeros(lead + (N_PROJ - COL_RW - (rw_end - nsa_end),), BF16)]
    return jnp.concatenate(parts, axis=-1)


def kernel(x, p, ffn1_norm, ffn1_wgu, ffn1_wd, mix_norm, w_in, hg_lb, hg_norm, cmp_pe, cmp_w1, cmp_w2, rel_bias, rw_mu, rw_w0, rw_wB, rw_a0, rw_aB, rw_gB, rw_kk, rw_ka, rw_rk, rw_ln_w, rw_ln_b, w_branch, w_out, ffn2_norm, ffn2_wgu, ffn2_wd, ple_norm, ple_gate_w, ple_w, final_norm):
    B, S, D = x.shape
    depth = ffn1_norm.shape[0]
    T = B * S
    assert D == D_MODEL and S // CMP_STRIDE == LANES and S % NSA_TQ == 0
    assert w_in.shape[2] - 3 * D_MODEL == 4 * HG_W + NSA_W + 12 * NSA_HD + 24 + 3 * RW_W + 256
    tm, tf, tn = _tiles(T, ffn1_wd.shape[1])
    consts = _consts()
    tabs = _nsa_tables(rel_bias, S)
    row = lambda v: v.reshape(1, -1)
    wgu1, wd1, wgu2, wd2 = (w.astype(BF16) for w in (ffn1_wgu, ffn1_wd, ffn2_wgu, ffn2_wd))
    w_proj = _pack_w_in(w_in)
    wb, wo, wpg, wpp = (w.astype(BF16) for w in (w_branch, w_out, ple_gate_w, ple_w))
    p_rows = p.reshape(depth, T, -1)
    h = x.reshape(T, D)
    for i in range(depth):
        h = _ffn(h, row(ffn1_norm[i]), wgu1, wd1, i, tm, tf)
        proj = _proj(h, row(mix_norm[i]), w_proj, i, tm, tn)
        o_ns = _nsa(proj, *_nsa_params(cmp_pe[i], cmp_w1[i], cmp_w2[i]), tabs, B, S)
        o_hg, o_rw = _hgrn_rwkv(
            proj, hg_lb, row(hg_norm[i]), consts["tri16"], consts["tot16"],
            *_rwkv_params(rw_mu[i], rw_w0[i], rw_wB[i], rw_a0[i], rw_aB[i], rw_gB[i], rw_kk[i], rw_ka[i], rw_rk[i],
                          rw_ln_w[i], rw_ln_b[i]),
            consts["j512"], consts["tri64"], B, S, i, 4)
        h = _merge(h, proj, o_hg, o_ns, o_rw, wb, wo, i, tm)
        h = _ffn(h, row(ffn2_norm[i]), wgu2, wd2, i, tm, tf)
        h = _ple(h, row(ple_norm[i]), wpg, p_rows, wpp, row(final_norm), i, tm, i == depth - 1)
    return h.reshape(B, S, D)
```

```python
import functools
import itertools
import math

import jax
import jax.numpy as jnp
import numpy as np
from jax import lax
from jax.experimental import pallas as pl
from jax.experimental.pallas import tpu as pltpu

F32 = jnp.float32
BF16 = jnp.bfloat16

RMS_EPS = 1e-6
LOG2E = math.log2(math.e)
LANES = 128
VMEM_LIMIT = 48 * 1024 * 1024

HG_W = 512
NSA_W = 512
RW_W = 512
D_MODEL = 1024
MIX_ROWS = 256
COL_MG = 0
COL_HG = 3072
COL_NSA = 5120
COL_RW = 6656
N_PROJ = 8704


def _cparams(sem):
    return pltpu.CompilerParams(dimension_semantics=sem, vmem_limit_bytes=VMEM_LIMIT)


def _rms(x, g):
    return x * lax.rsqrt(jnp.mean(x * x, axis=-1, keepdims=True) + RMS_EPS) * g


def _sigmoid(x):
    return 1.0 / (1.0 + jnp.exp(-x))


def _silu(x):
    return x * _sigmoid(x)


def _dot(a, b):
    return jnp.dot(a, b, preferred_element_type=F32)


def _nt(a, b):
    return lax.dot_general(a, b, (((1,), (1,)), ((), ())), preferred_element_type=F32)


def _ffn_kernel(h_ref, g_ref, wg_ref, wu_ref, wd_ref, o_ref, xn_ref, acc_ref):
    j = pl.program_id(1)

    @pl.when(j == 0)
    def _():
        xn_ref[...] = _rms(h_ref[...], g_ref[...]).astype(BF16)
        acc_ref[...] = jnp.zeros_like(acc_ref)

    xn = xn_ref[...]
    gate = _dot(xn, wg_ref[...])
    up = _dot(xn, wu_ref[...])
    act = (_silu(gate) * up).astype(BF16)
    acc_ref[...] += _dot(act, wd_ref[...])

    @pl.when(j == pl.num_programs(1) - 1)
    def _():
        o_ref[...] = h_ref[...] + 0.5 * acc_ref[...]


def _ffn(h, g, wgu, wd, layer, tm, tf):
    T, D = h.shape
    FF = wd.shape[1]
    nf = FF // tf
    return pl.pallas_call(
        _ffn_kernel,
        out_shape=jax.ShapeDtypeStruct((T, D), F32),
        grid=(T // tm, nf),
        in_specs=[
            pl.BlockSpec((tm, D), lambda i, j: (i, 0)),
            pl.BlockSpec((1, D), lambda i, j: (0, 0)),
            pl.BlockSpec((None, D, tf), lambda i, j: (layer, 0, j)),
            pl.BlockSpec((None, D, tf), lambda i, j: (layer, 0, j + nf)),
            pl.BlockSpec((None, tf, D), lambda i, j: (layer, j, 0)),
        ],
        out_specs=pl.BlockSpec((tm, D), lambda i, j: (i, 0)),
        scratch_shapes=[pltpu.VMEM((tm, D), BF16), pltpu.VMEM((tm, D), F32)],
        compiler_params=_cparams(("parallel", "arbitrary")),
        name="ffn",
    )(h, g, wgu, wgu, wd)


def _proj_kernel(h_ref, g_ref, w_ref, o_ref):
    o_ref[...] = _dot(_rms(h_ref[...], g_ref[...]).astype(BF16), w_ref[...])


def _proj(h, g, w, layer, tm, tn):
    T, D = h.shape
    N = w.shape[2]
    return pl.pallas_call(
        _proj_kernel,
        out_shape=jax.ShapeDtypeStruct((T, N), F32),
        grid=(N // tn, T // tm),
        in_specs=[
            pl.BlockSpec((tm, D), lambda j, i: (i, 0)),
            pl.BlockSpec((1, D), lambda j, i: (0, 0)),
            pl.BlockSpec((None, D, tn), lambda j, i: (layer, 0, j)),
        ],
        out_specs=pl.BlockSpec((tm, tn), lambda j, i: (i, j)),
        compiler_params=_cparams(("parallel", "parallel")),
        name="in_proj",
    )(h, g, w)


def _merge_kernel(h_ref, m0_ref, m1_ref, m2_ref, a_ref, b_ref, c_ref, wb_ref, wo_ref, o_ref):
    merged = _sigmoid(m0_ref[...]) * _dot(a_ref[...].astype(BF16), wb_ref[0])
    merged += _sigmoid(m1_ref[...]) * _dot(b_ref[...].astype(BF16), wb_ref[1])
    merged += _sigmoid(m2_ref[...]) * _dot(c_ref[...].astype(BF16), wb_ref[2])
    o_ref[...] = h_ref[...] + _dot(merged.astype(BF16), wo_ref[...])


def _merge(h, proj, o_hg, o_ns, o_rw, wb, wo, layer, tm):
    T, D = h.shape
    W = o_hg.shape[1]
    mg0 = COL_MG // D
    return pl.pallas_call(
        _merge_kernel,
        out_shape=jax.ShapeDtypeStruct((T, D), F32),
        grid=(T // tm,),
        in_specs=[
            pl.BlockSpec((tm, D), lambda i: (i, 0)),
            pl.BlockSpec((tm, D), lambda i: (i, mg0)),
            pl.BlockSpec((tm, D), lambda i: (i, mg0 + 1)),
            pl.BlockSpec((tm, D), lambda i: (i, mg0 + 2)),
            pl.BlockSpec((tm, W), lambda i: (i, 0)),
            pl.BlockSpec((tm, W), lambda i: (i, 0)),
            pl.BlockSpec((tm, W), lambda i: (i, 0)),
            pl.BlockSpec((None, 3, W, D), lambda i: (layer, 0, 0, 0)),
            pl.BlockSpec((None, D, D), lambda i: (layer, 0, 0)),
        ],
        out_specs=pl.BlockSpec((tm, D), lambda i: (i, 0)),
        compiler_params=_cparams(("parallel",)),
        name="merge",
    )(h, proj, proj, proj, o_hg, o_ns, o_rw, wb, wo)


def _ple_kernel(h_ref, g_ref, wg_ref, p_ref, wp_ref, fg_ref, o_ref, *, final):
    h = h_ref[...]
    gate = _sigmoid(_dot(_rms(h, g_ref[...]).astype(BF16), wg_ref[...]))
    out = h + gate * _dot(p_ref[...].astype(BF16), wp_ref[...])
    if final:
        out = _rms(out, fg_ref[...])
    o_ref[...] = out


def _ple(h, g, wg, p, wp, fg, layer, tm, final):
    T, D = h.shape
    P = p.shape[2]
    return pl.pallas_call(
        functools.partial(_ple_kernel, final=final),
        out_shape=jax.ShapeDtypeStruct((T, D), F32),
        grid=(T // tm,),
        in_specs=[
            pl.BlockSpec((tm, D), lambda i: (i, 0)),
            pl.BlockSpec((1, D), lambda i: (0, 0)),
            pl.BlockSpec((None, D, D), lambda i: (layer, 0, 0)),
            pl.BlockSpec((None, tm, P), lambda i: (layer, i, 0)),
            pl.BlockSpec((None, P, D), lambda i: (layer, 0, 0)),
            pl.BlockSpec((1, D), lambda i: (0, 0)),
        ],
        out_specs=pl.BlockSpec((tm, D), lambda i: (i, 0)),
        compiler_params=_cparams(("parallel",)),
        name="ple",
    )(h, g, wg, p, wp, fg)


def _softplus(x):
    return jnp.maximum(x, 0.0) + jnp.log(1.0 + jnp.exp(-jnp.abs(x)))


def _split2(x):
    hi = x.astype(BF16)
    lo = (x - hi.astype(F32)).astype(BF16)
    return hi, lo


def _split3(x):
    hi = x.astype(BF16)
    r1 = x - hi.astype(F32)
    mid = r1.astype(BF16)
    lo = (r1 - mid.astype(F32)).astype(BF16)
    return hi, mid, lo


def _segsum(x, j):
    hi, lo = _split2(x)
    return _dot(hi, j) + _dot(lo, j)


def _cumsum_rows(tri, x):
    hi, mid, lo = _split3(x)
    return _dot(tri, hi) + (_dot(tri, mid) + _dot(tri, lo))


def _pair_stack(x, lo_mask):
    return jnp.concatenate([jnp.where(lo_mask, x, 0.0), jnp.where(lo_mask, 0.0, x)], axis=0)


def _consts():
    i512 = np.arange(512)
    j512 = (i512[:, None] // 64 == i512[None, :] // 64).astype(np.float32)
    i64 = np.arange(64)
    tri64 = (i64[:, None] >= i64[None, :]).astype(np.float32)
    i128 = np.arange(128)
    same = i128[:, None] // 16 == i128[None, :] // 16
    tri16 = (same & (i128[:, None] >= i128[None, :])).astype(np.float32)
    tot16 = same.astype(np.float32)
    return {"j512": jnp.asarray(j512, BF16), "tri64": jnp.asarray(tri64, BF16),
            "tri16": jnp.asarray(tri16, BF16), "tot16": jnp.asarray(tot16, BF16)}


RW_HD = 64
RW_CH = 64
RW_GN_EPS = 64e-5


def _rwkv_steps(r_ref, k_ref, v_ref, l_ref, mu_ref, vec_ref, wb_ref, ab_ref, gb_ref, j_ref,
                tri_ref, o_ref, carry_ref, st_ref, *, nch):
    C = RW_CH
    TC = nch * C
    W = r_ref.shape[1]
    npair = W // LANES

    row = lax.broadcasted_iota(jnp.int32, (TC, W), 0)

    def shift(x_ref, idx):
        x = x_ref[...]
        prev = jnp.where(row == 0, carry_ref[idx:idx + 1, :], pltpu.roll(x, 1, axis=0))
        carry_ref[idx:idx + 1, :] = x[TC - 1:TC, :]
        return x + (prev - x) * mu_ref[idx:idx + 1, :]

    xr = shift(r_ref, 0)
    xk = shift(k_ref, 1)
    xv = shift(v_ref, 2)
    xl = shift(l_ref, 3)
    w0, a0, k_k, k_a = (vec_ref[i:i + 1, :] for i in range(4))
    ln_w, ln_b, r_k = (vec_ref[i:i + 1, :] for i in range(4, 7))
    jmat = j_ref[...]

    wlal = xl[:, 0:LANES]
    w_pre = w0 + _dot(jnp.tanh(wlal).astype(BF16), wb_ref[...])
    a_pre = a0 + _dot(wlal.astype(BF16), ab_ref[...])
    gate = _dot(_sigmoid(xl[:, LANES:2 * LANES]).astype(BF16), gb_ref[...])
    logw = -jnp.exp(-_softplus(-w_pre) - 0.5)
    a = _sigmoid(a_pre)
    kkr = xk * k_k
    kk = kkr / jnp.maximum(jnp.sqrt(_segsum(kkr * kkr, jmat)), 1e-12)
    k2 = xk * (1.0 + (a - 1.0) * k_a)
    ka = kk * a
    yield

    lane = lax.broadcasted_iota(jnp.int32, (C, LANES), 1)
    trow = lax.broadcasted_iota(jnp.int32, (C, LANES), 0)
    lo_mask = lane < RW_HD
    scol = lane & (RW_HD - 1)
    strict = trow > scol
    incl = trow >= scol
    eye2 = (trow == scol).astype(F32)
    r128 = lax.broadcasted_iota(jnp.int32, (LANES, LANES), 0)
    c128 = lax.broadcasted_iota(jnp.int32, (LANES, LANES), 1)
    bd_mask = (r128 // RW_HD) == (c128 // RW_HD)
    diag_mask = r128 == c128
    tri = tri_ref[...]

    def bf(x):
        return x.astype(BF16)

    def stack(x):
        return _pair_stack(x, lo_mask)

    ops = []
    for c in range(nch):
        rs = slice(c * C, (c + 1) * C)
        lw = logw[rs] * LOG2E
        b = _cumsum_rows(tri, lw)
        bend = b[C - 1:C, :]
        enb = jnp.exp2(-b)
        egc = jnp.exp2(bend - b)
        g_end = jnp.exp2(bend)
        full = (xr[rs] * jnp.exp2(b), k2[rs] * enb, ka[rs] * enb, kk[rs] * jnp.exp2(b - lw), k2[rs] * egc,
                ka[rs] * egc, xv[rs], jnp.broadcast_to(g_end, (C, W)))
        for p in range(npair):
            ops.append(tuple(t[:, p * LANES:(p + 1) * LANES] for t in full))
        yield
    n = len(ops)
    gms = [_nt(bf(jnp.concatenate([bt, rt], axis=0)), bf(jnp.concatenate([stack(at), stack(kt)], axis=0)))
           for rt, kt, at, bt, _, _, _, _ in ops]
    a_ba = [jnp.where(strict, gm[0:C, 0:LANES], 0.0) for gm in gms]
    a_bk = [jnp.where(strict, gm[0:C, LANES:], 0.0) for gm in gms]
    a_ra = [jnp.where(incl, gm[C:, 0:LANES], 0.0) for gm in gms]
    a_rk = [jnp.where(incl, gm[C:, LANES:], 0.0) for gm in gms]
    yield
    pw = [-a for a in a_ba]
    ti = [eye2 + x for x in pw]
    pw = [_dot(bf(x), bf(stack(x))) for x in pw]
    yield
    nsq = int(math.log2(C)) - 1
    for k in range(1, nsq):
        both = [_dot(bf(jnp.concatenate([x, t], axis=0)), bf(stack(x))) for x, t in zip(pw, ti)]
        pw = [m[0:C] for m in both]
        ti = [t + m[C:] for t, m in zip(ti, both)]
        yield
    ti = [t + _dot(bf(t), bf(stack(x))) for t, x in zip(ti, pw)]
    tib = [bf(t) for t in ti]
    yield
    wm = [_dot(tib[j], bf(stack(ops[j][3]))) for j in range(n)]
    av = [_dot(bf(a_bk[j]), bf(stack(ops[j][6]))) for j in range(n)]
    yield
    u0 = [_dot(tib[j], bf(stack(av[j]))) for j in range(n)]
    yield
    y0 = [_dot(bf(jnp.concatenate([a_rk[j], a_ra[j]], axis=1)),
               bf(jnp.concatenate([stack(ops[j][6]), -stack(u0[j])], axis=0))) for j in range(n)]
    rw = [ops[j][0] - _dot(bf(a_ra[j]), bf(stack(wm[j]))) for j in range(n)]
    yield
    m2 = [jnp.where(diag_mask, jnp.concatenate([ops[j][7], ops[j][7]], axis=0), 0.0)
          - jnp.where(bd_mask, _dot(bf(ops[j][5].T), bf(wm[j])), 0.0) for j in range(n)]
    yield
    n2 = [jnp.where(bd_mask, _dot(bf(jnp.concatenate([ops[j][4], ops[j][5]], axis=0).T),
                                  bf(jnp.concatenate([ops[j][6], -u0[j]], axis=0))), 0.0) for j in range(n)]
    yield
    s2 = [st_ref[p] for p in range(npair)]
    ys = []
    for c in range(nch):
        js = [c * npair + p for p in range(npair)]
        ys.append(jnp.concatenate([y0[j] + _dot(bf(rw[j]), bf(s2[p])) for p, j in enumerate(js)], axis=1))
        s2 = [_dot(bf(m2[j]), bf(s2[p])) + n2[j] for p, j in enumerate(js)]
        yield
    for p in range(npair):
        st_ref[p] = s2[p]
    y = jnp.concatenate(ys, axis=0) if nch > 1 else ys[0]

    inv_n = 1.0 / RW_HD
    mean = _segsum(y, jmat) * inv_n
    yc = y - mean
    var = _segsum(yc * yc, jmat) * inv_n
    yn = yc * lax.rsqrt(var + RW_GN_EPS) * ln_w + ln_b
    bonus = _segsum(xr * k2 * r_k, jmat) * xv
    o_ref[...] = (yn + bonus) * gate


def _rwkv_params(mu, w0, wB, a0, aB, gB, k_k, k_a, r_k, ln_w, ln_b):
    W = RW_W
    mu4 = jnp.stack([mu[0:W], mu[W:2 * W], mu[2 * W:3 * W], jnp.pad(mu[3 * W:], (0, W - (mu.shape[0] - 3 * W)))])
    vec = jnp.stack([w0, a0, k_k, k_a, ln_w, ln_b, r_k.reshape(-1), jnp.zeros_like(w0)])
    wb = jnp.pad(wB, ((0, LANES - wB.shape[0]), (0, 0))).astype(BF16)
    ab = jnp.pad(aB, ((LANES - aB.shape[0], 0), (0, 0))).astype(BF16)
    return mu4, vec, wb, ab, gB.astype(BF16)


HG_HD = 64
HG_SUB = 16


def _hgrn_steps(q_ref, f_ref, i_ref, g_ref, lbp_ref, ng_ref, j_ref, tri_ref, tot_ref, o_ref, st_ref,
                *, layer, rows):
    W = q_ref.shape[1]
    TC = rows.stop - rows.start
    npair = W // LANES
    nsub = TC // HG_SUB

    z = f_ref[rows, :]
    log_f = -_softplus(-z)
    k = _sigmoid(-z)
    if layer > 0:
        lbp = lbp_ref[...]
        e = jnp.exp(lbp - jnp.max(lbp, axis=0, keepdims=True))
        sm = e / jnp.sum(e, axis=0, keepdims=True)
        lb = sm[1:2, :]
        for j in range(2, layer + 1):
            lb = lb + sm[j:j + 1, :]
        lb = jnp.maximum(lb, 0.0)
        t2 = jnp.log(lb) - _softplus(z)
        log_f = jnp.maximum(log_f, t2) + jnp.log(1.0 + jnp.exp(-jnp.abs(log_f - t2)))
        k = (1.0 - lb) * k
    q = _silu(q_ref[rows, :])
    v = i_ref[rows, :]
    log_f = log_f * LOG2E
    b = _cumsum_rows(tri_ref[...], log_f)
    bend = _cumsum_rows(tot_ref[...], log_f)
    qe = q * jnp.exp2(b)
    kg = k * jnp.exp2(bend - b)
    jmat = j_ref[...]
    j128 = jmat[0:LANES, 0:LANES]

    rowb = lax.broadcasted_iota(jnp.int32, (TC, LANES), 0)
    trow = lax.broadcasted_iota(jnp.int32, (HG_SUB // 2, W), 0)
    r128 = lax.broadcasted_iota(jnp.int32, (LANES, LANES), 0)
    c128 = lax.broadcasted_iota(jnp.int32, (LANES, LANES), 1)
    bd_mask = (r128 // HG_HD) == (c128 // HG_HD)

    def bf(x):
        return x.astype(BF16)

    vts = [bf(v[:, p * LANES:(p + 1) * LANES].T) for p in range(npair)]
    yield
    outs = []
    H8 = HG_SUB // 2
    for i in range(nsub):
        rs = slice(i * HG_SUB, (i + 1) * HG_SUB)
        r0 = i * HG_SUB
        (b_a, q_a, k_a, v_a), (b_b, q_b, k_b, v_b) = (
            tuple(t[r0 + h * H8:r0 + (h + 1) * H8] for t in (b, q, k, v)) for h in range(2))
        xs = []
        for s in range(H8):
            bs, ks = b_a[s:s + 1, :], k_a[s:s + 1, :]
            xs.append(jnp.where(trow >= s, q_a * (ks * jnp.exp2(b_a - bs)), 0.0))
            xs.append(q_b * (ks * jnp.exp2(b_b - bs)))
        for s in range(H8):
            bs, ks = b_b[s:s + 1, :], k_b[s:s + 1, :]
            xs.append(jnp.where(trow >= s, q_b * (ks * jnp.exp2(b_b - bs)), 0.0))
        x = bf(jnp.concatenate(xs, axis=0))
        g_end = jnp.exp2(bend[i * HG_SUB:i * HG_SUB + 1, :])
        op = []
        for p in range(npair):
            ls = slice(p * LANES, (p + 1) * LANES)
            pm = _dot(x[:, ls], j128)
            od_a = pm[0:H8] * v_a[0:1, ls]
            od_b = pm[H8:HG_SUB] * v_a[0:1, ls]
            for s in range(1, H8):
                od_a = od_a + pm[s * HG_SUB:s * HG_SUB + H8] * v_a[s:s + 1, ls]
                od_b = od_b + pm[s * HG_SUB + H8:(s + 1) * HG_SUB] * v_a[s:s + 1, ls]
            for s in range(H8):
                od_b = od_b + pm[(HG_SUB + s) * H8:(HG_SUB + s + 1) * H8] * v_b[s:s + 1, ls]
            od = jnp.concatenate([od_a, od_b], axis=0)
            st = st_ref[p]
            oi = lax.dot_general(bf(qe[rs, ls]), bf(st), (((1,), (1,)), ((), ())), preferred_element_type=F32)
            kgm = jnp.where((rowb >= i * HG_SUB) & (rowb < (i + 1) * HG_SUB), kg[:, ls], 0.0)
            st_ref[p] = st * g_end[:, ls] + jnp.where(bd_mask, _dot(vts[p], bf(kgm)), 0.0)
            op.append(od + oi)
        outs.append(jnp.concatenate(op, axis=1))
        yield
    o = jnp.concatenate(outs, axis=0)
    ms = _segsum(o * o, jmat) * (1.0 / HG_HD)
    o_ref[rows, :] = o * lax.rsqrt(ms + RMS_EPS) * ng_ref[...] * _silu(g_ref[rows, :])


def _interleave(gens):
    live = list(gens)
    while live:
        for g in list(live):
            try:
                next(g)
            except StopIteration:
                live.remove(g)


def _run_parts(parts, grid, name):
    n_in = [len(p["args"]) for p in parts]
    n_out = [len(p["out_shape"]) for p in parts]
    n_scr = [len(p["scratch"]) for p in parts]

    def take(refs, counts):
        out, pos = [], 0
        for c in counts:
            out.append(refs[pos:pos + c])
            pos += c
        return out

    def kern(*refs):
        ins = take(refs[:sum(n_in)], n_in)
        outs = take(refs[sum(n_in):sum(n_in) + sum(n_out)], n_out)
        scr = take(refs[sum(n_in) + sum(n_out):], n_scr)
        gens = []
        for p, a, b, c in zip(parts, ins, outs, scr):
            gens += p["gens"](a, b, c)
        _interleave(gens)

    cat = lambda key: [x for p in parts for x in p[key]]
    return pl.pallas_call(
        kern, out_shape=cat("out_shape"), grid=grid, in_specs=cat("in_specs"), out_specs=cat("out_specs"),
        scratch_shapes=cat("scratch"), compiler_params=_cparams(("parallel", "arbitrary")), name=name,
    )(*cat("args"))


def _seq_part_specs(proj, col0, consts, S):
    T = proj.shape[0]
    W = HG_W
    nblk = S // MIX_ROWS
    rows = lambda col: pl.BlockSpec((MIX_ROWS, W), lambda b, i: (b * nblk + i, col))
    const = lambda a: pl.BlockSpec(a.shape, lambda b, i: (0, 0))
    c0 = col0 // W
    return dict(args=[proj] * 4 + list(consts), in_specs=[rows(c0 + j) for j in range(4)] + [const(a) for a in consts],
                out_shape=[jax.ShapeDtypeStruct((T, W), F32)], out_specs=[rows(0)])


def _hgrn_part(proj, lbp, ng, jmat, tri16, tot16, S, layer):
    def gens(ins, outs, scr):
        (st_ref,) = scr

        @pl.when(pl.program_id(1) == 0)
        def _():
            st_ref[...] = jnp.zeros_like(st_ref)

        th = ins[7].shape[0]
        return [itertools.chain(*[_hgrn_steps(*ins, *outs, st_ref, layer=layer, rows=slice(r, r + th))
                                  for r in range(0, MIX_ROWS, th)])]

    return dict(_seq_part_specs(proj, COL_HG, (lbp, ng, jmat, tri16, tot16), S), gens=gens,
                scratch=[pltpu.VMEM((HG_W // LANES, LANES, LANES), F32)])


def _rwkv_part(proj, mu, vec, wb, ab, gb, jmat, tri64, S):
    def gens(ins, outs, scr):
        carry_ref, st_ref = scr

        @pl.when(pl.program_id(1) == 0)
        def _():
            carry_ref[...] = jnp.zeros_like(carry_ref)
            st_ref[...] = jnp.zeros_like(st_ref)

        return [_rwkv_steps(*ins, *outs, carry_ref, st_ref, nch=MIX_ROWS // RW_CH)]

    return dict(_seq_part_specs(proj, COL_RW, (mu, vec, wb, ab, gb, jmat, tri64), S), gens=gens,
                scratch=[pltpu.VMEM((8, RW_W), F32), pltpu.VMEM((RW_W // LANES, LANES, LANES), F32)])


NSA_HD = 64
NSA_HEADS = 8
NSA_TQ = 128
CMP_STRIDE = 16
CMP_LEN = 32
SLC_BLOCK = 64
SLC_TOPN = 8
NEG_INF = float("-inf")
V_ROWS = NSA_HD + 16


def _nsa_steps(q_ref, kv1_ref, kv2_ref, gt_ref, w1_ref, pe_ref, w2_ref, biasc_ref, bt_ref, cov_ref, egt_ref,
               o_ref, kc_ref, vct_ref, k_ref, vt_ref, am_ref, m_ref, acc_ref, cx_ref, s_ref, p_ref, al_ref, sn_ref,
               *, bs):
    TQ = NSA_TQ
    S = kv1_ref.shape[0]
    nkc = S // TQ
    ncmp = S // CMP_STRIDE
    nslc = S // SLC_BLOCK
    GW = (NSA_HEADS // 2) * TQ
    i = 2 * pl.program_id(1) + bs
    nq = 2 * pl.num_programs(1)
    q_ref, gt_ref, o_ref = (r.at[bs * TQ:(bs + 1) * TQ] for r in (q_ref, gt_ref, o_ref))
    am_ref, m_ref, acc_ref, s_ref, p_ref, al_ref, sn_ref = (
        r.at[bs] for r in (am_ref, m_ref, acc_ref, s_ref, p_ref, al_ref, sn_ref))
    first_block_only = pl.when(i == 0) if bs == 0 else (lambda f: None)

    def with_ones(vt_g):
        return jnp.concatenate([vt_g, jnp.ones((V_ROWS - NSA_HD, vt_g.shape[1]), F32)], axis=0).astype(BF16)

    @first_block_only
    def _():
        k_ref[0] = kv1_ref[:, 2 * LANES:3 * LANES].astype(BF16)
        k_ref[1] = kv2_ref[:, 0:LANES].astype(BF16)
        for src, (ref, c0) in enumerate(((kv1_ref, 3), (kv2_ref, 1))):
            for c in range(nkc):
                vt = ref[c * TQ:(c + 1) * TQ, c0 * LANES:(c0 + 1) * LANES].T
                for g in range(2):
                    vt_ref[src, g, c] = with_ones(vt[g * NSA_HD:(g + 1) * NSA_HD])
        for t in range(2):
            cx_ref[...] = kv1_ref[:, t * LANES:(t + 1) * LANES]
            acc_a = jnp.zeros((ncmp, 2 * LANES), F32)
            acc_b = jnp.zeros((ncmp, 2 * LANES), F32)
            for l in range(CMP_STRIDE):
                xl = cx_ref[pl.ds(l, ncmp, stride=CMP_STRIDE), :]
                acc_a += _dot((xl + pe_ref[t, l:l + 1, :]).astype(BF16), w1_ref[t, l])
                acc_b += _dot((xl + pe_ref[t, CMP_STRIDE + l:CMP_STRIDE + l + 1, :]).astype(BF16),
                              w1_ref[t, CMP_STRIDE + l])
            hid = _silu(acc_a + pltpu.roll(acc_b, ncmp - 1, axis=0))
            cmp = _dot(hid.astype(BF16), w2_ref[t])
            if t == 0:
                kc_ref[...] = cmp.astype(BF16)
            else:
                ct = cmp.T
                for g in range(2):
                    vct_ref[g] = with_ones(ct[g * NSA_HD:(g + 1) * NSA_HD])

    qs = q_ref[...] * (LOG2E * NSA_HD ** -0.5)
    zero = jnp.zeros((NSA_HD, TQ), F32)
    cols = []
    for pp in range(NSA_HEADS // 2):
        qt = qs[:, pp * LANES:(pp + 1) * LANES].T
        for e in range(2):
            qh = qt[e * NSA_HD:(e + 1) * NSA_HD]
            cols.append(jnp.concatenate([qh, zero] if pp < 2 else [zero, qh], axis=0))
    wq = jnp.concatenate(cols, axis=1).astype(BF16)

    def group_rows(o_t):
        return jnp.concatenate([o_t[:, hh * TQ:(hh + 1) * TQ] for hh in range(NSA_HEADS // 2)], axis=0)

    def raw_scores(br, kc):
        return _dot(k_ref[br, pl.ds(pl.multiple_of(kc * TQ, TQ), TQ), :], wq)

    cpq = TQ // CMP_STRIDE
    sc = _dot(kc_ref[...], wq) + biasc_ref[pl.ds(pl.multiple_of(cpq * (nq - 1 - i), cpq), ncmp), :]
    near = [i - 2, i - 1, i]
    near_kc = [jnp.maximum(kc, 0) for kc in near]
    for t, kc in enumerate(near_kc):
        for br in range(2):
            sn_ref[br, t] = raw_scores(br, kc)
    for u in range(2):
        s_ref[0, u] = raw_scores(0, u)
    yield

    mx = jnp.max(sc, axis=0, keepdims=True)
    ex = jnp.exp2(sc - jnp.where(mx == NEG_INF, 0.0, mx))
    p = ex * (1.0 / jnp.maximum(jnp.sum(ex, axis=0, keepdims=True), 1e-30))
    o_cmp = []
    for g in range(2):
        pg = p[:, g * GW:(g + 1) * GW]
        o_cmp.append(group_rows(_dot(vct_ref[g, 0:NSA_HD, :], pg.astype(BF16))))
        psum = pg[:, 0:TQ]
        for hh in range(1, NSA_HEADS // 2):
            psum = psum + pg[:, hh * TQ:(hh + 1) * TQ]
        hi, lw = _split2(psum)
        imp_t = _dot(cov_ref[...], hi) + _dot(cov_ref[...], lw)
        blk = lax.broadcasted_iota(jnp.int32, (nslc, TQ), 0)
        cur = (i * TQ + lax.broadcasted_iota(jnp.int32, (nslc, TQ), 1)) // SLC_BLOCK
        forced = (blk == 0) | (blk == cur) | (blk == cur - 1)
        score = jnp.where(forced, jnp.inf, jnp.where(blk <= cur, imp_t, NEG_INF))
        cnt = jnp.zeros((nslc, TQ), F32)
        for mp in range(nslc):
            sm = score[mp:mp + 1, :]
            ahead = (sm > score) | ((sm == score) & (blk > mp))
            cnt = cnt + jnp.where(ahead, 1.0, 0.0)
        am_ref[g, 0:nslc] = jnp.where(cnt < SLC_TOPN, 0.0, NEG_INF)
        am_ref[g, nslc:] = jnp.full((am_ref.shape[1] - nslc, TQ), NEG_INF, F32)

    yield

    def reset(br):
        m_ref[br] = jnp.full(m_ref.shape[1:], NEG_INF, F32)
        acc_ref[br] = jnp.zeros(acc_ref.shape[1:], F32)

    def block_mask(r0):
        mk = []
        for g in range(2):
            halves = [jnp.broadcast_to(am_ref[g, pl.ds(r0 + u, 1), :], (SLC_BLOCK, TQ)) for u in range(2)]
            mk.append(jnp.concatenate(halves, axis=0))
        return jnp.concatenate([mk[0]] * (NSA_HEADS // 2) + [mk[1]] * (NSA_HEADS // 2), axis=1)

    def softmax_update(br, ss):
        m_prev = m_ref[br]
        m_new = m_prev
        for s in ss:
            m_new = jnp.maximum(m_new, jnp.max(s, axis=0, keepdims=True))
        m_safe = jnp.where(m_new == NEG_INF, 0.0, m_new)
        m_ref[br] = m_new
        return jnp.exp2(m_prev - m_safe), [jnp.exp2(s - m_safe).astype(BF16) for s in ss]

    def value_update(br, alpha, kcs, pts):
        for g in range(2):
            upd = alpha[:, g * GW:(g + 1) * GW] * acc_ref[br, g]
            for kc, pt in zip(kcs, pts):
                upd = upd + _dot(vt_ref[br, g, kc], pt[:, g * GW:(g + 1) * GW])
            acc_ref[br, g] = upd

    def result(br):
        outs = []
        for g in range(2):
            acc = acc_ref[br, g]
            inv = 1.0 / jnp.maximum(acc[NSA_HD:NSA_HD + 1, :], 1e-30)
            outs.append(group_rows(acc[0:NSA_HD, :] * inv))
        return outs

    reset(1)
    win_tiles = [jnp.where(kc < 0, 0, tile) for kc, tile in zip(near, (4, 2, 1))]
    alpha, pts = softmax_update(1, [sn_ref[1, t] + bt_ref[win_tiles[t]] for t in range(3)])
    value_update(1, alpha, near_kc, pts)
    yield

    reset(0)
    nfar = jnp.maximum(i - 1, 0)
    nfp = nfar // 2
    tail_tiles = [jnp.where((near[0] >= 0) & (nfar - 2 * nfp == 1), 3, 0), jnp.where(near[1] >= 0, 2, 0), 1]
    alpha, pts = softmax_update(0, [sn_ref[0, t] + bt_ref[tail_tiles[t]] + block_mask(2 * near_kc[t])
                                    for t in range(3)])
    value_update(0, alpha, near_kc, pts)
    yield

    def pair_scores(pr, slot):
        for u in range(2):
            s_ref[slot, u] = raw_scores(0, jnp.clip(2 * pr + u, 0, nkc - 1))

    def pair_values(pr, slot):
        value_update(0, al_ref[slot], [jnp.clip(2 * pr + u, 0, nkc - 1) for u in range(2)],
                     [p_ref[slot, u] for u in range(2)])

    def pair_step(pr, slot):
        pair_scores(pr + 1, 1 - slot)
        pair_values(pr - 1, 1 - slot)
        rows = [jnp.where(pr < nfp, 2 * (2 * pr + u), nslc) for u in range(2)]
        alpha, pts = softmax_update(0, [s_ref[slot, u] + block_mask(rows[u]) for u in range(2)])
        al_ref[slot] = alpha
        for u in range(2):
            p_ref[slot, u] = pts[u]

    p_ref[1] = jnp.zeros(p_ref.shape[1:], BF16)
    al_ref[1] = jnp.ones(al_ref.shape[1:], F32)

    def two_pairs(q2, carry):
        pair_step(2 * q2, 0)
        pair_step(2 * q2 + 1, 1)
        return carry

    ntrip = (nfp + 1) // 2
    lax.fori_loop(0, ntrip, two_pairs, 0)
    yield
    pair_values(2 * ntrip - 1, 1)

    o_slc = result(0)
    o_win = result(1)

    sig_hi, sig_lo = _split2(_sigmoid(gt_ref[...]).T)
    out_t = None
    for j, parts in enumerate((o_cmp, o_slc, o_win)):
        gate_t = _dot(egt_ref[j], sig_hi) + _dot(egt_ref[j], sig_lo)
        term = gate_t * jnp.concatenate(parts, axis=0)
        out_t = term if out_t is None else out_t + term
    o_ref[...] = jnp.concatenate([out_t[c * LANES:(c + 1) * LANES].T for c in range(NSA_W // LANES)], axis=1)


def _nsa_part(proj, w1b, peb, w2b, tabs, B, S):
    T = proj.shape[0]
    W = NSA_W
    TQ = NSA_TQ
    nq = S // TQ
    HW = NSA_HEADS * TQ
    c0 = COL_NSA // W
    cg = (COL_NSA + 2 * W + 2 * LANES) // LANES
    biasc, bt, cov, egt = tabs

    def const(shape):
        return pl.BlockSpec(shape, lambda b, i, _n=len(shape): (0,) * _n)

    nstep = nq // 2

    def gens(ins, outs, scr):
        return [_nsa_steps(*ins, *outs, *scr, bs=bs) for bs in range(2)]

    return dict(
        args=[proj, proj, proj, proj, w1b, peb, w2b, biasc, bt, cov, egt], gens=gens,
        out_shape=[jax.ShapeDtypeStruct((T, W), F32)],
        in_specs=[
            pl.BlockSpec((2 * TQ, W), lambda b, i: (b * nstep + i, c0)),
            pl.BlockSpec((S, W), lambda b, i: (b, c0 + 1)),
            pl.BlockSpec((S, 2 * LANES), lambda b, i: (b, (COL_NSA + 2 * W) // (2 * LANES))),
            pl.BlockSpec((2 * TQ, LANES), lambda b, i: (b * nstep + i, cg)),
            const(w1b.shape), const(peb.shape), const(w2b.shape),
            const(biasc.shape), const(bt.shape), const(cov.shape), const(egt.shape),
        ],
        out_specs=[pl.BlockSpec((2 * TQ, W), lambda b, i: (b * nstep + i, 0))],
        scratch=[
            pltpu.VMEM((S // CMP_STRIDE, LANES), BF16),
            pltpu.VMEM((2, V_ROWS, S // CMP_STRIDE), BF16),
            pltpu.VMEM((2, S, LANES), BF16),
            pltpu.VMEM((2, 2, S // TQ, V_ROWS, TQ), BF16),
            pltpu.VMEM((2, 2, S // SLC_BLOCK + 8, TQ), F32),
            pltpu.VMEM((2, 2, 1, HW), F32),
            pltpu.VMEM((2, 2, 2, V_ROWS, HW // 2), F32),
            pltpu.VMEM((S, LANES), F32),
            pltpu.VMEM((2, 2, 2, TQ, HW), F32),
            pltpu.VMEM((2, 2, 2, TQ, HW), BF16),
            pltpu.VMEM((2, 2, 1, HW), F32),
            pltpu.VMEM((2, 2, 3, TQ, HW), F32),
        ])


def _blockdiag2(w):
    z = jnp.zeros_like(w)
    return jnp.concatenate([jnp.concatenate([w, z], axis=-1), jnp.concatenate([z, w], axis=-1)], axis=-2)


def _nsa_params(pe, w1, w2):
    L, Dh = pe.shape[1], pe.shape[2]
    w1b = _blockdiag2(w1.reshape(2, L, Dh, -1)).astype(BF16)
    w2b = _blockdiag2(w2).astype(BF16)
    peb = jnp.concatenate([pe, pe], axis=-1)
    return w1b, peb, w2b


def _t5_bucket_np(n):
    n = np.maximum(n, 0)
    nf = np.maximum(n, 1).astype(np.float64)
    large = 16 + (np.log(nf / 16.0) / math.log(128 / 16) * 16).astype(np.int64)
    return np.where(n < 16, n, np.minimum(large, 31))


def _nsa_tables(rel_bias, S):
    TQ = NSA_TQ
    nq = S // TQ
    rb = rel_bias.astype(F32) * LOG2E

    def tile(dist, valid, shift=False):
        vals = rb[jnp.asarray(_t5_bucket_np(dist))]
        if shift:
            vals = vals - rb[_t5_bucket_np(np.asarray(2 * TQ))]
        vals = jnp.where(jnp.asarray(valid)[..., None], vals, NEG_INF)
        vals = jnp.swapaxes(vals, -1, -2)
        return vals.reshape(vals.shape[:-2] + (NSA_HEADS * TQ,))

    ncmp = S // CMP_STRIDE
    end = np.arange(ncmp) * CMP_STRIDE + CMP_LEN - 1
    cpq = TQ // CMP_STRIDE
    rel_blk = np.arange(cpq * (nq - 1) + ncmp) - cpq * (nq - 1)
    dist_c = np.arange(TQ)[None, :] - (rel_blk[:, None] * CMP_STRIDE + CMP_LEN - 1)
    biasc = tile(dist_c, dist_c >= 0)
    t = np.arange(TQ)[None, :]
    j = np.arange(TQ)[:, None]
    every = np.ones((TQ, TQ), bool)
    bt = jnp.stack([
        tile(t - j, ~every, True),
        tile(t - j, t >= j, True),
        tile(t - j + TQ, every, True),
        tile(t - j + 2 * TQ, every, True),
        tile(t - j + 2 * TQ, j > t, True),
    ])
    assert _t5_bucket_np(np.asarray(TQ + 1)) == _t5_bucket_np(np.asarray(S))
    nslc = S // SLC_BLOCK
    s_lo = np.arange(nslc) * SLC_BLOCK
    start = np.arange(ncmp) * CMP_STRIDE
    cover = ((start[:, None] <= (s_lo + SLC_BLOCK - 1)[None, :]) & (end[:, None] >= s_lo[None, :]))
    cover &= (np.arange(ncmp) < ncmp - 1)[:, None]
    egt = np.zeros((3, NSA_W, LANES), np.float32)
    for jj in range(3):
        egt[jj, np.arange(NSA_W), (np.arange(NSA_W) // NSA_HD) * 3 + jj] = 1.0
    return biasc, bt, jnp.asarray(cover.T.astype(np.float32), BF16), jnp.asarray(egt, BF16)


def _tiles(T, n_ff):
    tm = 512 if T % 512 == 0 else T
    tf = n_ff // 2 if (n_ff // 2) % LANES == 0 else n_ff
    tn = N_PROJ // 4
    return tm, tf, tn


def _pack_w_in(w):
    hg_end = 4 * HG_W
    nsa_end = hg_end + NSA_W + 6 * 2 * NSA_HD + 3 * 8
    rw_end = nsa_end + 3 * RW_W + 256
    w = w.astype(BF16)
    lead = w.shape[:-1]
    parts = [w[..., rw_end:], w[..., :hg_end], w[..., hg_end:nsa_end],
             jnp.zeros(lead + (COL_RW - COL_NSA - (nsa_end - hg_end),), BF16), w[..., nsa_end:rw_end],
             jnp.zeros(lead + (N_PROJ - COL_RW - (rw_end - nsa_end),), BF16)]
    return jnp.concatenate(parts, axis=-1)


def kernel(x, p, ffn1_norm, ffn1_wgu, ffn1_wd, mix_norm, w_in, hg_lb, hg_norm, cmp_pe, cmp_w1, cmp_w2, rel_bias, rw_mu, rw_w0, rw_wB, rw_a0, rw_aB, rw_gB, rw_kk, rw_ka, rw_rk, rw_ln_w, rw_ln_b, w_branch, w_out, ffn2_norm, ffn2_wgu, ffn2_wd, ple_norm, ple_gate_w, ple_w, final_norm):
    B, S, D = x.shape
    depth = ffn1_norm.shape[0]
    T = B * S
    assert D == D_MODEL and S // CMP_STRIDE == LANES and S % NSA_TQ == 0
    assert w_in.shape[2] - 3 * D_MODEL == 4 * HG_W + NSA_W + 12 * NSA_HD + 24 + 3 * RW_W + 256
    tm, tf, tn = _tiles(T, ffn1_wd.shape[1])
    consts = _consts()
    tabs = _nsa_tables(rel_bias, S)
    row = lambda v: v.reshape(1, -1)
    wgu1, wd1, wgu2, wd2 = (w.astype(BF16) for w in (ffn1_wgu, ffn1_wd, ffn2_wgu, ffn2_wd))
    w_proj = _pack_w_in(w_in)
    wb, wo, wpg, wpp = (w.astype(BF16) for w in (w_branch, w_out, ple_gate_w, ple_w))
    p_rows = p.reshape(depth, T, -1)
    h = x.reshape(T, D)
    for i in range(depth):
        h = _ffn(h, row(ffn1_norm[i]), wgu1, wd1, i, tm, tf)
        proj = _proj(h, row(mix_norm[i]), w_proj, i, tm, tn)
        nsa_part = _nsa_part(proj, *_nsa_params(cmp_pe[i], cmp_w1[i], cmp_w2[i]), tabs, B, S)
        hg_part = _hgrn_part(proj, hg_lb, row(hg_norm[i]), consts["j512"], consts["tri16"], consts["tot16"], S, i)
        rw_part = _rwkv_part(
            proj, *_rwkv_params(rw_mu[i], rw_w0[i], rw_wB[i], rw_a0[i], rw_aB[i], rw_gB[i], rw_kk[i], rw_ka[i],
                                rw_rk[i], rw_ln_w[i], rw_ln_b[i]), consts["j512"], consts["tri64"], S)
        grid = (B, S // MIX_ROWS)
        (o_ns,) = _run_parts([nsa_part], grid, "nsa")
        o_hg, o_rw = _run_parts([hg_part, rw_part], grid, "hgrn2_rwkv7")
        h = _merge(h, proj, o_hg, o_ns, o_rw, wb, wo, i, tm)
        h = _ffn(h, row(ffn2_norm[i]), wgu2, wd2, i, tm, tf)
        h = _ple(h, row(ple_norm[i]), wpg, p_rows, wpp, row(final_norm), i, tm, i == depth - 1)
    return h.reshape(B, S, D)
```

```python
import functools
import itertools
import math

import jax
import jax.numpy as jnp
import numpy as np
from jax import lax
from jax.experimental import pallas as pl
from jax.experimental.pallas import tpu as pltpu

F32 = jnp.float32
BF16 = jnp.bfloat16

RMS_EPS = 1e-6
LOG2E = math.log2(math.e)
LANES = 128
VMEM_LIMIT = 48 * 1024 * 1024

HG_W = 512
NSA_W = 512
RW_W = 512
D_MODEL = 1024
MIX_ROWS = 256
COL_MG = 0
COL_HG = 3072
COL_NSA = 5120
COL_RW = 6656
N_PROJ = 8704


def _cparams(sem):
    return pltpu.CompilerParams(dimension_semantics=sem, vmem_limit_bytes=VMEM_LIMIT)


def _rms(x, g):
    return x * lax.rsqrt(jnp.mean(x * x, axis=-1, keepdims=True) + RMS_EPS) * g


def _sigmoid(x):
    return 1.0 / (1.0 + jnp.exp(-x))


def _silu(x):
    return x * _sigmoid(x)


def _dot(a, b):
    return jnp.dot(a, b, preferred_element_type=F32)


def _nt(a, b):
    return lax.dot_general(a, b, (((1,), (1,)), ((), ())), preferred_element_type=F32)


def _ffn_kernel(h_ref, g_ref, wg_ref, wu_ref, wd_ref, o_ref, xn_ref, acc_ref):
    j = pl.program_id(1)

    @pl.when(j == 0)
    def _():
        xn_ref[...] = _rms(h_ref[...], g_ref[...]).astype(BF16)
        acc_ref[...] = jnp.zeros_like(acc_ref)

    xn = xn_ref[...]
    gate = _dot(xn, wg_ref[...])
    up = _dot(xn, wu_ref[...])
    act = (_silu(gate) * up).astype(BF16)
    acc_ref[...] += _dot(act, wd_ref[...])

    @pl.when(j == pl.num_programs(1) - 1)
    def _():
        o_ref[...] = h_ref[...] + 0.5 * acc_ref[...]


def _ffn(h, g, wgu, wd, layer, tm, tf):
    T, D = h.shape
    FF = wd.shape[1]
    nf = FF // tf
    return pl.pallas_call(
        _ffn_kernel,
        out_shape=jax.ShapeDtypeStruct((T, D), F32),
        grid=(T // tm, nf),
        in_specs=[
            pl.BlockSpec((tm, D), lambda i, j: (i, 0)),
            pl.BlockSpec((1, D), lambda i, j: (0, 0)),
            pl.BlockSpec((None, D, tf), lambda i, j: (layer, 0, j)),
            pl.BlockSpec((None, D, tf), lambda i, j: (layer, 0, j + nf)),
            pl.BlockSpec((None, tf, D), lambda i, j: (layer, j, 0)),
        ],
        out_specs=pl.BlockSpec((tm, D), lambda i, j: (i, 0)),
        scratch_shapes=[pltpu.VMEM((tm, D), BF16), pltpu.VMEM((tm, D), F32)],
        compiler_params=_cparams(("parallel", "arbitrary")),
        name="ffn",
    )(h, g, wgu, wgu, wd)


def _proj_kernel(h_ref, g_ref, w_ref, o_ref):
    o_ref[...] = _dot(_rms(h_ref[...], g_ref[...]).astype(BF16), w_ref[...])


def _proj(h, g, w, layer, tm, tn):
    T, D = h.shape
    N = w.shape[2]
    return pl.pallas_call(
        _proj_kernel,
        out_shape=jax.ShapeDtypeStruct((T, N), F32),
        grid=(N // tn, T // tm),
        in_specs=[
            pl.BlockSpec((tm, D), lambda j, i: (i, 0)),
            pl.BlockSpec((1, D), lambda j, i: (0, 0)),
            pl.BlockSpec((None, D, tn), lambda j, i: (layer, 0, j)),
        ],
        out_specs=pl.BlockSpec((tm, tn), lambda j, i: (i, j)),
        compiler_params=_cparams(("parallel", "parallel")),
        name="in_proj",
    )(h, g, w)


def _merge_kernel(h_ref, m0_ref, m1_ref, m2_ref, a_ref, b_ref, c_ref, wb_ref, wo_ref, o_ref):
    merged = _sigmoid(m0_ref[...]) * _dot(a_ref[...].astype(BF16), wb_ref[0])
    merged += _sigmoid(m1_ref[...]) * _dot(b_ref[...].astype(BF16), wb_ref[1])
    merged += _sigmoid(m2_ref[...]) * _dot(c_ref[...].astype(BF16), wb_ref[2])
    o_ref[...] = h_ref[...] + _dot(merged.astype(BF16), wo_ref[...])


def _merge(h, proj, o_hg, o_ns, o_rw, wb, wo, layer, tm):
    T, D = h.shape
    W = o_hg.shape[1]
    mg0 = COL_MG // D
    return pl.pallas_call(
        _merge_kernel,
        out_shape=jax.ShapeDtypeStruct((T, D), F32),
        grid=(T // tm,),
        in_specs=[
            pl.BlockSpec((tm, D), lambda i: (i, 0)),
            pl.BlockSpec((tm, D), lambda i: (i, mg0)),
            pl.BlockSpec((tm, D), lambda i: (i, mg0 + 1)),
            pl.BlockSpec((tm, D), lambda i: (i, mg0 + 2)),
            pl.BlockSpec((tm, W), lambda i: (i, 0)),
            pl.BlockSpec((tm, W), lambda i: (i, 0)),
            pl.BlockSpec((tm, W), lambda i: (i, 0)),
            pl.BlockSpec((None, 3, W, D), lambda i: (layer, 0, 0, 0)),
            pl.BlockSpec((None, D, D), lambda i: (layer, 0, 0)),
        ],
        out_specs=pl.BlockSpec((tm, D), lambda i: (i, 0)),
        compiler_params=_cparams(("parallel",)),
        name="merge",
    )(h, proj, proj, proj, o_hg, o_ns, o_rw, wb, wo)


def _ple_kernel(h_ref, g_ref, wg_ref, p_ref, wp_ref, fg_ref, o_ref, *, final):
    h = h_ref[...]
    gate = _sigmoid(_dot(_rms(h, g_ref[...]).astype(BF16), wg_ref[...]))
    out = h + gate * _dot(p_ref[...].astype(BF16), wp_ref[...])
    if final:
        out = _rms(out, fg_ref[...])
    o_ref[...] = out


def _ple(h, g, wg, p, wp, fg, layer, tm, final):
    T, D = h.shape
    P = p.shape[2]
    return pl.pallas_call(
        functools.partial(_ple_kernel, final=final),
        out_shape=jax.ShapeDtypeStruct((T, D), F32),
        grid=(T // tm,),
        in_specs=[
            pl.BlockSpec((tm, D), lambda i: (i, 0)),
            pl.BlockSpec((1, D), lambda i: (0, 0)),
            pl.BlockSpec((None, D, D), lambda i: (layer, 0, 0)),
            pl.BlockSpec((None, tm, P), lambda i: (layer, i, 0)),
            pl.BlockSpec((None, P, D), lambda i: (layer, 0, 0)),
            pl.BlockSpec((1, D), lambda i: (0, 0)),
        ],
        out_specs=pl.BlockSpec((tm, D), lambda i: (i, 0)),
        compiler_params=_cparams(("parallel",)),
        name="ple",
    )(h, g, wg, p, wp, fg)


def _softplus(x):
    return jnp.maximum(x, 0.0) + jnp.log(1.0 + jnp.exp(-jnp.abs(x)))


def _split2(x):
    hi = x.astype(BF16)
    lo = (x - hi.astype(F32)).astype(BF16)
    return hi, lo


def _split3(x):
    hi = x.astype(BF16)
    r1 = x - hi.astype(F32)
    mid = r1.astype(BF16)
    lo = (r1 - mid.astype(F32)).astype(BF16)
    return hi, mid, lo


def _segsum(x, j):
    hi, lo = _split2(x)
    return _dot(hi, j) + _dot(lo, j)


def _cumsum_rows(tri, x):
    hi, mid, lo = _split3(x)
    return _dot(tri, hi) + (_dot(tri, mid) + _dot(tri, lo))


def _pair_stack(x, lo_mask):
    return jnp.concatenate([jnp.where(lo_mask, x, 0.0), jnp.where(lo_mask, 0.0, x)], axis=0)


def _consts():
    i512 = np.arange(512)
    j512 = (i512[:, None] // 64 == i512[None, :] // 64).astype(np.float32)
    i64 = np.arange(64)
    tri64 = (i64[:, None] >= i64[None, :]).astype(np.float32)
    i128 = np.arange(128)
    same = i128[:, None] // 16 == i128[None, :] // 16
    tri16 = (same & (i128[:, None] >= i128[None, :])).astype(np.float32)
    tot16 = same.astype(np.float32)
    return {"j512": jnp.asarray(j512, BF16), "tri64": jnp.asarray(tri64, BF16),
            "tri16": jnp.asarray(tri16, BF16), "tot16": jnp.asarray(tot16, BF16)}


RW_HD = 64
RW_CH = 64
RW_GN_EPS = 64e-5


def _rwkv_steps(r_ref, k_ref, v_ref, l_ref, mu_ref, vec_ref, wb_ref, ab_ref, gb_ref, j_ref,
                tri_ref, o_ref, carry_ref, st_ref, *, nch):
    C = RW_CH
    TC = nch * C
    W = r_ref.shape[1]
    npair = W // LANES

    row = lax.broadcasted_iota(jnp.int32, (TC, W), 0)

    def shift(x_ref, idx):
        x = x_ref[...]
        prev = jnp.where(row == 0, carry_ref[idx:idx + 1, :], pltpu.roll(x, 1, axis=0))
        carry_ref[idx:idx + 1, :] = x[TC - 1:TC, :]
        return x + (prev - x) * mu_ref[idx:idx + 1, :]

    xr = shift(r_ref, 0)
    xk = shift(k_ref, 1)
    xv = shift(v_ref, 2)
    xl = shift(l_ref, 3)
    w0, a0, k_k, k_a = (vec_ref[i:i + 1, :] for i in range(4))
    ln_w, ln_b, r_k = (vec_ref[i:i + 1, :] for i in range(4, 7))
    jmat = j_ref[...]

    wlal = xl[:, 0:LANES]
    w_pre = w0 + _dot(jnp.tanh(wlal).astype(BF16), wb_ref[...])
    a_pre = a0 + _dot(wlal.astype(BF16), ab_ref[...])
    gate = _dot(_sigmoid(xl[:, LANES:2 * LANES]).astype(BF16), gb_ref[...])
    logw = -jnp.exp(-_softplus(-w_pre) - 0.5)
    a = _sigmoid(a_pre)
    kkr = xk * k_k
    kk = kkr / jnp.maximum(jnp.sqrt(_segsum(kkr * kkr, jmat)), 1e-12)
    k2 = xk * (1.0 + (a - 1.0) * k_a)
    ka = kk * a
    yield

    lane = lax.broadcasted_iota(jnp.int32, (C, LANES), 1)
    trow = lax.broadcasted_iota(jnp.int32, (C, LANES), 0)
    lo_mask = lane < RW_HD
    scol = lane & (RW_HD - 1)
    strict = trow > scol
    incl = trow >= scol
    eye2 = (trow == scol).astype(F32)
    r128 = lax.broadcasted_iota(jnp.int32, (LANES, LANES), 0)
    c128 = lax.broadcasted_iota(jnp.int32, (LANES, LANES), 1)
    bd_mask = (r128 // RW_HD) == (c128 // RW_HD)
    diag_mask = r128 == c128
    tri = tri_ref[...]

    def bf(x):
        return x.astype(BF16)

    def stack(x):
        return _pair_stack(x, lo_mask)

    ops = []
    for c in range(nch):
        rs = slice(c * C, (c + 1) * C)
        lw = logw[rs] * LOG2E
        b = _cumsum_rows(tri, lw)
        bend = b[C - 1:C, :]
        enb = jnp.exp2(-b)
        egc = jnp.exp2(bend - b)
        g_end = jnp.exp2(bend)
        full = (xr[rs] * jnp.exp2(b), k2[rs] * enb, ka[rs] * enb, kk[rs] * jnp.exp2(b - lw), k2[rs] * egc,
                ka[rs] * egc, xv[rs], jnp.broadcast_to(g_end, (C, W)))
        for p in range(npair):
            ops.append(tuple(t[:, p * LANES:(p + 1) * LANES] for t in full))
        yield
    n = len(ops)
    gms = [_nt(bf(jnp.concatenate([bt, rt], axis=0)), bf(jnp.concatenate([stack(at), stack(kt)], axis=0)))
           for rt, kt, at, bt, _, _, _, _ in ops]
    a_ba = [jnp.where(strict, gm[0:C, 0:LANES], 0.0) for gm in gms]
    a_bk = [jnp.where(strict, gm[0:C, LANES:], 0.0) for gm in gms]
    a_ra = [jnp.where(incl, gm[C:, 0:LANES], 0.0) for gm in gms]
    a_rk = [jnp.where(incl, gm[C:, LANES:], 0.0) for gm in gms]
    yield
    pw = [-a for a in a_ba]
    ti = [eye2 + x for x in pw]
    pw = [_dot(bf(x), bf(stack(x))) for x in pw]
    yield
    nsq = int(math.log2(C)) - 1
    for k in range(1, nsq):
        both = [_dot(bf(jnp.concatenate([x, t], axis=0)), bf(stack(x))) for x, t in zip(pw, ti)]
        pw = [m[0:C] for m in both]
        ti = [t + m[C:] for t, m in zip(ti, both)]
        yield
    ti = [t + _dot(bf(t), bf(stack(x))) for t, x in zip(ti, pw)]
    tib = [bf(t) for t in ti]
    yield
    wm = [_dot(tib[j], bf(stack(ops[j][3]))) for j in range(n)]
    av = [_dot(bf(a_bk[j]), bf(stack(ops[j][6]))) for j in range(n)]
    yield
    u0 = [_dot(tib[j], bf(stack(av[j]))) for j in range(n)]
    yield
    y0 = [_dot(bf(jnp.concatenate([a_rk[j], a_ra[j]], axis=1)),
               bf(jnp.concatenate([stack(ops[j][6]), -stack(u0[j])], axis=0))) for j in range(n)]
    rw = [ops[j][0] - _dot(bf(a_ra[j]), bf(stack(wm[j]))) for j in range(n)]
    yield
    m2 = [jnp.where(diag_mask, jnp.concatenate([ops[j][7], ops[j][7]], axis=0), 0.0)
          - jnp.where(bd_mask, _dot(bf(ops[j][5].T), bf(wm[j])), 0.0) for j in range(n)]
    yield
    n2 = [jnp.where(bd_mask, _dot(bf(jnp.concatenate([ops[j][4], ops[j][5]], axis=0).T),
                                  bf(jnp.concatenate([ops[j][6], -u0[j]], axis=0))), 0.0) for j in range(n)]
    yield
    s2 = [st_ref[p] for p in range(npair)]
    ys = []
    for c in range(nch):
        js = [c * npair + p for p in range(npair)]
        ys.append(jnp.concatenate([y0[j] + _dot(bf(rw[j]), bf(s2[p])) for p, j in enumerate(js)], axis=1))
        s2 = [_dot(bf(m2[j]), bf(s2[p])) + n2[j] for p, j in enumerate(js)]
        yield
    for p in range(npair):
        st_ref[p] = s2[p]
    y = jnp.concatenate(ys, axis=0) if nch > 1 else ys[0]

    inv_n = 1.0 / RW_HD
    mean = _segsum(y, jmat) * inv_n
    yc = y - mean
    var = _segsum(yc * yc, jmat) * inv_n
    yn = yc * lax.rsqrt(var + RW_GN_EPS) * ln_w + ln_b
    bonus = _segsum(xr * k2 * r_k, jmat) * xv
    o_ref[...] = (yn + bonus) * gate


def _rwkv_params(mu, w0, wB, a0, aB, gB, k_k, k_a, r_k, ln_w, ln_b):
    W = RW_W
    mu4 = jnp.stack([mu[0:W], mu[W:2 * W], mu[2 * W:3 * W], jnp.pad(mu[3 * W:], (0, W - (mu.shape[0] - 3 * W)))])
    vec = jnp.stack([w0, a0, k_k, k_a, ln_w, ln_b, r_k.reshape(-1), jnp.zeros_like(w0)])
    wb = jnp.pad(wB, ((0, LANES - wB.shape[0]), (0, 0))).astype(BF16)
    ab = jnp.pad(aB, ((LANES - aB.shape[0], 0), (0, 0))).astype(BF16)
    return mu4, vec, wb, ab, gB.astype(BF16)


HG_HD = 64
HG_SUB = 16


def _hgrn_steps(q_ref, f_ref, i_ref, g_ref, lbp_ref, ng_ref, j_ref, tri_ref, tot_ref, o_ref, st_ref,
                *, layer, rows):
    W = q_ref.shape[1]
    TC = rows.stop - rows.start
    npair = W // LANES
    nsub = TC // HG_SUB

    z = f_ref[rows, :]
    log_f = -_softplus(-z)
    k = _sigmoid(-z)
    if layer > 0:
        lbp = lbp_ref[...]
        e = jnp.exp(lbp - jnp.max(lbp, axis=0, keepdims=True))
        sm = e / jnp.sum(e, axis=0, keepdims=True)
        lb = sm[1:2, :]
        for j in range(2, layer + 1):
            lb = lb + sm[j:j + 1, :]
        lb = jnp.maximum(lb, 0.0)
        t2 = jnp.log(lb) - _softplus(z)
        log_f = jnp.maximum(log_f, t2) + jnp.log(1.0 + jnp.exp(-jnp.abs(log_f - t2)))
        k = (1.0 - lb) * k
    q = _silu(q_ref[rows, :])
    v = i_ref[rows, :]
    log_f = log_f * LOG2E
    b = _cumsum_rows(tri_ref[...], log_f)
    bend = _cumsum_rows(tot_ref[...], log_f)
    qe = q * jnp.exp2(b)
    kg = k * jnp.exp2(bend - b)
    jmat = j_ref[...]
    j128 = jmat[0:LANES, 0:LANES]

    rowb = lax.broadcasted_iota(jnp.int32, (TC, LANES), 0)
    trow = lax.broadcasted_iota(jnp.int32, (HG_SUB // 2, W), 0)
    r128 = lax.broadcasted_iota(jnp.int32, (LANES, LANES), 0)
    c128 = lax.broadcasted_iota(jnp.int32, (LANES, LANES), 1)
    bd_mask = (r128 // HG_HD) == (c128 // HG_HD)

    def bf(x):
        return x.astype(BF16)

    vts = [bf(v[:, p * LANES:(p + 1) * LANES].T) for p in range(npair)]
    yield
    outs = []
    H8 = HG_SUB // 2
    for i in range(nsub):
        rs = slice(i * HG_SUB, (i + 1) * HG_SUB)
        r0 = i * HG_SUB
        (b_a, q_a, k_a, v_a), (b_b, q_b, k_b, v_b) = (
            tuple(t[r0 + h * H8:r0 + (h + 1) * H8] for t in (b, q, k, v)) for h in range(2))
        xs = []
        for s in range(H8):
            bs, ks = b_a[s:s + 1, :], k_a[s:s + 1, :]
            xs.append(jnp.where(trow >= s, q_a * (ks * jnp.exp2(b_a - bs)), 0.0))
            xs.append(q_b * (ks * jnp.exp2(b_b - bs)))
        for s in range(H8):
            bs, ks = b_b[s:s + 1, :], k_b[s:s + 1, :]
            xs.append(jnp.where(trow >= s, q_b * (ks * jnp.exp2(b_b - bs)), 0.0))
        x = bf(jnp.concatenate(xs, axis=0))
        g_end = jnp.exp2(bend[i * HG_SUB:i * HG_SUB + 1, :])
        op = []
        for p in range(npair):
            ls = slice(p * LANES, (p + 1) * LANES)
            pm = _dot(x[:, ls], j128)
            od_a = pm[0:H8] * v_a[0:1, ls]
            od_b = pm[H8:HG_SUB] * v_a[0:1, ls]
            for s in range(1, H8):
                od_a = od_a + pm[s * HG_SUB:s * HG_SUB + H8] * v_a[s:s + 1, ls]
                od_b = od_b + pm[s * HG_SUB + H8:(s + 1) * HG_SUB] * v_a[s:s + 1, ls]
            for s in range(H8):
                od_b = od_b + pm[(HG_SUB + s) * H8:(HG_SUB + s + 1) * H8] * v_b[s:s + 1, ls]
            od = jnp.concatenate([od_a, od_b], axis=0)
            st = st_ref[p]
            oi = lax.dot_general(bf(qe[rs, ls]), bf(st), (((1,), (1,)), ((), ())), preferred_element_type=F32)
            kgm = jnp.where((rowb >= i * HG_SUB) & (rowb < (i + 1) * HG_SUB), kg[:, ls], 0.0)
            st_ref[p] = st * g_end[:, ls] + jnp.where(bd_mask, _dot(vts[p], bf(kgm)), 0.0)
            op.append(od + oi)
        outs.append(jnp.concatenate(op, axis=1))
        yield
    o = jnp.concatenate(outs, axis=0)
    ms = _segsum(o * o, jmat) * (1.0 / HG_HD)
    o_ref[rows, :] = o * lax.rsqrt(ms + RMS_EPS) * ng_ref[...] * _silu(g_ref[rows, :])


def _interleave(gens):
    live = list(gens)
    while live:
        for g in list(live):
            try:
                next(g)
            except StopIteration:
                live.remove(g)


def _run_parts(parts, grid, name):
    n_in = [len(p["args"]) for p in parts]
    n_out = [len(p["out_shape"]) for p in parts]
    n_scr = [len(p["scratch"]) for p in parts]

    def take(refs, counts):
        out, pos = [], 0
        for c in counts:
            out.append(refs[pos:pos + c])
            pos += c
        return out

    def kern(*refs):
        ins = take(refs[:sum(n_in)], n_in)
        outs = take(refs[sum(n_in):sum(n_in) + sum(n_out)], n_out)
        scr = take(refs[sum(n_in) + sum(n_out):], n_scr)
        gens = []
        for p, a, b, c in zip(parts, ins, outs, scr):
            gens += p["gens"](a, b, c)
        _interleave(gens)

    cat = lambda key: [x for p in parts for x in p[key]]
    return pl.pallas_call(
        kern, out_shape=cat("out_shape"), grid=grid, in_specs=cat("in_specs"), out_specs=cat("out_specs"),
        scratch_shapes=cat("scratch"), compiler_params=_cparams(("parallel", "arbitrary")), name=name,
    )(*cat("args"))


def _seq_part_specs(proj, col0, consts, S):
    T = proj.shape[0]
    W = HG_W
    nblk = S // MIX_ROWS
    rows = lambda col: pl.BlockSpec((MIX_ROWS, W), lambda b, i: (b * nblk + i, col))
    const = lambda a: pl.BlockSpec(a.shape, lambda b, i: (0, 0))
    c0 = col0 // W
    return dict(args=[proj] * 4 + list(consts), in_specs=[rows(c0 + j) for j in range(4)] + [const(a) for a in consts],
                out_shape=[jax.ShapeDtypeStruct((T, W), F32)], out_specs=[rows(0)])


def _hgrn_part(proj, lbp, ng, jmat, tri16, tot16, S, layer):
    def gens(ins, outs, scr):
        (st_ref,) = scr

        @pl.when(pl.program_id(1) == 0)
        def _():
            st_ref[...] = jnp.zeros_like(st_ref)

        th = ins[7].shape[0]
        return [itertools.chain(*[_hgrn_steps(*ins, *outs, st_ref, layer=layer, rows=slice(r, r + th))
                                  for r in range(0, MIX_ROWS, th)])]

    return dict(_seq_part_specs(proj, COL_HG, (lbp, ng, jmat, tri16, tot16), S), gens=gens,
                scratch=[pltpu.VMEM((HG_W // LANES, LANES, LANES), F32)])


def _rwkv_part(proj, mu, vec, wb, ab, gb, jmat, tri64, S):
    def gens(ins, outs, scr):
        carry_ref, st_ref = scr

        @pl.when(pl.program_id(1) == 0)
        def _():
            carry_ref[...] = jnp.zeros_like(carry_ref)
            st_ref[...] = jnp.zeros_like(st_ref)

        return [_rwkv_steps(*ins, *outs, carry_ref, st_ref, nch=MIX_ROWS // RW_CH)]

    return dict(_seq_part_specs(proj, COL_RW, (mu, vec, wb, ab, gb, jmat, tri64), S), gens=gens,
                scratch=[pltpu.VMEM((8, RW_W), F32), pltpu.VMEM((RW_W // LANES, LANES, LANES), F32)])


NSA_HD = 64
NSA_HEADS = 8
NSA_TQ = 128
CMP_STRIDE = 16
CMP_LEN = 32
SLC_BLOCK = 64
SLC_TOPN = 8
NEG_INF = float("-inf")
V_ROWS = NSA_HD + 16


def _nsa_steps(q_ref, kv1_ref, kv2_ref, gt_ref, w1_ref, pe_ref, w2_ref, biasc_ref, bt_ref, cov_ref, egt_ref,
               o_ref, kc_ref, vct_ref, k_ref, vt_ref, am_ref, m_ref, acc_ref, cx_ref, s_ref, p_ref, al_ref, sn_ref,
               *, bs):
    TQ = NSA_TQ
    S = kv1_ref.shape[0]
    nkc = S // TQ
    ncmp = S // CMP_STRIDE
    nslc = S // SLC_BLOCK
    GW = (NSA_HEADS // 2) * TQ
    i = 2 * pl.program_id(1) + bs
    nq = 2 * pl.num_programs(1)
    q_ref, gt_ref, o_ref = (r.at[bs * TQ:(bs + 1) * TQ] for r in (q_ref, gt_ref, o_ref))
    am_ref, m_ref, acc_ref, s_ref, p_ref, al_ref, sn_ref = (
        r.at[bs] for r in (am_ref, m_ref, acc_ref, s_ref, p_ref, al_ref, sn_ref))
    first_block_only = pl.when(i == 0) if bs == 0 else (lambda f: None)

    def with_ones(vt_g):
        return jnp.concatenate([vt_g, jnp.ones((V_ROWS - NSA_HD, vt_g.shape[1]), F32)], axis=0).astype(BF16)

    @first_block_only
    def _():
        k_ref[0] = kv1_ref[:, 2 * LANES:3 * LANES].astype(BF16)
        k_ref[1] = kv2_ref[:, 0:LANES].astype(BF16)
        for src, (ref, c0) in enumerate(((kv1_ref, 3), (kv2_ref, 1))):
            for c in range(nkc):
                vt = ref[c * TQ:(c + 1) * TQ, c0 * LANES:(c0 + 1) * LANES].T
                for g in range(2):
                    vt_ref[src, g, c] = with_ones(vt[g * NSA_HD:(g + 1) * NSA_HD])
        for t in range(2):
            cx_ref[...] = kv1_ref[:, t * LANES:(t + 1) * LANES]
            acc_a = jnp.zeros((ncmp, 2 * LANES), F32)
            acc_b = jnp.zeros((ncmp, 2 * LANES), F32)
            for l in range(CMP_STRIDE):
                xl = cx_ref[pl.ds(l, ncmp, stride=CMP_STRIDE), :]
                acc_a += _dot((xl + pe_ref[t, l:l + 1, :]).astype(BF16), w1_ref[t, l])
                acc_b += _dot((xl + pe_ref[t, CMP_STRIDE + l:CMP_STRIDE + l + 1, :]).astype(BF16),
                              w1_ref[t, CMP_STRIDE + l])
            hid = _silu(acc_a + pltpu.roll(acc_b, ncmp - 1, axis=0))
            cmp = _dot(hid.astype(BF16), w2_ref[t])
            if t == 0:
                kc_ref[...] = cmp.astype(BF16)
            else:
                ct = cmp.T
                for g in range(2):
                    vct_ref[g] = with_ones(ct[g * NSA_HD:(g + 1) * NSA_HD])

    qs = q_ref[...] * (LOG2E * NSA_HD ** -0.5)
    zero = jnp.zeros((NSA_HD, TQ), F32)
    cols = []
    for pp in range(NSA_HEADS // 2):
        qt = qs[:, pp * LANES:(pp + 1) * LANES].T
        for e in range(2):
            qh = qt[e * NSA_HD:(e + 1) * NSA_HD]
            cols.append(jnp.concatenate([qh, zero] if pp < 2 else [zero, qh], axis=0))
    wq = jnp.concatenate(cols, axis=1).astype(BF16)

    def group_rows(o_t):
        return jnp.concatenate([o_t[:, hh * TQ:(hh + 1) * TQ] for hh in range(NSA_HEADS // 2)], axis=0)

    def raw_scores(br, kc):
        return _dot(k_ref[br, pl.ds(pl.multiple_of(kc * TQ, TQ), TQ), :], wq)

    cpq = TQ // CMP_STRIDE
    sc = _dot(kc_ref[...], wq) + biasc_ref[pl.ds(pl.multiple_of(cpq * (nq - 1 - i), cpq), ncmp), :]
    near = [i - 2, i - 1, i]
    near_kc = [jnp.maximum(kc, 0) for kc in near]
    for t, kc in enumerate(near_kc):
        for br in range(2):
            sn_ref[br, t] = raw_scores(br, kc)
    for u in range(2):
        s_ref[0, u] = raw_scores(0, u)
    yield

    mx = jnp.max(sc, axis=0, keepdims=True)
    ex = jnp.exp2(sc - jnp.where(mx == NEG_INF, 0.0, mx))
    p = ex * (1.0 / jnp.maximum(jnp.sum(ex, axis=0, keepdims=True), 1e-30))
    o_cmp = []
    for g in range(2):
        pg = p[:, g * GW:(g + 1) * GW]
        o_cmp.append(group_rows(_dot(vct_ref[g, 0:NSA_HD, :], pg.astype(BF16))))
        psum = pg[:, 0:TQ]
        for hh in range(1, NSA_HEADS // 2):
            psum = psum + pg[:, hh * TQ:(hh + 1) * TQ]
        hi, lw = _split2(psum)
        imp_t = _dot(cov_ref[...], hi) + _dot(cov_ref[...], lw)
        blk = lax.broadcasted_iota(jnp.int32, (nslc, TQ), 0)
        cur = (i * TQ + lax.broadcasted_iota(jnp.int32, (nslc, TQ), 1)) // SLC_BLOCK
        forced = (blk == 0) | (blk == cur) | (blk == cur - 1)
        score = jnp.where(forced, jnp.inf, jnp.where(blk <= cur, imp_t, NEG_INF))
        cnt = jnp.zeros((nslc, TQ), F32)
        for mp in range(nslc):
            sm = score[mp:mp + 1, :]
            ahead = (sm > score) | ((sm == score) & (blk > mp))
            cnt = cnt + jnp.where(ahead, 1.0, 0.0)
        am_ref[g, 0:nslc] = jnp.where(cnt < SLC_TOPN, 0.0, NEG_INF)
        am_ref[g, nslc:] = jnp.full((am_ref.shape[1] - nslc, TQ), NEG_INF, F32)

    yield

    def reset(br):
        m_ref[br] = jnp.full(m_ref.shape[1:], NEG_INF, F32)
        acc_ref[br] = jnp.zeros(acc_ref.shape[1:], F32)

    def block_mask(r0):
        mk = []
        for g in range(2):
            halves = [jnp.broadcast_to(am_ref[g, pl.ds(r0 + u, 1), :], (SLC_BLOCK, TQ)) for u in range(2)]
            mk.append(jnp.concatenate(halves, axis=0))
        return jnp.concatenate([mk[0]] * (NSA_HEADS // 2) + [mk[1]] * (NSA_HEADS // 2), axis=1)

    def softmax_update(br, ss):
        m_prev = m_ref[br]
        m_new = m_prev
        for s in ss:
            m_new = jnp.maximum(m_new, jnp.max(s, axis=0, keepdims=True))
        m_safe = jnp.where(m_new == NEG_INF, 0.0, m_new)
        m_ref[br] = m_new
        return jnp.exp2(m_prev - m_safe), [jnp.exp2(s - m_safe).astype(BF16) for s in ss]

    def value_update(br, alpha, kcs, pts):
        for g in range(2):
            upd = alpha[:, g * GW:(g + 1) * GW] * acc_ref[br, g]
            for kc, pt in zip(kcs, pts):
                upd = upd + _dot(vt_ref[br, g, kc], pt[:, g * GW:(g + 1) * GW])
            acc_ref[br, g] = upd

    def result(br):
        outs = []
        for g in range(2):
            acc = acc_ref[br, g]
            inv = 1.0 / jnp.maximum(acc[NSA_HD:NSA_HD + 1, :], 1e-30)
            outs.append(group_rows(acc[0:NSA_HD, :] * inv))
        return outs

    reset(1)
    win_tiles = [jnp.where(kc < 0, 0, tile) for kc, tile in zip(near, (4, 2, 1))]
    alpha, pts = softmax_update(1, [sn_ref[1, t] + bt_ref[win_tiles[t]] for t in range(3)])
    value_update(1, alpha, near_kc, pts)
    yield

    reset(0)
    nfar = jnp.maximum(i - 1, 0)
    nfp = nfar // 2
    tail_tiles = [jnp.where((near[0] >= 0) & (nfar - 2 * nfp == 1), 3, 0), jnp.where(near[1] >= 0, 2, 0), 1]
    alpha, pts = softmax_update(0, [sn_ref[0, t] + bt_ref[tail_tiles[t]] + block_mask(2 * near_kc[t])
                                    for t in range(3)])
    value_update(0, alpha, near_kc, pts)
    yield

    def pair_scores(pr, slot):
        for u in range(2):
            s_ref[slot, u] = raw_scores(0, jnp.clip(2 * pr + u, 0, nkc - 1))

    def pair_values(pr, slot):
        value_update(0, al_ref[slot], [jnp.clip(2 * pr + u, 0, nkc - 1) for u in range(2)],
                     [p_ref[slot, u] for u in range(2)])

    def pair_step(pr, slot):
        pair_scores(pr + 1, 1 - slot)
        pair_values(pr - 1, 1 - slot)
        rows = [jnp.where(pr < nfp, 2 * (2 * pr + u), nslc) for u in range(2)]
        alpha, pts = softmax_update(0, [s_ref[slot, u] + block_mask(rows[u]) for u in range(2)])
        al_ref[slot] = alpha
        for u in range(2):
            p_ref[slot, u] = pts[u]

    p_ref[1] = jnp.zeros(p_ref.shape[1:], BF16)
    al_ref[1] = jnp.ones(al_ref.shape[1:], F32)

    def two_pairs(q2, carry):
        pair_step(2 * q2, 0)
        pair_step(2 * q2 + 1, 1)
        return carry

    ntrip = (nfp + 1) // 2
    lax.fori_loop(0, ntrip, two_pairs, 0)
    yield
    pair_values(2 * ntrip - 1, 1)

    o_slc = result(0)
    o_win = result(1)

    sig_hi, sig_lo = _split2(_sigmoid(gt_ref[...]).T)
    out_t = None
    for j, parts in enumerate((o_cmp, o_slc, o_win)):
        gate_t = _dot(egt_ref[j], sig_hi) + _dot(egt_ref[j], sig_lo)
        term = gate_t * jnp.concatenate(parts, axis=0)
        out_t = term if out_t is None else out_t + term
    o_ref[...] = jnp.concatenate([out_t[c * LANES:(c + 1) * LANES].T for c in range(NSA_W // LANES)], axis=1)


def _nsa_part(proj, w1b, peb, w2b, tabs, B, S):
    T = proj.shape[0]
    W = NSA_W
    TQ = NSA_TQ
    nq = S // TQ
    HW = NSA_HEADS * TQ
    c0 = COL_NSA // W
    cg = (COL_NSA + 2 * W + 2 * LANES) // LANES
    biasc, bt, cov, egt = tabs

    def const(shape):
        return pl.BlockSpec(shape, lambda b, i, _n=len(shape): (0,) * _n)

    nstep = nq // 2

    def gens(ins, outs, scr):
        return [_nsa_steps(*ins, *outs, *scr, bs=bs) for bs in range(2)]

    return dict(
        args=[proj, proj, proj, proj, w1b, peb, w2b, biasc, bt, cov, egt], gens=gens,
        out_shape=[jax.ShapeDtypeStruct((T, W), F32)],
        in_specs=[
            pl.BlockSpec((2 * TQ, W), lambda b, i: (b * nstep + i, c0)),
            pl.BlockSpec((S, W), lambda b, i: (b, c0 + 1)),
            pl.BlockSpec((S, 2 * LANES), lambda b, i: (b, (COL_NSA + 2 * W) // (2 * LANES))),
            pl.BlockSpec((2 * TQ, LANES), lambda b, i: (b * nstep + i, cg)),
            const(w1b.shape), const(peb.shape), const(w2b.shape),
            const(biasc.shape), const(bt.shape), const(cov.shape), const(egt.shape),
        ],
        out_specs=[pl.BlockSpec((2 * TQ, W), lambda b, i: (b * nstep + i, 0))],
        scratch=[
            pltpu.VMEM((S // CMP_STRIDE, LANES), BF16),
            pltpu.VMEM((2, V_ROWS, S // CMP_STRIDE), BF16),
            pltpu.VMEM((2, S, LANES), BF16),
            pltpu.VMEM((2, 2, S // TQ, V_ROWS, TQ), BF16),
            pltpu.VMEM((2, 2, S // SLC_BLOCK + 8, TQ), F32),
            pltpu.VMEM((2, 2, 1, HW), F32),
            pltpu.VMEM((2, 2, 2, V_ROWS, HW // 2), F32),
            pltpu.VMEM((S, LANES), F32),
            pltpu.VMEM((2, 2, 2, TQ, HW), F32),
            pltpu.VMEM((2, 2, 2, TQ, HW), BF16),
            pltpu.VMEM((2, 2, 1, HW), F32),
            pltpu.VMEM((2, 2, 3, TQ, HW), F32),
        ])


def _blockdiag2(w):
    z = jnp.zeros_like(w)
    return jnp.concatenate([jnp.concatenate([w, z], axis=-1), jnp.concatenate([z, w], axis=-1)], axis=-2)


def _nsa_params(pe, w1, w2):
    L, Dh = pe.shape[1], pe.shape[2]
    w1b = _blockdiag2(w1.reshape(2, L, Dh, -1)).astype(BF16)
    w2b = _blockdiag2(w2).astype(BF16)
    peb = jnp.concatenate([pe, pe], axis=-1)
    return w1b, peb, w2b


def _t5_bucket_np(n):
    n = np.maximum(n, 0)
    nf = np.maximum(n, 1).astype(np.float64)
    large = 16 + (np.log(nf / 16.0) / math.log(128 / 16) * 16).astype(np.int64)
    return np.where(n < 16, n, np.minimum(large, 31))


def _bias_table_kernel(rb_ref, idx_ref, add_ref, shift_ref, o_ref):
    nb, nh = rb_ref.shape
    idx = idx_ref[...]
    acc = [jnp.zeros(idx.shape, F32) for _ in range(nh)]
    for b in range(nb):
        hit = idx == b
        acc = [jnp.where(hit, rb_ref[b, h], a) for h, a in enumerate(acc)]
    for h in range(nh):
        o_ref[:, h * NSA_TQ:(h + 1) * NSA_TQ] = (acc[h] - shift_ref[...] * rb_ref[nb - 1, h]) * LOG2E + add_ref[...]


def _nsa_tables(rel_bias, S):
    TQ = NSA_TQ
    nq = S // TQ
    tiles = []

    def tile(dist, valid, shift):
        tiles.append((_t5_bucket_np(dist), np.where(valid, 0.0, -np.inf), np.full(dist.shape, float(shift))))

    ncmp = S // CMP_STRIDE
    end = np.arange(ncmp) * CMP_STRIDE + CMP_LEN - 1
    t = np.arange(TQ)[None, :]
    j = np.arange(TQ)[:, None]
    every = np.ones((TQ, TQ), bool)
    tile(t - j, ~every, True)
    tile(t - j, t >= j, True)
    tile(t - j + TQ, every, True)
    tile(t - j + 2 * TQ, every, True)
    tile(t - j + 2 * TQ, j > t, True)
    assert _t5_bucket_np(np.asarray(TQ + 1)) == _t5_bucket_np(np.asarray(S)) == rel_bias.shape[0] - 1
    cpq = TQ // CMP_STRIDE
    rel_blk = np.arange(cpq * (nq - 1) + ncmp) - cpq * (nq - 1)
    dist_c = np.arange(TQ)[None, :] - (rel_blk[:, None] * CMP_STRIDE + CMP_LEN - 1)
    tile(dist_c, dist_c >= 0, False)
    idx, add, shift = (np.concatenate([x[k] for x in tiles], axis=0) for k in range(3))
    full = lambda a: pl.BlockSpec(a.shape, lambda: (0,) * a.ndim)
    table = pl.pallas_call(
        _bias_table_kernel,
        out_shape=jax.ShapeDtypeStruct((idx.shape[0], NSA_HEADS * TQ), F32),
        in_specs=[pl.BlockSpec(memory_space=pltpu.SMEM), full(idx), full(add), full(shift)],
        out_specs=pl.BlockSpec((idx.shape[0], NSA_HEADS * TQ), lambda: (0, 0)),
        name="bias_tables",
    )(rel_bias.astype(F32), jnp.asarray(idx, jnp.int32), jnp.asarray(add, F32), jnp.asarray(shift, F32))
    bt = table[:5 * TQ].reshape(5, TQ, NSA_HEADS * TQ)
    biasc = table[5 * TQ:]
    nslc = S // SLC_BLOCK
    s_lo = np.arange(nslc) * SLC_BLOCK
    start = np.arange(ncmp) * CMP_STRIDE
    cover = ((start[:, None] <= (s_lo + SLC_BLOCK - 1)[None, :]) & (end[:, None] >= s_lo[None, :]))
    cover &= (np.arange(ncmp) < ncmp - 1)[:, None]
    egt = np.zeros((3, NSA_W, LANES), np.float32)
    for jj in range(3):
        egt[jj, np.arange(NSA_W), (np.arange(NSA_W) // NSA_HD) * 3 + jj] = 1.0
    return biasc, bt, jnp.asarray(cover.T.astype(np.float32), BF16), jnp.asarray(egt, BF16)


def _tiles(T, n_ff):
    tm = 512 if T % 512 == 0 else T
    tf = n_ff // 2 if (n_ff // 2) % LANES == 0 else n_ff
    tn = N_PROJ // 4
    return tm, tf, tn


def _pack_w_in(w):
    hg_end = 4 * HG_W
    nsa_end = hg_end + NSA_W + 6 * 2 * NSA_HD + 3 * 8
    rw_end = nsa_end + 3 * RW_W + 256
    out = jnp.zeros(w.shape[:-1] + (N_PROJ,), BF16)
    for dst, lo, hi in ((COL_MG, rw_end, w.shape[-1]), (COL_HG, 0, hg_end), (COL_NSA, hg_end, nsa_end),
                        (COL_RW, nsa_end, rw_end)):
        out = out.at[..., dst:dst + hi - lo].set(w[..., lo:hi].astype(BF16))
    return out


def kernel(x, p, ffn1_norm, ffn1_wgu, ffn1_wd, mix_norm, w_in, hg_lb, hg_norm, cmp_pe, cmp_w1, cmp_w2, rel_bias, rw_mu, rw_w0, rw_wB, rw_a0, rw_aB, rw_gB, rw_kk, rw_ka, rw_rk, rw_ln_w, rw_ln_b, w_branch, w_out, ffn2_norm, ffn2_wgu, ffn2_wd, ple_norm, ple_gate_w, ple_w, final_norm):
    B, S, D = x.shape
    depth = ffn1_norm.shape[0]
    T = B * S
    assert D == D_MODEL and S // CMP_STRIDE == LANES and S % NSA_TQ == 0
    assert w_in.shape[2] - 3 * D_MODEL == 4 * HG_W + NSA_W + 12 * NSA_HD + 24 + 3 * RW_W + 256
    tm, tf, tn = _tiles(T, ffn1_wd.shape[1])
    consts = _consts()
    tabs = _nsa_tables(rel_bias, S)
    row = lambda v: v.reshape(1, -1)
    wgu1, wd1, wgu2, wd2 = (w.astype(BF16) for w in (ffn1_wgu, ffn1_wd, ffn2_wgu, ffn2_wd))
    w_proj = _pack_w_in(w_in)
    wb, wo, wpg, wpp = (w.astype(BF16) for w in (w_branch, w_out, ple_gate_w, ple_w))
    p_rows = p.reshape(depth, T, -1)
    h = x.reshape(T, D)
    for i in range(depth):
        h = _ffn(h, row(ffn1_norm[i]), wgu1, wd1, i, tm, tf)
        proj = _proj(h, row(mix_norm[i]), w_proj, i, tm, tn)
        nsa_part = _nsa_part(proj, *_nsa_params(cmp_pe[i], cmp_w1[i], cmp_w2[i]), tabs, B, S)
        hg_part = _hgrn_part(proj, hg_lb, row(hg_norm[i]), consts["j512"], consts["tri16"], consts["tot16"], S, i)
        rw_part = _rwkv_part(
            proj, *_rwkv_params(rw_mu[i], rw_w0[i], rw_wB[i], rw_a0[i], rw_aB[i], rw_gB[i], rw_kk[i], rw_ka[i],
                                rw_rk[i], rw_ln_w[i], rw_ln_b[i]), consts["j512"], consts["tri64"], S)
        grid = (B, S // MIX_ROWS)
        (o_ns,) = _run_parts([nsa_part], grid, "nsa")
        o_hg, o_rw = _run_parts([hg_part, rw_part], grid, "hgrn2_rwkv7")
        h = _merge(h, proj, o_hg, o_ns, o_rw, wb, wo, i, tm)
        h = _ffn(h, row(ffn2_norm[i]), wgu2, wd2, i, tm, tf)
        h = _ple(h, row(ple_norm[i]), wpg, p_rows, wpp, row(final_norm), i, tm, i == depth - 1)
    return h.reshape(B, S, D)
```

```python
import functools
import itertools
import math

import jax
import jax.numpy as jnp
import numpy as np
from jax import lax
from jax.experimental import pallas as pl
from jax.experimental.pallas import tpu as pltpu

F32 = jnp.float32
BF16 = jnp.bfloat16

RMS_EPS = 1e-6
LOG2E = math.log2(math.e)
LANES = 128
SUBLANES = 8
BF16_ROWS = 16
VMEM_LIMIT = 48 * 1024 * 1024

HG_W = 512
NSA_W = 512
RW_W = 512
D_MODEL = 1024
MIX_ROWS = 512
COL_MG = 0
COL_HG = 3072
COL_NSA = 5120
COL_RW = 6656
N_PROJ = 8704


def _cparams(sem):
    return pltpu.CompilerParams(dimension_semantics=sem, vmem_limit_bytes=VMEM_LIMIT)


def _rms(x, g):
    return x * lax.rsqrt(jnp.mean(x * x, axis=-1, keepdims=True) + RMS_EPS) * g


def _sigmoid(x):
    return 1.0 / (1.0 + jnp.exp(-x))


def _silu(x):
    return x * _sigmoid(x)


def _dot(a, b):
    return jnp.dot(a, b, preferred_element_type=F32)


def _nt(a, b):
    return lax.dot_general(a, b, (((1,), (1,)), ((), ())), preferred_element_type=F32)


def _ffn_kernel(h_ref, g_ref, wg_ref, wu_ref, wd_ref, o_ref, xn_ref, acc_ref):
    j = pl.program_id(1)

    @pl.when(j == 0)
    def _():
        xn_ref[...] = _rms(h_ref[...], g_ref[...]).astype(BF16)
        acc_ref[...] = jnp.zeros_like(acc_ref)

    xn = xn_ref[...]
    gate = _dot(xn, wg_ref[...])
    up = _dot(xn, wu_ref[...])
    act = (_silu(gate) * up).astype(BF16)
    acc_ref[...] += _dot(act, wd_ref[...])

    @pl.when(j == pl.num_programs(1) - 1)
    def _():
        o_ref[...] = h_ref[...] + 0.5 * acc_ref[...]


def _ffn(h, g, wgu, wd, layer, tm, tf):
    T, D = h.shape
    FF = wd.shape[1]
    nf = FF // tf
    return pl.pallas_call(
        _ffn_kernel,
        out_shape=jax.ShapeDtypeStruct((T, D), F32),
        grid=(T // tm, nf),
        in_specs=[
            pl.BlockSpec((tm, D), lambda i, j: (i, 0)),
            pl.BlockSpec((1, D), lambda i, j: (0, 0)),
            pl.BlockSpec((None, D, tf), lambda i, j: (layer, 0, j)),
            pl.BlockSpec((None, D, tf), lambda i, j: (layer, 0, j + nf)),
            pl.BlockSpec((None, tf, D), lambda i, j: (layer, j, 0)),
        ],
        out_specs=pl.BlockSpec((tm, D), lambda i, j: (i, 0)),
        scratch_shapes=[pltpu.VMEM((tm, D), BF16), pltpu.VMEM((tm, D), F32)],
        compiler_params=_cparams(("parallel", "arbitrary")),
        name="ffn",
    )(h, g, wgu, wgu, wd)


def _proj_kernel(h_ref, g_ref, w_ref, o_ref):
    o_ref[...] = _dot(_rms(h_ref[...], g_ref[...]).astype(BF16), w_ref[...])


def _proj(h, g, w, layer, tm, tn):
    T, D = h.shape
    N = w.shape[2]
    return pl.pallas_call(
        _proj_kernel,
        out_shape=jax.ShapeDtypeStruct((T, N), F32),
        grid=(N // tn, T // tm),
        in_specs=[
            pl.BlockSpec((tm, D), lambda j, i: (i, 0)),
            pl.BlockSpec((1, D), lambda j, i: (0, 0)),
            pl.BlockSpec((None, D, tn), lambda j, i: (layer, 0, j)),
        ],
        out_specs=pl.BlockSpec((tm, tn), lambda j, i: (i, j)),
        compiler_params=_cparams(("parallel", "parallel")),
        name="in_proj",
    )(h, g, w)


def _merge_kernel(h_ref, m0_ref, m1_ref, m2_ref, a_ref, b_ref, c_ref, wb_ref, wo_ref, o_ref):
    merged = _sigmoid(m0_ref[...]) * _dot(a_ref[...].astype(BF16), wb_ref[0])
    merged += _sigmoid(m1_ref[...]) * _dot(b_ref[...].astype(BF16), wb_ref[1])
    merged += _sigmoid(m2_ref[...]) * _dot(c_ref[...].astype(BF16), wb_ref[2])
    o_ref[...] = h_ref[...] + _dot(merged.astype(BF16), wo_ref[...])


def _merge(h, proj, o_hg, o_ns, o_rw, wb, wo, layer, tm):
    T, D = h.shape
    W = o_hg.shape[1]
    mg0 = COL_MG // D
    return pl.pallas_call(
        _merge_kernel,
        out_shape=jax.ShapeDtypeStruct((T, D), F32),
        grid=(T // tm,),
        in_specs=[
            pl.BlockSpec((tm, D), lambda i: (i, 0)),
            pl.BlockSpec((tm, D), lambda i: (i, mg0)),
            pl.BlockSpec((tm, D), lambda i: (i, mg0 + 1)),
            pl.BlockSpec((tm, D), lambda i: (i, mg0 + 2)),
            pl.BlockSpec((tm, W), lambda i: (i, 0)),
            pl.BlockSpec((tm, W), lambda i: (i, 0)),
            pl.BlockSpec((tm, W), lambda i: (i, 0)),
            pl.BlockSpec((None, 3, W, D), lambda i: (layer, 0, 0, 0)),
            pl.BlockSpec((None, D, D), lambda i: (layer, 0, 0)),
        ],
        out_specs=pl.BlockSpec((tm, D), lambda i: (i, 0)),
        compiler_params=_cparams(("parallel",)),
        name="merge",
    )(h, proj, proj, proj, o_hg, o_ns, o_rw, wb, wo)


def _ple_kernel(h_ref, g_ref, wg_ref, p_ref, wp_ref, fg_ref, o_ref, *, final):
    h = h_ref[...]
    gate = _sigmoid(_dot(_rms(h, g_ref[...]).astype(BF16), wg_ref[...]))
    out = h + gate * _dot(p_ref[...].astype(BF16), wp_ref[...])
    if final:
        out = _rms(out, fg_ref[...])
    o_ref[...] = out


def _ple(h, g, wg, p, wp, fg, layer, tm, final):
    T, D = h.shape
    P = p.shape[2]
    return pl.pallas_call(
        functools.partial(_ple_kernel, final=final),
        out_shape=jax.ShapeDtypeStruct((T, D), F32),
        grid=(T // tm,),
        in_specs=[
            pl.BlockSpec((tm, D), lambda i: (i, 0)),
            pl.BlockSpec((1, D), lambda i: (0, 0)),
            pl.BlockSpec((None, D, D), lambda i: (layer, 0, 0)),
            pl.BlockSpec((None, tm, P), lambda i: (layer, i, 0)),
            pl.BlockSpec((None, P, D), lambda i: (layer, 0, 0)),
            pl.BlockSpec((1, D), lambda i: (0, 0)),
        ],
        out_specs=pl.BlockSpec((tm, D), lambda i: (i, 0)),
        compiler_params=_cparams(("parallel",)),
        name="ple",
    )(h, g, wg, p, wp, fg)


def _softplus(x):
    return jnp.maximum(x, 0.0) + jnp.log(1.0 + jnp.exp(-jnp.abs(x)))


def _split2(x):
    hi = x.astype(BF16)
    lo = (x - hi.astype(F32)).astype(BF16)
    return hi, lo


def _split3(x):
    hi = x.astype(BF16)
    r1 = x - hi.astype(F32)
    mid = r1.astype(BF16)
    lo = (r1 - mid.astype(F32)).astype(BF16)
    return hi, mid, lo


def _segsum(x, j):
    hi, lo = _split2(x)
    return _dot(hi, j) + _dot(lo, j)


def _cumsum_rows(tri, x):
    hi, mid, lo = _split3(x)
    return _dot(tri, hi) + (_dot(tri, mid) + _dot(tri, lo))


def _pair_stack(x, lo_mask):
    return jnp.concatenate([jnp.where(lo_mask, x, 0.0), jnp.where(lo_mask, 0.0, x)], axis=0)


def _consts():
    lane = np.arange(HG_W)
    j512 = (lane[:, None] // HG_HD == lane[None, :] // HG_HD).astype(np.float32)
    step = np.arange(RW_CH)
    tri64 = (step[:, None] >= step[None, :]).astype(np.float32)
    r = np.arange(HG_ROWS)
    same = r[:, None] // HG_SUB == r[None, :] // HG_SUB
    tri16 = (same & (r[:, None] >= r[None, :])).astype(np.float32)
    tot16 = same.astype(np.float32)
    return {"j512": jnp.asarray(j512, BF16), "tri64": jnp.asarray(tri64, BF16),
            "tri16": jnp.asarray(tri16, BF16), "tot16": jnp.asarray(tot16, BF16)}


RW_HD = 64
RW_CH = 64
RW_GN_EPS = 64e-5


def _rwkv_steps(r_ref, k_ref, v_ref, l_ref, mu_ref, vec_ref, wb_ref, ab_ref, gb_ref, j_ref,
                tri_ref, o_ref, carry_ref, st_ref, *, nch):
    C = RW_CH
    TC = nch * C
    W = r_ref.shape[1]
    npair = W // LANES

    row = lax.broadcasted_iota(jnp.int32, (TC, W), 0)

    def shift(x_ref, idx):
        x = x_ref[...]
        prev = jnp.where(row == 0, carry_ref[idx:idx + 1, :], pltpu.roll(x, 1, axis=0))
        carry_ref[idx:idx + 1, :] = x[TC - 1:TC, :]
        return x + (prev - x) * mu_ref[idx:idx + 1, :]

    xr = shift(r_ref, 0)
    xk = shift(k_ref, 1)
    xv = shift(v_ref, 2)
    xl = shift(l_ref, 3)
    w0, a0, k_k, k_a = (vec_ref[i:i + 1, :] for i in range(4))
    ln_w, ln_b, r_k = (vec_ref[i:i + 1, :] for i in range(4, 7))
    jmat = j_ref[...]

    wlal = xl[:, 0:LANES]
    w_pre = w0 + _dot(jnp.tanh(wlal).astype(BF16), wb_ref[...])
    a_pre = a0 + _dot(wlal.astype(BF16), ab_ref[...])
    gate = _dot(_sigmoid(xl[:, LANES:2 * LANES]).astype(BF16), gb_ref[...])
    logw = -jnp.exp(-_softplus(-w_pre) - 0.5)
    a = _sigmoid(a_pre)
    kkr = xk * k_k
    kk = kkr / jnp.maximum(jnp.sqrt(_segsum(kkr * kkr, jmat)), 1e-12)
    k2 = xk * (1.0 + (a - 1.0) * k_a)
    ka = kk * a
    yield

    lane = lax.broadcasted_iota(jnp.int32, (C, LANES), 1)
    trow = lax.broadcasted_iota(jnp.int32, (C, LANES), 0)
    lo_mask = lane < RW_HD
    scol = lane & (RW_HD - 1)
    strict = trow > scol
    incl = trow >= scol
    eye2 = (trow == scol).astype(F32)
    r128 = lax.broadcasted_iota(jnp.int32, (LANES, LANES), 0)
    c128 = lax.broadcasted_iota(jnp.int32, (LANES, LANES), 1)
    bd_mask = (r128 // RW_HD) == (c128 // RW_HD)
    diag_mask = r128 == c128
    tri = tri_ref[...]

    def bf(x):
        return x.astype(BF16)

    def stack(x):
        return _pair_stack(x, lo_mask)

    ops = []
    for c in range(nch):
        rs = slice(c * C, (c + 1) * C)
        lw = logw[rs] * LOG2E
        b = _cumsum_rows(tri, lw)
        bend = b[C - 1:C, :]
        enb = jnp.exp2(-b)
        egc = jnp.exp2(bend - b)
        g_end = jnp.exp2(bend)
        full = (xr[rs] * jnp.exp2(b), k2[rs] * enb, ka[rs] * enb, kk[rs] * jnp.exp2(b - lw), k2[rs] * egc,
                ka[rs] * egc, xv[rs], jnp.broadcast_to(g_end, (C, W)))
        for p in range(npair):
            ops.append(tuple(t[:, p * LANES:(p + 1) * LANES] for t in full))
        yield
    n = len(ops)
    gms = [_nt(bf(jnp.concatenate([bt, rt], axis=0)), bf(jnp.concatenate([stack(at), stack(kt)], axis=0)))
           for rt, kt, at, bt, _, _, _, _ in ops]
    a_ba = [jnp.where(strict, gm[0:C, 0:LANES], 0.0) for gm in gms]
    a_bk = [jnp.where(strict, gm[0:C, LANES:], 0.0) for gm in gms]
    a_ra = [jnp.where(incl, gm[C:, 0:LANES], 0.0) for gm in gms]
    a_rk = [jnp.where(incl, gm[C:, LANES:], 0.0) for gm in gms]
    yield
    pw = [-a for a in a_ba]
    ti = [eye2 + x for x in pw]
    pw = [_dot(bf(x), bf(stack(x))) for x in pw]
    yield
    nsq = int(math.log2(C)) - 1
    for k in range(1, nsq):
        both = [_dot(bf(jnp.concatenate([x, t], axis=0)), bf(stack(x))) for x, t in zip(pw, ti)]
        pw = [m[0:C] for m in both]
        ti = [t + m[C:] for t, m in zip(ti, both)]
        yield
    ti = [t + _dot(bf(t), bf(stack(x))) for t, x in zip(ti, pw)]
    tib = [bf(t) for t in ti]
    yield
    wm = [_dot(tib[j], bf(stack(ops[j][3]))) for j in range(n)]
    av = [_dot(bf(a_bk[j]), bf(stack(ops[j][6]))) for j in range(n)]
    yield
    u0 = [_dot(tib[j], bf(stack(av[j]))) for j in range(n)]
    yield
    y0 = [_dot(bf(jnp.concatenate([a_rk[j], a_ra[j]], axis=1)),
               bf(jnp.concatenate([stack(ops[j][6]), -stack(u0[j])], axis=0))) for j in range(n)]
    rw = [ops[j][0] - _dot(bf(a_ra[j]), bf(stack(wm[j]))) for j in range(n)]
    yield
    m2 = [jnp.where(diag_mask, jnp.concatenate([ops[j][7], ops[j][7]], axis=0), 0.0)
          - jnp.where(bd_mask, _dot(bf(ops[j][5].T), bf(wm[j])), 0.0) for j in range(n)]
    yield
    n2 = [jnp.where(bd_mask, _dot(bf(jnp.concatenate([ops[j][4], ops[j][5]], axis=0).T),
                                  bf(jnp.concatenate([ops[j][6], -u0[j]], axis=0))), 0.0) for j in range(n)]
    yield
    s2 = [st_ref[p] for p in range(npair)]
    ys = []
    for c in range(nch):
        js = [c * npair + p for p in range(npair)]
        ys.append(jnp.concatenate([y0[j] + _dot(bf(rw[j]), bf(s2[p])) for p, j in enumerate(js)], axis=1))
        s2 = [_dot(bf(m2[j]), bf(s2[p])) + n2[j] for p, j in enumerate(js)]
        yield
    for p in range(npair):
        st_ref[p] = s2[p]
    y = jnp.concatenate(ys, axis=0) if nch > 1 else ys[0]

    inv_n = 1.0 / RW_HD
    mean = _segsum(y, jmat) * inv_n
    yc = y - mean
    var = _segsum(yc * yc, jmat) * inv_n
    yn = yc * lax.rsqrt(var + RW_GN_EPS) * ln_w + ln_b
    bonus = _segsum(xr * k2 * r_k, jmat) * xv
    o_ref[...] = (yn + bonus) * gate


def _rwkv_params(mu, w0, wB, a0, aB, gB, k_k, k_a, r_k, ln_w, ln_b):
    W = RW_W
    mu4 = jnp.stack([mu[0:W], mu[W:2 * W], mu[2 * W:3 * W], jnp.pad(mu[3 * W:], (0, W - (mu.shape[0] - 3 * W)))])
    vec = jnp.stack([w0, a0, k_k, k_a, ln_w, ln_b, r_k.reshape(-1), jnp.zeros_like(w0)])
    wb = jnp.pad(wB, ((0, LANES - wB.shape[0]), (0, 0))).astype(BF16)
    ab = jnp.pad(aB, ((LANES - aB.shape[0], 0), (0, 0))).astype(BF16)
    return mu4, vec, wb, ab, gB.astype(BF16)


HG_HD = 64
HG_SUB = 16
HG_ROWS = 128


def _hgrn_steps(q_ref, f_ref, i_ref, g_ref, lbp_ref, ng_ref, j_ref, tri_ref, tot_ref, o_ref, st_ref,
                *, layer, rows):
    W = q_ref.shape[1]
    TC = rows.stop - rows.start
    npair = W // LANES
    nsub = TC // HG_SUB

    z = f_ref[rows, :]
    sp = _softplus(-z)
    log_f = -sp
    k = _sigmoid(-z)
    if layer > 0:
        lbp = lbp_ref[...]
        e = jnp.exp(lbp - jnp.max(lbp, axis=0, keepdims=True))
        sm = e / jnp.sum(e, axis=0, keepdims=True)
        lb = sm[1:2, :]
        for j in range(2, layer + 1):
            lb = lb + sm[j:j + 1, :]
        lb = jnp.maximum(lb, 0.0)
        t2 = jnp.log(lb) - (sp + z)
        log_f = jnp.maximum(log_f, t2) + jnp.log(1.0 + jnp.exp(-jnp.abs(log_f - t2)))
        k = (1.0 - lb) * k
    q = _silu(q_ref[rows, :])
    v = i_ref[rows, :]
    log_f = log_f * LOG2E
    b = _cumsum_rows(tri_ref[...], log_f)
    bend = _cumsum_rows(tot_ref[...], log_f)
    qe = q * jnp.exp2(b)
    kg = k * jnp.exp2(bend - b)
    jmat = j_ref[...]
    j128 = jmat[0:LANES, 0:LANES]

    rowb = lax.broadcasted_iota(jnp.int32, (TC, LANES), 0)
    trow = lax.broadcasted_iota(jnp.int32, (HG_SUB // 2, W), 0)
    r128 = lax.broadcasted_iota(jnp.int32, (LANES, LANES), 0)
    c128 = lax.broadcasted_iota(jnp.int32, (LANES, LANES), 1)
    bd_mask = (r128 // HG_HD) == (c128 // HG_HD)

    def bf(x):
        return x.astype(BF16)

    vts = [bf(v[:, p * LANES:(p + 1) * LANES].T) for p in range(npair)]
    yield
    outs = []
    H8 = HG_SUB // 2
    for i in range(nsub):
        rs = slice(i * HG_SUB, (i + 1) * HG_SUB)
        r0 = i * HG_SUB
        (b_a, q_a, k_a, v_a), (b_b, q_b, k_b, v_b) = (
            tuple(t[r0 + h * H8:r0 + (h + 1) * H8] for t in (b, q, k, v)) for h in range(2))
        xs = []
        for s in range(H8):
            bs, ks = b_a[s:s + 1, :], k_a[s:s + 1, :]
            xs.append(jnp.where(trow >= s, q_a * (ks * jnp.exp2(b_a - bs)), 0.0))
            xs.append(q_b * (ks * jnp.exp2(b_b - bs)))
        for s in range(H8):
            bs, ks = b_b[s:s + 1, :], k_b[s:s + 1, :]
            xs.append(jnp.where(trow >= s, q_b * (ks * jnp.exp2(b_b - bs)), 0.0))
        x = bf(jnp.concatenate(xs, axis=0))
        g_end = jnp.exp2(bend[i * HG_SUB:i * HG_SUB + 1, :])
        op = []
        for p in range(npair):
            ls = slice(p * LANES, (p + 1) * LANES)
            pm = _dot(x[:, ls], j128)
            od_a = pm[0:H8] * v_a[0:1, ls]
            od_b = pm[H8:HG_SUB] * v_a[0:1, ls]
            for s in range(1, H8):
                od_a = od_a + pm[s * HG_SUB:s * HG_SUB + H8] * v_a[s:s + 1, ls]
                od_b = od_b + pm[s * HG_SUB + H8:(s + 1) * HG_SUB] * v_a[s:s + 1, ls]
            for s in range(H8):
                od_b = od_b + pm[(HG_SUB + s) * H8:(HG_SUB + s + 1) * H8] * v_b[s:s + 1, ls]
            od = jnp.concatenate([od_a, od_b], axis=0)
            st = st_ref[p]
            oi = lax.dot_general(bf(qe[rs, ls]), bf(st), (((1,), (1,)), ((), ())), preferred_element_type=F32)
            kgm = jnp.where((rowb >= i * HG_SUB) & (rowb < (i + 1) * HG_SUB), kg[:, ls], 0.0)
            st_ref[p] = st * g_end[:, ls] + jnp.where(bd_mask, _dot(vts[p], bf(kgm)), 0.0)
            op.append(od + oi)
        outs.append(jnp.concatenate(op, axis=1))
        yield
    o = jnp.concatenate(outs, axis=0)
    ms = _segsum(o * o, jmat) * (1.0 / HG_HD)
    o_ref[rows, :] = o * lax.rsqrt(ms + RMS_EPS) * ng_ref[...] * _silu(g_ref[rows, :])


def _interleave(gens):
    live = list(gens)
    while live:
        for g in list(live):
            try:
                next(g)
            except StopIteration:
                live.remove(g)


def _run_parts(parts, grid, name):
    n_in = [len(p["args"]) for p in parts]
    n_out = [len(p["out_shape"]) for p in parts]
    n_scr = [len(p["scratch"]) for p in parts]

    def take(refs, counts):
        out, pos = [], 0
        for c in counts:
            out.append(refs[pos:pos + c])
            pos += c
        return out

    def kern(*refs):
        ins = take(refs[:sum(n_in)], n_in)
        outs = take(refs[sum(n_in):sum(n_in) + sum(n_out)], n_out)
        scr = take(refs[sum(n_in) + sum(n_out):], n_scr)
        gens = []
        for p, a, b, c in zip(parts, ins, outs, scr):
            gens += p["gens"](a, b, c)
        _interleave(gens)

    cat = lambda key: [x for p in parts for x in p[key]]
    return pl.pallas_call(
        kern, out_shape=cat("out_shape"), grid=grid, in_specs=cat("in_specs"), out_specs=cat("out_specs"),
        scratch_shapes=cat("scratch"), compiler_params=_cparams(("parallel", "arbitrary")), name=name,
    )(*cat("args"))


def _seq_part_specs(proj, col0, consts, S):
    T = proj.shape[0]
    W = HG_W
    nblk = S // MIX_ROWS
    rows = lambda col: pl.BlockSpec((MIX_ROWS, W), lambda b, i: (b * nblk + i, col))
    const = lambda a: pl.BlockSpec(a.shape, lambda b, i: (0, 0))
    c0 = col0 // W
    return dict(args=[proj] * 4 + list(consts), in_specs=[rows(c0 + j) for j in range(4)] + [const(a) for a in consts],
                out_shape=[jax.ShapeDtypeStruct((T, W), F32)], out_specs=[rows(0)])


def _hgrn_part(proj, lbp, ng, jmat, tri16, tot16, S, layer):
    def gens(ins, outs, scr):
        (st_ref,) = scr

        @pl.when(pl.program_id(1) == 0)
        def _():
            st_ref[...] = jnp.zeros_like(st_ref)

        th = ins[7].shape[0]
        return [itertools.chain(*[_hgrn_steps(*ins, *outs, st_ref, layer=layer, rows=slice(r, r + th))
                                  for r in range(0, MIX_ROWS, th)])]

    return dict(_seq_part_specs(proj, COL_HG, (lbp, ng, jmat, tri16, tot16), S), gens=gens,
                scratch=[pltpu.VMEM((HG_W // LANES, LANES, LANES), F32)])


def _rwkv_part(proj, mu, vec, wb, ab, gb, jmat, tri64, S):
    def gens(ins, outs, scr):
        carry_ref, st_ref = scr

        @pl.when(pl.program_id(1) == 0)
        def _():
            carry_ref[...] = jnp.zeros_like(carry_ref)
            st_ref[...] = jnp.zeros_like(st_ref)

        return [_rwkv_steps(*ins, *outs, carry_ref, st_ref, nch=MIX_ROWS // RW_CH)]

    return dict(_seq_part_specs(proj, COL_RW, (mu, vec, wb, ab, gb, jmat, tri64), S), gens=gens,
                scratch=[pltpu.VMEM((SUBLANES, RW_W), F32), pltpu.VMEM((RW_W // LANES, LANES, LANES), F32)])


NSA_HD = 64
NSA_HEADS = 8
NSA_TQ = 128
CMP_STRIDE = 16
CMP_LEN = 32
SLC_BLOCK = 64
SLC_TOPN = 8
NEG_INF = float("-inf")
V_ROWS = NSA_HD + BF16_ROWS


def _nsa_steps(q_ref, kv1_ref, kv2_ref, gt_ref, w1_ref, pe_ref, w2_ref, biasc_ref, bt_ref, cov_ref, egt_ref,
               o_ref, kc_ref, vct_ref, k_ref, vt_ref, am_ref, m_ref, acc_ref, cx_ref, s_ref, p_ref, al_ref, sn_ref,
               *, bs):
    TQ = NSA_TQ
    S = kv1_ref.shape[0]
    nkc = S // TQ
    ncmp = S // CMP_STRIDE
    nslc = S // SLC_BLOCK
    GW = (NSA_HEADS // 2) * TQ
    i = 2 * pl.program_id(1) + bs
    nq = 2 * pl.num_programs(1)
    q_ref, gt_ref, o_ref = (r.at[bs * TQ:(bs + 1) * TQ] for r in (q_ref, gt_ref, o_ref))
    am_ref, m_ref, acc_ref, s_ref, p_ref, al_ref, sn_ref = (
        r.at[bs] for r in (am_ref, m_ref, acc_ref, s_ref, p_ref, al_ref, sn_ref))
    first_block_only = pl.when(i == 0) if bs == 0 else (lambda f: None)

    def with_ones(vt_g):
        return jnp.concatenate([vt_g, jnp.ones((V_ROWS - NSA_HD, vt_g.shape[1]), F32)], axis=0).astype(BF16)

    @first_block_only
    def _():
        k_ref[0] = kv1_ref[:, 2 * LANES:3 * LANES].astype(BF16)
        k_ref[1] = kv2_ref[:, 0:LANES].astype(BF16)
        for src, (ref, c0) in enumerate(((kv1_ref, 3), (kv2_ref, 1))):
            for c in range(nkc):
                vt = ref[c * TQ:(c + 1) * TQ, c0 * LANES:(c0 + 1) * LANES].T
                for g in range(2):
                    vt_ref[src, g, c] = with_ones(vt[g * NSA_HD:(g + 1) * NSA_HD])
        for t in range(2):
            cx_ref[...] = kv1_ref[:, t * LANES:(t + 1) * LANES]
            acc_a = jnp.zeros((ncmp, 2 * LANES), F32)
            acc_b = jnp.zeros((ncmp, 2 * LANES), F32)
            for l in range(CMP_STRIDE):
                xl = cx_ref[pl.ds(l, ncmp, stride=CMP_STRIDE), :]
                acc_a += _dot((xl + pe_ref[t, l:l + 1, :]).astype(BF16), w1_ref[t, l])
                acc_b += _dot((xl + pe_ref[t, CMP_STRIDE + l:CMP_STRIDE + l + 1, :]).astype(BF16),
                              w1_ref[t, CMP_STRIDE + l])
            hid = _silu(acc_a + pltpu.roll(acc_b, ncmp - 1, axis=0))
            cmp = _dot(hid.astype(BF16), w2_ref[t])
            if t == 0:
                kc_ref[...] = cmp.astype(BF16)
            else:
                ct = cmp.T
                for g in range(2):
                    vct_ref[g] = with_ones(ct[g * NSA_HD:(g + 1) * NSA_HD])

    qs = q_ref[...] * (LOG2E * NSA_HD ** -0.5)
    zero = jnp.zeros((NSA_HD, TQ), F32)
    cols = []
    for pp in range(NSA_HEADS // 2):
        qt = qs[:, pp * LANES:(pp + 1) * LANES].T
        for e in range(2):
            qh = qt[e * NSA_HD:(e + 1) * NSA_HD]
            cols.append(jnp.concatenate([qh, zero] if pp < 2 else [zero, qh], axis=0))
    wq = jnp.concatenate(cols, axis=1).astype(BF16)

    def group_rows(o_t):
        return jnp.concatenate([o_t[:, hh * TQ:(hh + 1) * TQ] for hh in range(NSA_HEADS // 2)], axis=0)

    def raw_scores(br, kc):
        return _dot(k_ref[br, pl.ds(pl.multiple_of(kc * TQ, TQ), TQ), :], wq)

    cpq = TQ // CMP_STRIDE
    sc = _dot(kc_ref[...], wq) + biasc_ref[pl.ds(pl.multiple_of(cpq * (nq - 1 - i), cpq), ncmp), :]
    near = [i - 2, i - 1, i]
    near_kc = [jnp.maximum(kc, 0) for kc in near]
    for t, kc in enumerate(near_kc):
        for br in range(2):
            sn_ref[br, t] = raw_scores(br, kc)
    for u in range(2):
        s_ref[0, u] = raw_scores(0, u)
    yield

    mx = jnp.max(sc, axis=0, keepdims=True)
    ex = jnp.exp2(sc - jnp.where(mx == NEG_INF, 0.0, mx))
    p = ex * (1.0 / jnp.maximum(jnp.sum(ex, axis=0, keepdims=True), 1e-30))
    o_cmp = []
    for g in range(2):
        pg = p[:, g * GW:(g + 1) * GW]
        o_cmp.append(group_rows(_dot(vct_ref[g, 0:NSA_HD, :], pg.astype(BF16))))
        psum = pg[:, 0:TQ]
        for hh in range(1, NSA_HEADS // 2):
            psum = psum + pg[:, hh * TQ:(hh + 1) * TQ]
        hi, lw = _split2(psum)
        imp_t = _dot(cov_ref[...], hi) + _dot(cov_ref[...], lw)
        blk = lax.broadcasted_iota(jnp.int32, (nslc, TQ), 0)
        cur = (i * TQ + lax.broadcasted_iota(jnp.int32, (nslc, TQ), 1)) // SLC_BLOCK
        forced = (blk == 0) | (blk == cur) | (blk == cur - 1)
        score = jnp.where(forced, jnp.inf, jnp.where(blk <= cur, imp_t, NEG_INF))
        cnt = jnp.zeros((nslc, TQ), F32)
        for mp in range(nslc):
            sm = score[mp:mp + 1, :]
            ahead = (sm > score) | ((sm == score) & (blk > mp))
            cnt = cnt + jnp.where(ahead, 1.0, 0.0)
        am_ref[g, 0:nslc] = jnp.where(cnt < SLC_TOPN, 0.0, NEG_INF)
        am_ref[g, nslc:] = jnp.full((am_ref.shape[1] - nslc, TQ), NEG_INF, F32)

    yield

    def reset(br):
        m_ref[br] = jnp.full(m_ref.shape[1:], NEG_INF, F32)
        acc_ref[br] = jnp.zeros(acc_ref.shape[1:], F32)

    def block_mask(r0):
        mk = []
        for g in range(2):
            halves = [jnp.broadcast_to(am_ref[g, pl.ds(r0 + u, 1), :], (SLC_BLOCK, TQ)) for u in range(2)]
            mk.append(jnp.concatenate(halves, axis=0))
        return jnp.concatenate([mk[0]] * (NSA_HEADS // 2) + [mk[1]] * (NSA_HEADS // 2), axis=1)

    def softmax_update(br, ss):
        m_prev = m_ref[br]
        m_new = m_prev
        for s in ss:
            m_new = jnp.maximum(m_new, jnp.max(s, axis=0, keepdims=True))
        m_safe = jnp.where(m_new == NEG_INF, 0.0, m_new)
        m_ref[br] = m_new
        return jnp.exp2(m_prev - m_safe), [jnp.exp2(s - m_safe).astype(BF16) for s in ss]

    def value_update(br, alpha, kcs, pts):
        for g in range(2):
            upd = alpha[:, g * GW:(g + 1) * GW] * acc_ref[br, g]
            for kc, pt in zip(kcs, pts):
                upd = upd + _dot(vt_ref[br, g, kc], pt[:, g * GW:(g + 1) * GW])
            acc_ref[br, g] = upd

    def result(br):
        outs = []
        for g in range(2):
            acc = acc_ref[br, g]
            inv = 1.0 / jnp.maximum(acc[NSA_HD:NSA_HD + 1, :], 1e-30)
            outs.append(group_rows(acc[0:NSA_HD, :] * inv))
        return outs

    reset(1)
    win_tiles = [jnp.where(kc < 0, 0, tile) for kc, tile in zip(near, (4, 2, 1))]
    alpha, pts = softmax_update(1, [sn_ref[1, t] + bt_ref[win_tiles[t]] for t in range(3)])
    value_update(1, alpha, near_kc, pts)
    yield

    reset(0)
    nfar = jnp.maximum(i - 1, 0)
    nfp = nfar // 2
    tail_tiles = [jnp.where((near[0] >= 0) & (nfar - 2 * nfp == 1), 3, 0), jnp.where(near[1] >= 0, 2, 0), 1]
    alpha, pts = softmax_update(0, [sn_ref[0, t] + bt_ref[tail_tiles[t]] + block_mask(2 * near_kc[t])
                                    for t in range(3)])
    value_update(0, alpha, near_kc, pts)
    yield

    def pair_scores(pr, slot):
        for u in range(2):
            s_ref[slot, u] = raw_scores(0, jnp.clip(2 * pr + u, 0, nkc - 1))

    def pair_values(pr, slot):
        value_update(0, al_ref[slot], [jnp.clip(2 * pr + u, 0, nkc - 1) for u in range(2)],
                     [p_ref[slot, u] for u in range(2)])

    def pair_step(pr, slot):
        pair_scores(pr + 1, 1 - slot)
        pair_values(pr - 1, 1 - slot)
        rows = [jnp.where(pr < nfp, 2 * (2 * pr + u), nslc) for u in range(2)]
        alpha, pts = softmax_update(0, [s_ref[slot, u] + block_mask(rows[u]) for u in range(2)])
        al_ref[slot] = alpha
        for u in range(2):
            p_ref[slot, u] = pts[u]

    p_ref[1] = jnp.zeros(p_ref.shape[1:], BF16)
    al_ref[1] = jnp.ones(al_ref.shape[1:], F32)

    def two_pairs(q2, carry):
        pair_step(2 * q2, 0)
        pair_step(2 * q2 + 1, 1)
        return carry

    ntrip = (nfp + 1) // 2
    lax.fori_loop(0, ntrip, two_pairs, 0)
    yield
    pair_values(2 * ntrip - 1, 1)

    o_slc = result(0)
    o_win = result(1)

    sig_hi, sig_lo = _split2(_sigmoid(gt_ref[...]).T)
    out_t = None
    for j, parts in enumerate((o_cmp, o_slc, o_win)):
        gate_t = _dot(egt_ref[j], sig_hi) + _dot(egt_ref[j], sig_lo)
        term = gate_t * jnp.concatenate(parts, axis=0)
        out_t = term if out_t is None else out_t + term
    o_ref[...] = jnp.concatenate([out_t[c * LANES:(c + 1) * LANES].T for c in range(NSA_W // LANES)], axis=1)


def _nsa_part(proj, w1b, peb, w2b, tabs, S):
    T = proj.shape[0]
    W = NSA_W
    TQ = NSA_TQ
    nq = S // TQ
    HW = NSA_HEADS * TQ
    c0 = COL_NSA // W
    cg = (COL_NSA + 2 * W + 2 * LANES) // LANES
    biasc, bt, cov, egt = tabs

    def const(shape):
        return pl.BlockSpec(shape, lambda b, i, _n=len(shape): (0,) * _n)

    nstep = nq // 2

    def gens(ins, outs, scr):
        return [_nsa_steps(*ins, *outs, *scr, bs=bs) for bs in range(2)]

    return dict(
        args=[proj, proj, proj, proj, w1b, peb, w2b, biasc, bt, cov, egt], gens=gens,
        out_shape=[jax.ShapeDtypeStruct((T, W), F32)],
        in_specs=[
            pl.BlockSpec((2 * TQ, W), lambda b, i: (b * nstep + i, c0)),
            pl.BlockSpec((S, W), lambda b, i: (b, c0 + 1)),
            pl.BlockSpec((S, 2 * LANES), lambda b, i: (b, (COL_NSA + 2 * W) // (2 * LANES))),
            pl.BlockSpec((2 * TQ, LANES), lambda b, i: (b * nstep + i, cg)),
            const(w1b.shape), const(peb.shape), const(w2b.shape),
            const(biasc.shape), const(bt.shape), const(cov.shape), const(egt.shape),
        ],
        out_specs=[pl.BlockSpec((2 * TQ, W), lambda b, i: (b * nstep + i, 0))],
        scratch=[
            pltpu.VMEM((S // CMP_STRIDE, LANES), BF16),
            pltpu.VMEM((2, V_ROWS, S // CMP_STRIDE), BF16),
            pltpu.VMEM((2, S, LANES), BF16),
            pltpu.VMEM((2, 2, S // TQ, V_ROWS, TQ), BF16),
            pltpu.VMEM((2, 2, S // SLC_BLOCK + SUBLANES, TQ), F32),
            pltpu.VMEM((2, 2, 1, HW), F32),
            pltpu.VMEM((2, 2, 2, V_ROWS, HW // 2), F32),
            pltpu.VMEM((S, LANES), F32),
            pltpu.VMEM((2, 2, 2, TQ, HW), F32),
            pltpu.VMEM((2, 2, 2, TQ, HW), BF16),
            pltpu.VMEM((2, 2, 1, HW), F32),
            pltpu.VMEM((2, 2, 3, TQ, HW), F32),
        ])


def _blockdiag2(w):
    z = jnp.zeros_like(w)
    return jnp.concatenate([jnp.concatenate([w, z], axis=-1), jnp.concatenate([z, w], axis=-1)], axis=-2)


def _nsa_params(pe, w1, w2):
    L, Dh = pe.shape[1], pe.shape[2]
    w1b = _blockdiag2(w1.reshape(2, L, Dh, -1)).astype(BF16)
    w2b = _blockdiag2(w2).astype(BF16)
    peb = jnp.concatenate([pe, pe], axis=-1)
    return w1b, peb, w2b


def _t5_bucket_np(n):
    n = np.maximum(n, 0)
    nf = np.maximum(n, 1).astype(np.float64)
    large = 16 + (np.log(nf / 16.0) / math.log(128 / 16) * 16).astype(np.int64)
    return np.where(n < 16, n, np.minimum(large, 31))


def _bias_table_kernel(rb_ref, idx_ref, add_ref, shift_ref, o_ref):
    nb, nh = rb_ref.shape
    idx = idx_ref[...]
    acc = [jnp.zeros(idx.shape, F32) for _ in range(nh)]
    for b in range(nb):
        hit = idx == b
        acc = [jnp.where(hit, rb_ref[b, h], a) for h, a in enumerate(acc)]
    for h in range(nh):
        o_ref[:, h * NSA_TQ:(h + 1) * NSA_TQ] = (acc[h] - shift_ref[...] * rb_ref[nb - 1, h]) * LOG2E + add_ref[...]


def _nsa_tables(rel_bias, S):
    TQ = NSA_TQ
    nq = S // TQ
    tiles = []

    def tile(dist, valid, shift):
        tiles.append((_t5_bucket_np(dist), np.where(valid, 0.0, -np.inf), np.full(dist.shape, float(shift))))

    ncmp = S // CMP_STRIDE
    end = np.arange(ncmp) * CMP_STRIDE + CMP_LEN - 1
    t = np.arange(TQ)[None, :]
    j = np.arange(TQ)[:, None]
    every = np.ones((TQ, TQ), bool)
    tile(t - j, ~every, True)
    tile(t - j, t >= j, True)
    tile(t - j + TQ, every, True)
    tile(t - j + 2 * TQ, every, True)
    tile(t - j + 2 * TQ, j > t, True)
    assert _t5_bucket_np(np.asarray(TQ + 1)) == _t5_bucket_np(np.asarray(S)) == rel_bias.shape[0] - 1
    cpq = TQ // CMP_STRIDE
    rel_blk = np.arange(cpq * (nq - 1) + ncmp) - cpq * (nq - 1)
    dist_c = np.arange(TQ)[None, :] - (rel_blk[:, None] * CMP_STRIDE + CMP_LEN - 1)
    tile(dist_c, dist_c >= 0, False)
    idx, add, shift = (np.concatenate([x[k] for x in tiles], axis=0) for k in range(3))
    full = lambda a: pl.BlockSpec(a.shape, lambda: (0,) * a.ndim)
    table = pl.pallas_call(
        _bias_table_kernel,
        out_shape=jax.ShapeDtypeStruct((idx.shape[0], NSA_HEADS * TQ), F32),
        in_specs=[pl.BlockSpec(memory_space=pltpu.SMEM), full(idx), full(add), full(shift)],
        out_specs=pl.BlockSpec((idx.shape[0], NSA_HEADS * TQ), lambda: (0, 0)),
        name="bias_tables",
    )(rel_bias.astype(F32), jnp.asarray(idx, jnp.int32), jnp.asarray(add, F32), jnp.asarray(shift, F32))
    bt = table[:5 * TQ].reshape(5, TQ, NSA_HEADS * TQ)
    biasc = table[5 * TQ:]
    nslc = S // SLC_BLOCK
    s_lo = np.arange(nslc) * SLC_BLOCK
    start = np.arange(ncmp) * CMP_STRIDE
    cover = ((start[:, None] <= (s_lo + SLC_BLOCK - 1)[None, :]) & (end[:, None] >= s_lo[None, :]))
    cover &= (np.arange(ncmp) < ncmp - 1)[:, None]
    egt = np.zeros((3, NSA_W, LANES), np.float32)
    for jj in range(3):
        egt[jj, np.arange(NSA_W), (np.arange(NSA_W) // NSA_HD) * 3 + jj] = 1.0
    return biasc, bt, jnp.asarray(cover.T.astype(np.float32), BF16), jnp.asarray(egt, BF16)


def _tiles(T, n_ff):
    tm = 512 if T % 512 == 0 else T
    tf = n_ff // 2 if (n_ff // 2) % LANES == 0 else n_ff
    tn = N_PROJ // 4
    return tm, tf, tn


def _pack_w_in(w):
    hg_end = 4 * HG_W
    nsa_end = hg_end + NSA_W + 6 * 2 * NSA_HD + 3 * 8
    rw_end = nsa_end + 3 * RW_W + 256
    out = jnp.zeros(w.shape[:-1] + (N_PROJ,), BF16)
    for dst, lo, hi in ((COL_MG, rw_end, w.shape[-1]), (COL_HG, 0, hg_end), (COL_NSA, hg_end, nsa_end),
                        (COL_RW, nsa_end, rw_end)):
        out = out.at[..., dst:dst + hi - lo].set(w[..., lo:hi].astype(BF16))
    return out


def kernel(x, p, ffn1_norm, ffn1_wgu, ffn1_wd, mix_norm, w_in, hg_lb, hg_norm, cmp_pe, cmp_w1, cmp_w2, rel_bias, rw_mu, rw_w0, rw_wB, rw_a0, rw_aB, rw_gB, rw_kk, rw_ka, rw_rk, rw_ln_w, rw_ln_b, w_branch, w_out, ffn2_norm, ffn2_wgu, ffn2_wd, ple_norm, ple_gate_w, ple_w, final_norm):
    B, S, D = x.shape
    depth = ffn1_norm.shape[0]
    T = B * S
    assert D == D_MODEL and S // CMP_STRIDE == LANES and S % NSA_TQ == 0
    assert w_in.shape[2] - 3 * D_MODEL == 4 * HG_W + NSA_W + 12 * NSA_HD + 24 + 3 * RW_W + 256
    tm, tf, tn = _tiles(T, ffn1_wd.shape[1])
    consts = _consts()
    tabs = _nsa_tables(rel_bias, S)
    row = lambda v: v.reshape(1, -1)
    wgu1, wd1, wgu2, wd2 = (w.astype(BF16) for w in (ffn1_wgu, ffn1_wd, ffn2_wgu, ffn2_wd))
    w_proj = _pack_w_in(w_in)
    wb, wo, wpg, wpp = (w.astype(BF16) for w in (w_branch, w_out, ple_gate_w, ple_w))
    p_rows = p.reshape(depth, T, -1)
    h = x.reshape(T, D)
    for i in range(depth):
        h = _ffn(h, row(ffn1_norm[i]), wgu1, wd1, i, tm, tf)
        proj = _proj(h, row(mix_norm[i]), w_proj, i, tm, tn)
        nsa_part = _nsa_part(proj, *_nsa_params(cmp_pe[i], cmp_w1[i], cmp_w2[i]), tabs, S)
        hg_part = _hgrn_part(proj, hg_lb, row(hg_norm[i]), consts["j512"], consts["tri16"], consts["tot16"], S, i)
        rw_part = _rwkv_part(
            proj, *_rwkv_params(rw_mu[i], rw_w0[i], rw_wB[i], rw_a0[i], rw_aB[i], rw_gB[i], rw_kk[i], rw_ka[i],
                                rw_rk[i], rw_ln_w[i], rw_ln_b[i]), consts["j512"], consts["tri64"], S)
        (o_ns,) = _run_parts([nsa_part], (B, S // (2 * NSA_TQ)), "nsa")
        o_hg, o_rw = _run_parts([hg_part, rw_part], (B, S // MIX_ROWS), "hgrn2_rwkv7")
        h = _merge(h, proj, o_hg, o_ns, o_rw, wb, wo, i, tm)
        h = _ffn(h, row(ffn2_norm[i]), wgu2, wd2, i, tm, tf)
        h = _ple(h, row(ple_norm[i]), wpg, p_rows, wpp, row(final_norm), i, tm, i == depth - 1)
    return h.reshape(B, S, D)
```

```python
import functools
import itertools
import math

import jax
import jax.numpy as jnp
import numpy as np
from jax import lax
from jax.experimental import pallas as pl
from jax.experimental.pallas import tpu as pltpu

F32 = jnp.float32
BF16 = jnp.bfloat16

RMS_EPS = 1e-6
LOG2E = math.log2(math.e)
LANES = 128
SUBLANES = 8
BF16_ROWS = 16
VMEM_LIMIT = 48 * 1024 * 1024

HG_W = 512
NSA_W = 512
RW_W = 512
D_MODEL = 1024
MIX_ROWS = 512
COL_MG = 0
COL_HG = 3072
COL_NSA = 5120
COL_RW = 6656
N_PROJ = 8704


def _cparams(sem):
    return pltpu.CompilerParams(dimension_semantics=sem, vmem_limit_bytes=VMEM_LIMIT)


def _rms(x, g):
    return x * lax.rsqrt(jnp.mean(x * x, axis=-1, keepdims=True) + RMS_EPS) * g


def _sigmoid(x):
    return 1.0 / (1.0 + jnp.exp(-x))


def _silu(x):
    return x * _sigmoid(x)


def _dot(a, b):
    return jnp.dot(a, b, preferred_element_type=F32)


def _nt(a, b):
    return lax.dot_general(a, b, (((1,), (1,)), ((), ())), preferred_element_type=F32)


def _ffn_kernel(h_ref, g_ref, wgu_ref, wd_ref, o_ref, *, tf):
    h = h_ref[...]
    xn = _rms(h, g_ref[...]).astype(BF16)
    n_ff = wd_ref.shape[0]
    acc = None
    for c in range(0, n_ff, tf):
        gate = _dot(xn, wgu_ref[:, c:c + tf])
        up = _dot(xn, wgu_ref[:, n_ff + c:n_ff + c + tf])
        part = _dot((_silu(gate) * up).astype(BF16), wd_ref[c:c + tf, :])
        acc = part if acc is None else acc + part
    o_ref[...] = h + 0.5 * acc


def _ffn(h, g, wgu, wd, layer, tm, tf):
    T, D = h.shape
    FF = wd.shape[1]
    resident = dict(pipeline_mode=pl.Buffered(1))
    return pl.pallas_call(
        functools.partial(_ffn_kernel, tf=tf),
        out_shape=jax.ShapeDtypeStruct((T, D), F32),
        grid=(T // tm,),
        in_specs=[
            pl.BlockSpec((tm, D), lambda i: (i, 0)),
            pl.BlockSpec((1, D), lambda i: (0, 0)),
            pl.BlockSpec((None, D, 2 * FF), lambda i: (layer, 0, 0), **resident),
            pl.BlockSpec((None, FF, D), lambda i: (layer, 0, 0), **resident),
        ],
        out_specs=pl.BlockSpec((tm, D), lambda i: (i, 0)),
        compiler_params=_cparams(("parallel",)),
        name="ffn",
    )(h, g, wgu, wd)


def _proj_kernel(h_ref, g_ref, w_ref, o_ref):
    o_ref[...] = _dot(_rms(h_ref[...], g_ref[...]).astype(BF16), w_ref[...])


def _proj(h, g, w, layer, tm, tn):
    T, D = h.shape
    N = w.shape[2]
    return pl.pallas_call(
        _proj_kernel,
        out_shape=jax.ShapeDtypeStruct((T, N), F32),
        grid=(N // tn, T // tm),
        in_specs=[
            pl.BlockSpec((tm, D), lambda j, i: (i, 0)),
            pl.BlockSpec((1, D), lambda j, i: (0, 0)),
            pl.BlockSpec((None, D, tn), lambda j, i: (layer, 0, j)),
        ],
        out_specs=pl.BlockSpec((tm, tn), lambda j, i: (i, j)),
        compiler_params=_cparams(("parallel", "parallel")),
        name="in_proj",
    )(h, g, w)


def _merge_kernel(h_ref, m0_ref, m1_ref, m2_ref, a_ref, b_ref, c_ref, wb_ref, wo_ref, o_ref):
    merged = _sigmoid(m0_ref[...]) * _dot(a_ref[...].astype(BF16), wb_ref[0])
    merged += _sigmoid(m1_ref[...]) * _dot(b_ref[...].astype(BF16), wb_ref[1])
    merged += _sigmoid(m2_ref[...]) * _dot(c_ref[...].astype(BF16), wb_ref[2])
    o_ref[...] = h_ref[...] + _dot(merged.astype(BF16), wo_ref[...])


def _merge(h, proj, o_hg, o_ns, o_rw, wb, wo, layer, tm):
    T, D = h.shape
    W = o_hg.shape[1]
    mg0 = COL_MG // D
    return pl.pallas_call(
        _merge_kernel,
        out_shape=jax.ShapeDtypeStruct((T, D), F32),
        grid=(T // tm,),
        in_specs=[
            pl.BlockSpec((tm, D), lambda i: (i, 0)),
            pl.BlockSpec((tm, D), lambda i: (i, mg0)),
            pl.BlockSpec((tm, D), lambda i: (i, mg0 + 1)),
            pl.BlockSpec((tm, D), lambda i: (i, mg0 + 2)),
            pl.BlockSpec((tm, W), lambda i: (i, 0)),
            pl.BlockSpec((tm, W), lambda i: (i, 0)),
            pl.BlockSpec((tm, W), lambda i: (i, 0)),
            pl.BlockSpec((None, 3, W, D), lambda i: (layer, 0, 0, 0)),
            pl.BlockSpec((None, D, D), lambda i: (layer, 0, 0)),
        ],
        out_specs=pl.BlockSpec((tm, D), lambda i: (i, 0)),
        compiler_params=_cparams(("parallel",)),
        name="merge",
    )(h, proj, proj, proj, o_hg, o_ns, o_rw, wb, wo)


def _ple_kernel(h_ref, g_ref, wg_ref, p_ref, wp_ref, fg_ref, o_ref, *, final):
    h = h_ref[...]
    gate = _sigmoid(_dot(_rms(h, g_ref[...]).astype(BF16), wg_ref[...]))
    out = h + gate * _dot(p_ref[...].astype(BF16), wp_ref[...])
    if final:
        out = _rms(out, fg_ref[...])
    o_ref[...] = out


def _ple(h, g, wg, p, wp, fg, layer, tm, final):
    T, D = h.shape
    P = p.shape[2]
    return pl.pallas_call(
        functools.partial(_ple_kernel, final=final),
        out_shape=jax.ShapeDtypeStruct((T, D), F32),
        grid=(T // tm,),
        in_specs=[
            pl.BlockSpec((tm, D), lambda i: (i, 0)),
            pl.BlockSpec((1, D), lambda i: (0, 0)),
            pl.BlockSpec((None, D, D), lambda i: (layer, 0, 0)),
            pl.BlockSpec((None, tm, P), lambda i: (layer, i, 0)),
            pl.BlockSpec((None, P, D), lambda i: (layer, 0, 0)),
            pl.BlockSpec((1, D), lambda i: (0, 0)),
        ],
        out_specs=pl.BlockSpec((tm, D), lambda i: (i, 0)),
        compiler_params=_cparams(("parallel",)),
        name="ple",
    )(h, g, wg, p, wp, fg)


def _softplus(x):
    return jnp.maximum(x, 0.0) + jnp.log(1.0 + jnp.exp(-jnp.abs(x)))


def _split2(x):
    hi = x.astype(BF16)
    lo = (x - hi.astype(F32)).astype(BF16)
    return hi, lo


def _split3(x):
    hi = x.astype(BF16)
    r1 = x - hi.astype(F32)
    mid = r1.astype(BF16)
    lo = (r1 - mid.astype(F32)).astype(BF16)
    return hi, mid, lo


def _segsum(x, j):
    hi, lo = _split2(x)
    return _dot(hi, j) + _dot(lo, j)


def _cumsum_rows(tri, x):
    hi, mid, lo = _split3(x)
    return _dot(tri, hi) + (_dot(tri, mid) + _dot(tri, lo))


def _pair_stack(x, lo_mask):
    return jnp.concatenate([jnp.where(lo_mask, x, 0.0), jnp.where(lo_mask, 0.0, x)], axis=0)


def _consts():
    lane = np.arange(HG_W)
    j512 = (lane[:, None] // HG_HD == lane[None, :] // HG_HD).astype(np.float32)
    step = np.arange(RW_CH)
    tri64 = (step[:, None] >= step[None, :]).astype(np.float32)
    r = np.arange(HG_ROWS)
    same = r[:, None] // HG_SUB == r[None, :] // HG_SUB
    tri16 = (same & (r[:, None] >= r[None, :])).astype(np.float32)
    tot16 = same.astype(np.float32)
    return {"j512": jnp.asarray(j512, BF16), "tri64": jnp.asarray(tri64, BF16),
            "tri16": jnp.asarray(tri16, BF16), "tot16": jnp.asarray(tot16, BF16)}


RW_HD = 64
RW_CH = 64
RW_GN_EPS = 64e-5


def _rwkv_steps(r_ref, k_ref, v_ref, l_ref, mu_ref, vec_ref, wb_ref, ab_ref, gb_ref, j_ref,
                tri_ref, o_ref, carry_ref, st_ref, *, nch):
    C = RW_CH
    TC = nch * C
    W = r_ref.shape[1]
    npair = W // LANES

    row = lax.broadcasted_iota(jnp.int32, (TC, W), 0)

    def shift(x_ref, idx):
        x = x_ref[...]
        prev = jnp.where(row == 0, carry_ref[idx:idx + 1, :], pltpu.roll(x, 1, axis=0))
        carry_ref[idx:idx + 1, :] = x[TC - 1:TC, :]
        return x + (prev - x) * mu_ref[idx:idx + 1, :]

    xr = shift(r_ref, 0)
    xk = shift(k_ref, 1)
    xv = shift(v_ref, 2)
    xl = shift(l_ref, 3)
    w0, a0, k_k, k_a = (vec_ref[i:i + 1, :] for i in range(4))
    ln_w, ln_b, r_k = (vec_ref[i:i + 1, :] for i in range(4, 7))
    jmat = j_ref[...]

    wlal = xl[:, 0:LANES]
    w_pre = w0 + _dot(jnp.tanh(wlal).astype(BF16), wb_ref[...])
    a_pre = a0 + _dot(wlal.astype(BF16), ab_ref[...])
    gate = _dot(_sigmoid(xl[:, LANES:2 * LANES]).astype(BF16), gb_ref[...])
    logw = -jnp.exp(-_softplus(-w_pre) - 0.5)
    a = _sigmoid(a_pre)
    kkr = xk * k_k
    kk = kkr / jnp.maximum(jnp.sqrt(_segsum(kkr * kkr, jmat)), 1e-12)
    k2 = xk * (1.0 + (a - 1.0) * k_a)
    ka = kk * a
    yield

    lane = lax.broadcasted_iota(jnp.int32, (C, LANES), 1)
    trow = lax.broadcasted_iota(jnp.int32, (C, LANES), 0)
    lo_mask = lane < RW_HD
    scol = lane & (RW_HD - 1)
    strict = trow > scol
    incl = trow >= scol
    eye2 = (trow == scol).astype(F32)
    r128 = lax.broadcasted_iota(jnp.int32, (LANES, LANES), 0)
    c128 = lax.broadcasted_iota(jnp.int32, (LANES, LANES), 1)
    bd_mask = (r128 // RW_HD) == (c128 // RW_HD)
    diag_mask = r128 == c128
    tri = tri_ref[...]

    def bf(x):
        return x.astype(BF16)

    def stack(x):
        return _pair_stack(x, lo_mask)

    ops = []
    for c in range(nch):
        rs = slice(c * C, (c + 1) * C)
        lw = logw[rs] * LOG2E
        b = _cumsum_rows(tri, lw)
        bend = b[C - 1:C, :]
        enb = jnp.exp2(-b)
        egc = jnp.exp2(bend - b)
        g_end = jnp.exp2(bend)
        full = (xr[rs] * jnp.exp2(b), k2[rs] * enb, ka[rs] * enb, kk[rs] * jnp.exp2(b - lw), k2[rs] * egc,
                ka[rs] * egc, xv[rs], jnp.broadcast_to(g_end, (C, W)))
        for p in range(npair):
            ops.append(tuple(t[:, p * LANES:(p + 1) * LANES] for t in full))
        yield
    n = len(ops)
    gms = [_nt(bf(jnp.concatenate([bt, rt], axis=0)), bf(jnp.concatenate([stack(at), stack(kt)], axis=0)))
           for rt, kt, at, bt, _, _, _, _ in ops]
    a_ba = [jnp.where(strict, gm[0:C, 0:LANES], 0.0) for gm in gms]
    a_bk = [jnp.where(strict, gm[0:C, LANES:], 0.0) for gm in gms]
    a_ra = [jnp.where(incl, gm[C:, 0:LANES], 0.0) for gm in gms]
    a_rk = [jnp.where(incl, gm[C:, LANES:], 0.0) for gm in gms]
    yield
    pw = [-a for a in a_ba]
    ti = [eye2 + x for x in pw]
    pw = [_dot(bf(x), bf(stack(x))) for x in pw]
    yield
    nsq = int(math.log2(C)) - 1
    for k in range(1, nsq):
        both = [_dot(bf(jnp.concatenate([x, t], axis=0)), bf(stack(x))) for x, t in zip(pw, ti)]
        pw = [m[0:C] for m in both]
        ti = [t + m[C:] for t, m in zip(ti, both)]
        yield
    ti = [t + _dot(bf(t), bf(stack(x))) for t, x in zip(ti, pw)]
    tib = [bf(t) for t in ti]
    yield
    wm = [_dot(tib[j], bf(stack(ops[j][3]))) for j in range(n)]
    av = [_dot(bf(a_bk[j]), bf(stack(ops[j][6]))) for j in range(n)]
    yield
    u0 = [_dot(tib[j], bf(stack(av[j]))) for j in range(n)]
    yield
    y0 = [_dot(bf(jnp.concatenate([a_rk[j], a_ra[j]], axis=1)),
               bf(jnp.concatenate([stack(ops[j][6]), -stack(u0[j])], axis=0))) for j in range(n)]
    rw = [ops[j][0] - _dot(bf(a_ra[j]), bf(stack(wm[j]))) for j in range(n)]
    yield
    m2 = [jnp.where(diag_mask, jnp.concatenate([ops[j][7], ops[j][7]], axis=0), 0.0)
          - jnp.where(bd_mask, _dot(bf(ops[j][5].T), bf(wm[j])), 0.0) for j in range(n)]
    yield
    n2 = [jnp.where(bd_mask, _dot(bf(jnp.concatenate([ops[j][4], ops[j][5]], axis=0).T),
                                  bf(jnp.concatenate([ops[j][6], -u0[j]], axis=0))), 0.0) for j in range(n)]
    yield
    s2 = [st_ref[p] for p in range(npair)]
    ys = []
    for c in range(nch):
        js = [c * npair + p for p in range(npair)]
        ys.append(jnp.concatenate([y0[j] + _dot(bf(rw[j]), bf(s2[p])) for p, j in enumerate(js)], axis=1))
        s2 = [_dot(bf(m2[j]), bf(s2[p])) + n2[j] for p, j in enumerate(js)]
        yield
    for p in range(npair):
        st_ref[p] = s2[p]
    y = jnp.concatenate(ys, axis=0) if nch > 1 else ys[0]

    inv_n = 1.0 / RW_HD
    mean = _segsum(y, jmat) * inv_n
    yc = y - mean
    var = _segsum(yc * yc, jmat) * inv_n
    yn = yc * lax.rsqrt(var + RW_GN_EPS) * ln_w + ln_b
    bonus = _segsum(xr * k2 * r_k, jmat) * xv
    o_ref[...] = (yn + bonus) * gate


def _rwkv_params(mu, w0, wB, a0, aB, gB, k_k, k_a, r_k, ln_w, ln_b):
    W = RW_W
    mu4 = jnp.stack([mu[0:W], mu[W:2 * W], mu[2 * W:3 * W], jnp.pad(mu[3 * W:], (0, W - (mu.shape[0] - 3 * W)))])
    vec = jnp.stack([w0, a0, k_k, k_a, ln_w, ln_b, r_k.reshape(-1), jnp.zeros_like(w0)])
    wb = jnp.pad(wB, ((0, LANES - wB.shape[0]), (0, 0))).astype(BF16)
    ab = jnp.pad(aB, ((LANES - aB.shape[0], 0), (0, 0))).astype(BF16)
    return mu4, vec, wb, ab, gB.astype(BF16)


HG_HD = 64
HG_SUB = 16
HG_ROWS = 128


def _hgrn_steps(q_ref, f_ref, i_ref, g_ref, lbp_ref, ng_ref, j_ref, tri_ref, tot_ref, o_ref, st_ref,
                *, layer, rows):
    W = q_ref.shape[1]
    TC = rows.stop - rows.start
    npair = W // LANES
    nsub = TC // HG_SUB

    z = f_ref[rows, :]
    sp = _softplus(-z)
    log_f = -sp
    k = _sigmoid(-z)
    if layer > 0:
        lbp = lbp_ref[...]
        e = jnp.exp(lbp - jnp.max(lbp, axis=0, keepdims=True))
        sm = e / jnp.sum(e, axis=0, keepdims=True)
        lb = sm[1:2, :]
        for j in range(2, layer + 1):
            lb = lb + sm[j:j + 1, :]
        lb = jnp.maximum(lb, 0.0)
        t2 = jnp.log(lb) - (sp + z)
        log_f = jnp.maximum(log_f, t2) + jnp.log(1.0 + jnp.exp(-jnp.abs(log_f - t2)))
        k = (1.0 - lb) * k
    q = _silu(q_ref[rows, :])
    v = i_ref[rows, :]
    log_f = log_f * LOG2E
    b = _cumsum_rows(tri_ref[...], log_f)
    bend = _cumsum_rows(tot_ref[...], log_f)
    qe = q * jnp.exp2(b)
    kg = k * jnp.exp2(bend - b)
    jmat = j_ref[...]
    j128 = jmat[0:LANES, 0:LANES]

    rowb = lax.broadcasted_iota(jnp.int32, (TC, LANES), 0)
    trow = lax.broadcasted_iota(jnp.int32, (HG_SUB // 2, W), 0)
    r128 = lax.broadcasted_iota(jnp.int32, (LANES, LANES), 0)
    c128 = lax.broadcasted_iota(jnp.int32, (LANES, LANES), 1)
    bd_mask = (r128 // HG_HD) == (c128 // HG_HD)

    def bf(x):
        return x.astype(BF16)

    vts = [bf(v[:, p * LANES:(p + 1) * LANES].T) for p in range(npair)]
    yield
    outs = []
    H8 = HG_SUB // 2
    for i in range(nsub):
        rs = slice(i * HG_SUB, (i + 1) * HG_SUB)
        r0 = i * HG_SUB
        (b_a, q_a, k_a, v_a), (b_b, q_b, k_b, v_b) = (
            tuple(t[r0 + h * H8:r0 + (h + 1) * H8] for t in (b, q, k, v)) for h in range(2))
        xs = []
        for s in range(H8):
            bs, ks = b_a[s:s + 1, :], k_a[s:s + 1, :]
            xs.append(jnp.where(trow >= s, q_a * (ks * jnp.exp2(b_a - bs)), 0.0))
            xs.append(q_b * (ks * jnp.exp2(b_b - bs)))
        for s in range(H8):
            bs, ks = b_b[s:s + 1, :], k_b[s:s + 1, :]
            xs.append(jnp.where(trow >= s, q_b * (ks * jnp.exp2(b_b - bs)), 0.0))
        x = bf(jnp.concatenate(xs, axis=0))
        g_end = jnp.exp2(bend[i * HG_SUB:i * HG_SUB + 1, :])
        op = []
        for p in range(npair):
            ls = slice(p * LANES, (p + 1) * LANES)
            pm = _dot(x[:, ls], j128)
            od_a = pm[0:H8] * v_a[0:1, ls]
            od_b = pm[H8:HG_SUB] * v_a[0:1, ls]
            for s in range(1, H8):
                od_a = od_a + pm[s * HG_SUB:s * HG_SUB + H8] * v_a[s:s + 1, ls]
                od_b = od_b + pm[s * HG_SUB + H8:(s + 1) * HG_SUB] * v_a[s:s + 1, ls]
            for s in range(H8):
                od_b = od_b + pm[(HG_SUB + s) * H8:(HG_SUB + s + 1) * H8] * v_b[s:s + 1, ls]
            od = jnp.concatenate([od_a, od_b], axis=0)
            st = st_ref[p]
            oi = lax.dot_general(bf(qe[rs, ls]), bf(st), (((1,), (1,)), ((), ())), preferred_element_type=F32)
            kgm = jnp.where((rowb >= i * HG_SUB) & (rowb < (i + 1) * HG_SUB), kg[:, ls], 0.0)
            st_ref[p] = st * g_end[:, ls] + jnp.where(bd_mask, _dot(vts[p], bf(kgm)), 0.0)
            op.append(od + oi)
        outs.append(jnp.concatenate(op, axis=1))
        yield
    o = jnp.concatenate(outs, axis=0)
    ms = _segsum(o * o, jmat) * (1.0 / HG_HD)
    o_ref[rows, :] = o * lax.rsqrt(ms + RMS_EPS) * ng_ref[...] * _silu(g_ref[rows, :])


def _interleave(gens):
    live = list(gens)
    while live:
        for g in list(live):
            try:
                next(g)
            except StopIteration:
                live.remove(g)


def _run_parts(parts, grid, name):
    n_in = [len(p["args"]) for p in parts]
    n_out = [len(p["out_shape"]) for p in parts]
    n_scr = [len(p["scratch"]) for p in parts]

    def take(refs, counts):
        out, pos = [], 0
        for c in counts:
            out.append(refs[pos:pos + c])
            pos += c
        return out

    def kern(*refs):
        ins = take(refs[:sum(n_in)], n_in)
        outs = take(refs[sum(n_in):sum(n_in) + sum(n_out)], n_out)
        scr = take(refs[sum(n_in) + sum(n_out):], n_scr)
        gens = []
        for p, a, b, c in zip(parts, ins, outs, scr):
            gens += p["gens"](a, b, c)
        _interleave(gens)

    cat = lambda key: [x for p in parts for x in p[key]]
    return pl.pallas_call(
        kern, out_shape=cat("out_shape"), grid=grid, in_specs=cat("in_specs"), out_specs=cat("out_specs"),
        scratch_shapes=cat("scratch"), compiler_params=_cparams(("parallel", "arbitrary")), name=name,
    )(*cat("args"))


def _seq_part_specs(proj, col0, consts, S):
    T = proj.shape[0]
    W = HG_W
    nblk = S // MIX_ROWS
    rows = lambda col: pl.BlockSpec((MIX_ROWS, W), lambda b, i: (b * nblk + i, col))
    const = lambda a: pl.BlockSpec(a.shape, lambda b, i: (0, 0))
    c0 = col0 // W
    return dict(args=[proj] * 4 + list(consts), in_specs=[rows(c0 + j) for j in range(4)] + [const(a) for a in consts],
                out_shape=[jax.ShapeDtypeStruct((T, W), F32)], out_specs=[rows(0)])


def _hgrn_part(proj, lbp, ng, jmat, tri16, tot16, S, layer):
    def gens(ins, outs, scr):
        (st_ref,) = scr

        @pl.when(pl.program_id(1) == 0)
        def _():
            st_ref[...] = jnp.zeros_like(st_ref)

        th = ins[7].shape[0]
        return [itertools.chain(*[_hgrn_steps(*ins, *outs, st_ref, layer=layer, rows=slice(r, r + th))
                                  for r in range(0, MIX_ROWS, th)])]

    return dict(_seq_part_specs(proj, COL_HG, (lbp, ng, jmat, tri16, tot16), S), gens=gens,
                scratch=[pltpu.VMEM((HG_W // LANES, LANES, LANES), F32)])


def _rwkv_part(proj, mu, vec, wb, ab, gb, jmat, tri64, S):
    def gens(ins, outs, scr):
        carry_ref, st_ref = scr

        @pl.when(pl.program_id(1) == 0)
        def _():
            carry_ref[...] = jnp.zeros_like(carry_ref)
            st_ref[...] = jnp.zeros_like(st_ref)

        return [_rwkv_steps(*ins, *outs, carry_ref, st_ref, nch=MIX_ROWS // RW_CH)]

    return dict(_seq_part_specs(proj, COL_RW, (mu, vec, wb, ab, gb, jmat, tri64), S), gens=gens,
                scratch=[pltpu.VMEM((SUBLANES, RW_W), F32), pltpu.VMEM((RW_W // LANES, LANES, LANES), F32)])


NSA_HD = 64
NSA_HEADS = 8
NSA_TQ = 128
CMP_STRIDE = 16
CMP_LEN = 32
SLC_BLOCK = 64
SLC_TOPN = 8
NEG_INF = float("-inf")
V_ROWS = NSA_HD + BF16_ROWS


def _nsa_steps(q_ref, kv1_ref, kv2_ref, gt_ref, w1_ref, pe_ref, w2_ref, biasc_ref, bt_ref, cov_ref, egt_ref,
               o_ref, kc_ref, vct_ref, k_ref, vt_ref, am_ref, m_ref, acc_ref, cx_ref, s_ref, p_ref, al_ref, sn_ref,
               *, bs):
    TQ = NSA_TQ
    S = kv1_ref.shape[0]
    nkc = S // TQ
    ncmp = S // CMP_STRIDE
    nslc = S // SLC_BLOCK
    GW = (NSA_HEADS // 2) * TQ
    i = 2 * pl.program_id(1) + bs
    nq = 2 * pl.num_programs(1)
    q_ref, gt_ref, o_ref = (r.at[bs * TQ:(bs + 1) * TQ] for r in (q_ref, gt_ref, o_ref))
    am_ref, m_ref, acc_ref, s_ref, p_ref, al_ref, sn_ref = (
        r.at[bs] for r in (am_ref, m_ref, acc_ref, s_ref, p_ref, al_ref, sn_ref))
    first_block_only = pl.when(i == 0) if bs == 0 else (lambda f: None)

    def with_ones(vt_g):
        return jnp.concatenate([vt_g, jnp.ones((V_ROWS - NSA_HD, vt_g.shape[1]), F32)], axis=0).astype(BF16)

    @first_block_only
    def _():
        k_ref[0] = kv1_ref[:, 2 * LANES:3 * LANES].astype(BF16)
        k_ref[1] = kv2_ref[:, 0:LANES].astype(BF16)
        for src, (ref, c0) in enumerate(((kv1_ref, 3), (kv2_ref, 1))):
            for c in range(nkc):
                vt = ref[c * TQ:(c + 1) * TQ, c0 * LANES:(c0 + 1) * LANES].T
                for g in range(2):
                    vt_ref[src, g, c] = with_ones(vt[g * NSA_HD:(g + 1) * NSA_HD])
        for t in range(2):
            cx_ref[...] = kv1_ref[:, t * LANES:(t + 1) * LANES]
            acc_a = jnp.zeros((ncmp, 2 * LANES), F32)
            acc_b = jnp.zeros((ncmp, 2 * LANES), F32)
            for l in range(CMP_STRIDE):
                xl = cx_ref[pl.ds(l, ncmp, stride=CMP_STRIDE), :]
                acc_a += _dot((xl + pe_ref[t, l:l + 1, :]).astype(BF16), w1_ref[t, l])
                acc_b += _dot((xl + pe_ref[t, CMP_STRIDE + l:CMP_STRIDE + l + 1, :]).astype(BF16),
                              w1_ref[t, CMP_STRIDE + l])
            hid = _silu(acc_a + pltpu.roll(acc_b, ncmp - 1, axis=0))
            cmp = _dot(hid.astype(BF16), w2_ref[t])
            if t == 0:
                kc_ref[...] = cmp.astype(BF16)
            else:
                ct = cmp.T
                for g in range(2):
                    vct_ref[g] = with_ones(ct[g * NSA_HD:(g + 1) * NSA_HD])

    qs = q_ref[...] * (LOG2E * NSA_HD ** -0.5)
    zero = jnp.zeros((NSA_HD, TQ), F32)
    cols = []
    for pp in range(NSA_HEADS // 2):
        qt = qs[:, pp * LANES:(pp + 1) * LANES].T
        for e in range(2):
            qh = qt[e * NSA_HD:(e + 1) * NSA_HD]
            cols.append(jnp.concatenate([qh, zero] if pp < 2 else [zero, qh], axis=0))
    wq = jnp.concatenate(cols, axis=1).astype(BF16)

    def group_rows(o_t):
        return jnp.concatenate([o_t[:, hh * TQ:(hh + 1) * TQ] for hh in range(NSA_HEADS // 2)], axis=0)

    def raw_scores(br, kc):
        return _dot(k_ref[br, pl.ds(pl.multiple_of(kc * TQ, TQ), TQ), :], wq)

    cpq = TQ // CMP_STRIDE
    sc = _dot(kc_ref[...], wq) + biasc_ref[pl.ds(pl.multiple_of(cpq * (nq - 1 - i), cpq), ncmp), :]
    near = [i - 2, i - 1, i]
    near_kc = [jnp.maximum(kc, 0) for kc in near]
    for t, kc in enumerate(near_kc):
        for br in range(2):
            sn_ref[br, t] = raw_scores(br, kc)
    for u in range(2):
        s_ref[0, u] = raw_scores(0, u)
    yield

    mx = jnp.max(sc, axis=0, keepdims=True)
    ex = jnp.exp2(sc - jnp.where(mx == NEG_INF, 0.0, mx))
    p = ex * (1.0 / jnp.maximum(jnp.sum(ex, axis=0, keepdims=True), 1e-30))
    o_cmp = []
    for g in range(2):
        pg = p[:, g * GW:(g + 1) * GW]
        o_cmp.append(group_rows(_dot(vct_ref[g, 0:NSA_HD, :], pg.astype(BF16))))
        psum = pg[:, 0:TQ]
        for hh in range(1, NSA_HEADS // 2):
            psum = psum + pg[:, hh * TQ:(hh + 1) * TQ]
        hi, lw = _split2(psum)
        imp_t = _dot(cov_ref[...], hi) + _dot(cov_ref[...], lw)
        blk = lax.broadcasted_iota(jnp.int32, (nslc, TQ), 0)
        cur = (i * TQ + lax.broadcasted_iota(jnp.int32, (nslc, TQ), 1)) // SLC_BLOCK
        forced = (blk == 0) | (blk == cur) | (blk == cur - 1)
        score = jnp.where(forced, jnp.inf, jnp.where(blk <= cur, imp_t, NEG_INF))
        cnt = jnp.zeros((nslc, TQ), F32)
        for mp in range(nslc):
            sm = score[mp:mp + 1, :]
            ahead = (sm > score) | ((sm == score) & (blk > mp))
            cnt = cnt + jnp.where(ahead, 1.0, 0.0)
        am_ref[g, 0:nslc] = jnp.where(cnt < SLC_TOPN, 0.0, NEG_INF)
        am_ref[g, nslc:] = jnp.full((am_ref.shape[1] - nslc, TQ), NEG_INF, F32)

    yield

    def reset(br):
        m_ref[br] = jnp.full(m_ref.shape[1:], NEG_INF, F32)
        acc_ref[br] = jnp.zeros(acc_ref.shape[1:], F32)

    def block_mask(r0):
        mk = []
        for g in range(2):
            halves = [jnp.broadcast_to(am_ref[g, pl.ds(r0 + u, 1), :], (SLC_BLOCK, TQ)) for u in range(2)]
            mk.append(jnp.concatenate(halves, axis=0))
        return jnp.concatenate([mk[0]] * (NSA_HEADS // 2) + [mk[1]] * (NSA_HEADS // 2), axis=1)

    def softmax_update(br, ss):
        m_prev = m_ref[br]
        m_new = m_prev
        for s in ss:
            m_new = jnp.maximum(m_new, jnp.max(s, axis=0, keepdims=True))
        m_safe = jnp.where(m_new == NEG_INF, 0.0, m_new)
        m_ref[br] = m_new
        return jnp.exp2(m_prev - m_safe), [jnp.exp2(s - m_safe).astype(BF16) for s in ss]

    def value_update(br, alpha, kcs, pts):
        for g in range(2):
            upd = alpha[:, g * GW:(g + 1) * GW] * acc_ref[br, g]
            for kc, pt in zip(kcs, pts):
                upd = upd + _dot(vt_ref[br, g, kc], pt[:, g * GW:(g + 1) * GW])
            acc_ref[br, g] = upd

    def result(br):
        outs = []
        for g in range(2):
            acc = acc_ref[br, g]
            inv = 1.0 / jnp.maximum(acc[NSA_HD:NSA_HD + 1, :], 1e-30)
            outs.append(group_rows(acc[0:NSA_HD, :] * inv))
        return outs

    reset(1)
    win_tiles = [jnp.where(kc < 0, 0, tile) for kc, tile in zip(near, (4, 2, 1))]
    alpha, pts = softmax_update(1, [sn_ref[1, t] + bt_ref[win_tiles[t]] for t in range(3)])
    value_update(1, alpha, near_kc, pts)
    yield

    reset(0)
    nfar = jnp.maximum(i - 1, 0)
    nfp = nfar // 2
    tail_tiles = [jnp.where((near[0] >= 0) & (nfar - 2 * nfp == 1), 3, 0), jnp.where(near[1] >= 0, 2, 0), 1]
    alpha, pts = softmax_update(0, [sn_ref[0, t] + bt_ref[tail_tiles[t]] + block_mask(2 * near_kc[t])
                                    for t in range(3)])
    value_update(0, alpha, near_kc, pts)
    yield

    def pair_scores(pr, slot):
        for u in range(2):
            s_ref[slot, u] = raw_scores(0, jnp.clip(2 * pr + u, 0, nkc - 1))

    def pair_values(pr, slot):
        value_update(0, al_ref[slot], [jnp.clip(2 * pr + u, 0, nkc - 1) for u in range(2)],
                     [p_ref[slot, u] for u in range(2)])

    def pair_step(pr, slot):
        pair_scores(pr + 1, 1 - slot)
        pair_values(pr - 1, 1 - slot)
        rows = [jnp.where(pr < nfp, 2 * (2 * pr + u), nslc) for u in range(2)]
        alpha, pts = softmax_update(0, [s_ref[slot, u] + block_mask(rows[u]) for u in range(2)])
        al_ref[slot] = alpha
        for u in range(2):
            p_ref[slot, u] = pts[u]

    p_ref[1] = jnp.zeros(p_ref.shape[1:], BF16)
    al_ref[1] = jnp.ones(al_ref.shape[1:], F32)

    def two_pairs(q2, carry):
        pair_step(2 * q2, 0)
        pair_step(2 * q2 + 1, 1)
        return carry

    ntrip = (nfp + 1) // 2
    lax.fori_loop(0, ntrip, two_pairs, 0)
    yield
    pair_values(2 * ntrip - 1, 1)

    o_slc = result(0)
    o_win = result(1)

    sig_hi, sig_lo = _split2(_sigmoid(gt_ref[...]).T)
    out_t = None
    for j, parts in enumerate((o_cmp, o_slc, o_win)):
        gate_t = _dot(egt_ref[j], sig_hi) + _dot(egt_ref[j], sig_lo)
        term = gate_t * jnp.concatenate(parts, axis=0)
        out_t = term if out_t is None else out_t + term
    o_ref[...] = jnp.concatenate([out_t[c * LANES:(c + 1) * LANES].T for c in range(NSA_W // LANES)], axis=1)


def _nsa_part(proj, w1b, peb, w2b, tabs, S):
    T = proj.shape[0]
    W = NSA_W
    TQ = NSA_TQ
    nq = S // TQ
    HW = NSA_HEADS * TQ
    c0 = COL_NSA // W
    cg = (COL_NSA + 2 * W + 2 * LANES) // LANES
    biasc, bt, cov, egt = tabs

    def const(shape):
        return pl.BlockSpec(shape, lambda b, i, _n=len(shape): (0,) * _n)

    nstep = nq // 2

    def gens(ins, outs, scr):
        return [_nsa_steps(*ins, *outs, *scr, bs=bs) for bs in range(2)]

    return dict(
        args=[proj, proj, proj, proj, w1b, peb, w2b, biasc, bt, cov, egt], gens=gens,
        out_shape=[jax.ShapeDtypeStruct((T, W), F32)],
        in_specs=[
            pl.BlockSpec((2 * TQ, W), lambda b, i: (b * nstep + i, c0)),
            pl.BlockSpec((S, W), lambda b, i: (b, c0 + 1)),
            pl.BlockSpec((S, 2 * LANES), lambda b, i: (b, (COL_NSA + 2 * W) // (2 * LANES))),
            pl.BlockSpec((2 * TQ, LANES), lambda b, i: (b * nstep + i, cg)),
            const(w1b.shape), const(peb.shape), const(w2b.shape),
            const(biasc.shape), const(bt.shape), const(cov.shape), const(egt.shape),
        ],
        out_specs=[pl.BlockSpec((2 * TQ, W), lambda b, i: (b * nstep + i, 0))],
        scratch=[
            pltpu.VMEM((S // CMP_STRIDE, LANES), BF16),
            pltpu.VMEM((2, V_ROWS, S // CMP_STRIDE), BF16),
            pltpu.VMEM((2, S, LANES), BF16),
            pltpu.VMEM((2, 2, S // TQ, V_ROWS, TQ), BF16),
            pltpu.VMEM((2, 2, S // SLC_BLOCK + SUBLANES, TQ), F32),
            pltpu.VMEM((2, 2, 1, HW), F32),
            pltpu.VMEM((2, 2, 2, V_ROWS, HW // 2), F32),
            pltpu.VMEM((S, LANES), F32),
            pltpu.VMEM((2, 2, 2, TQ, HW), F32),
            pltpu.VMEM((2, 2, 2, TQ, HW), BF16),
            pltpu.VMEM((2, 2, 1, HW), F32),
            pltpu.VMEM((2, 2, 3, TQ, HW), F32),
        ])


def _blockdiag2(w):
    z = jnp.zeros_like(w)
    return jnp.concatenate([jnp.concatenate([w, z], axis=-1), jnp.concatenate([z, w], axis=-1)], axis=-2)


def _nsa_params(pe, w1, w2):
    L, Dh = pe.shape[1], pe.shape[2]
    w1b = _blockdiag2(w1.reshape(2, L, Dh, -1)).astype(BF16)
    w2b = _blockdiag2(w2).astype(BF16)
    peb = jnp.concatenate([pe, pe], axis=-1)
    return w1b, peb, w2b


def _t5_bucket_np(n):
    n = np.maximum(n, 0)
    nf = np.maximum(n, 1).astype(np.float64)
    large = 16 + (np.log(nf / 16.0) / math.log(128 / 16) * 16).astype(np.int64)
    return np.where(n < 16, n, np.minimum(large, 31))


def _bias_table_kernel(rb_ref, idx_ref, add_ref, shift_ref, o_ref):
    nb, nh = rb_ref.shape
    idx = idx_ref[...]
    acc = [jnp.zeros(idx.shape, F32) for _ in range(nh)]
    for b in range(nb):
        hit = idx == b
        acc = [jnp.where(hit, rb_ref[b, h], a) for h, a in enumerate(acc)]
    for h in range(nh):
        o_ref[:, h * NSA_TQ:(h + 1) * NSA_TQ] = (acc[h] - shift_ref[...] * rb_ref[nb - 1, h]) * LOG2E + add_ref[...]


def _nsa_tables(rel_bias, S):
    TQ = NSA_TQ
    nq = S // TQ
    tiles = []

    def tile(dist, valid, shift):
        tiles.append((_t5_bucket_np(dist), np.where(valid, 0.0, -np.inf), np.full(dist.shape, float(shift))))

    ncmp = S // CMP_STRIDE
    end = np.arange(ncmp) * CMP_STRIDE + CMP_LEN - 1
    t = np.arange(TQ)[None, :]
    j = np.arange(TQ)[:, None]
    every = np.ones((TQ, TQ), bool)
    tile(t - j, ~every, True)
    tile(t - j, t >= j, True)
    tile(t - j + TQ, every, True)
    tile(t - j + 2 * TQ, every, True)
    tile(t - j + 2 * TQ, j > t, True)
    assert _t5_bucket_np(np.asarray(TQ + 1)) == _t5_bucket_np(np.asarray(S)) == rel_bias.shape[0] - 1
    cpq = TQ // CMP_STRIDE
    rel_blk = np.arange(cpq * (nq - 1) + ncmp) - cpq * (nq - 1)
    dist_c = np.arange(TQ)[None, :] - (rel_blk[:, None] * CMP_STRIDE + CMP_LEN - 1)
    tile(dist_c, dist_c >= 0, False)
    idx, add, shift = (np.concatenate([x[k] for x in tiles], axis=0) for k in range(3))
    full = lambda a: pl.BlockSpec(a.shape, lambda: (0,) * a.ndim)
    table = pl.pallas_call(
        _bias_table_kernel,
        out_shape=jax.ShapeDtypeStruct((idx.shape[0], NSA_HEADS * TQ), F32),
        in_specs=[pl.BlockSpec(memory_space=pltpu.SMEM), full(idx), full(add), full(shift)],
        out_specs=pl.BlockSpec((idx.shape[0], NSA_HEADS * TQ), lambda: (0, 0)),
        name="bias_tables",
    )(rel_bias.astype(F32), jnp.asarray(idx, jnp.int32), jnp.asarray(add, F32), jnp.asarray(shift, F32))
    bt = table[:5 * TQ].reshape(5, TQ, NSA_HEADS * TQ)
    biasc = table[5 * TQ:]
    nslc = S // SLC_BLOCK
    s_lo = np.arange(nslc) * SLC_BLOCK
    start = np.arange(ncmp) * CMP_STRIDE
    cover = ((start[:, None] <= (s_lo + SLC_BLOCK - 1)[None, :]) & (end[:, None] >= s_lo[None, :]))
    cover &= (np.arange(ncmp) < ncmp - 1)[:, None]
    egt = np.zeros((3, NSA_W, LANES), np.float32)
    for jj in range(3):
        egt[jj, np.arange(NSA_W), (np.arange(NSA_W) // NSA_HD) * 3 + jj] = 1.0
    return biasc, bt, jnp.asarray(cover.T.astype(np.float32), BF16), jnp.asarray(egt, BF16)


def _tiles(T, n_ff):
    tm = 512 if T % 512 == 0 else T
    tm_proj = 2 * tm if T % (2 * tm) == 0 else tm
    tf = n_ff // 2 if (n_ff // 2) % LANES == 0 else n_ff
    tn = N_PROJ // 4
    return tm, tm_proj, tf, tn


def _pack_w_in(w):
    hg_end = 4 * HG_W
    nsa_end = hg_end + NSA_W + 6 * 2 * NSA_HD + 3 * 8
    rw_end = nsa_end + 3 * RW_W + 256
    out = jnp.zeros(w.shape[:-1] + (N_PROJ,), BF16)
    for dst, lo, hi in ((COL_MG, rw_end, w.shape[-1]), (COL_HG, 0, hg_end), (COL_NSA, hg_end, nsa_end),
                        (COL_RW, nsa_end, rw_end)):
        out = out.at[..., dst:dst + hi - lo].set(w[..., lo:hi].astype(BF16))
    return out


def kernel(x, p, ffn1_norm, ffn1_wgu, ffn1_wd, mix_norm, w_in, hg_lb, hg_norm, cmp_pe, cmp_w1, cmp_w2, rel_bias, rw_mu, rw_w0, rw_wB, rw_a0, rw_aB, rw_gB, rw_kk, rw_ka, rw_rk, rw_ln_w, rw_ln_b, w_branch, w_out, ffn2_norm, ffn2_wgu, ffn2_wd, ple_norm, ple_gate_w, ple_w, final_norm):
    B, S, D = x.shape
    depth = ffn1_norm.shape[0]
    T = B * S
    assert D == D_MODEL and S // CMP_STRIDE == LANES and S % NSA_TQ == 0
    assert w_in.shape[2] - 3 * D_MODEL == 4 * HG_W + NSA_W + 12 * NSA_HD + 24 + 3 * RW_W + 256
    tm, tm_proj, tf, tn = _tiles(T, ffn1_wd.shape[1])
    consts = _consts()
    tabs = _nsa_tables(rel_bias, S)
    row = lambda v: v.reshape(1, -1)
    wgu1, wd1, wgu2, wd2 = (w.astype(BF16) for w in (ffn1_wgu, ffn1_wd, ffn2_wgu, ffn2_wd))
    w_proj = _pack_w_in(w_in)
    wb, wo, wpg, wpp = (w.astype(BF16) for w in (w_branch, w_out, ple_gate_w, ple_w))
    p_rows = p.reshape(depth, T, -1)
    h = x.reshape(T, D)
    for i in range(depth):
        h = _ffn(h, row(ffn1_norm[i]), wgu1, wd1, i, tm, tf)
        proj = _proj(h, row(mix_norm[i]), w_proj, i, tm_proj, tn)
        nsa_part = _nsa_part(proj, *_nsa_params(cmp_pe[i], cmp_w1[i], cmp_w2[i]), tabs, S)
        hg_part = _hgrn_part(proj, hg_lb, row(hg_norm[i]), consts["j512"], consts["tri16"], consts["tot16"], S, i)
        rw_part = _rwkv_part(
            proj, *_rwkv_params(rw_mu[i], rw_w0[i], rw_wB[i], rw_a0[i], rw_aB[i], rw_gB[i], rw_kk[i], rw_ka[i],
                                rw_rk[i], rw_ln_w[i], rw_ln_b[i]), consts["j512"], consts["tri64"], S)
        (o_ns,) = _run_parts([nsa_part], (B, S // (2 * NSA_TQ)), "nsa")
        o_hg, o_rw = _run_parts([hg_part, rw_part], (B, S // MIX_ROWS), "hgrn2_rwkv7")
        h = _merge(h, proj, o_hg, o_ns, o_rw, wb, wo, i, tm)
        h = _ffn(h, row(ffn2_norm[i]), wgu2, wd2, i, tm, tf)
        h = _ple(h, row(ple_norm[i]), wpg, p_rows, wpp, row(final_norm), i, tm, i == depth - 1)
    return h.reshape(B, S, D)
```

```python
import functools
import itertools
import math

import jax
import jax.numpy as jnp
import numpy as np
from jax import lax
from jax.experimental import pallas as pl
from jax.experimental.pallas import tpu as pltpu

F32 = jnp.float32
BF16 = jnp.bfloat16

RMS_EPS = 1e-6
LOG2E = math.log2(math.e)
LANES = 128
SUBLANES = 8
BF16_ROWS = 16
VMEM_LIMIT = 48 * 1024 * 1024

HG_W = 512
NSA_W = 512
RW_W = 512
D_MODEL = 1024
MIX_ROWS = 512
COL_MG = 0
COL_HG = 3072
COL_NSA = 5120
COL_RW = 6656
N_PROJ = 8704


def _cparams(sem):
    return pltpu.CompilerParams(dimension_semantics=sem, vmem_limit_bytes=VMEM_LIMIT)


def _rms(x, g):
    return x * lax.rsqrt(jnp.mean(x * x, axis=-1, keepdims=True) + RMS_EPS) * g


def _sigmoid(x):
    return 1.0 / (1.0 + jnp.exp(-x))


def _silu(x):
    return x * _sigmoid(x)


def _dot(a, b):
    return jnp.dot(a, b, preferred_element_type=F32)


def _nt(a, b):
    return lax.dot_general(a, b, (((1,), (1,)), ((), ())), preferred_element_type=F32)


def _ffn_kernel(h_ref, g_ref, wgu_ref, wd_ref, *rest, tf, ple, final):
    o_ref = rest[-1]
    h = h_ref[...]
    xn = _rms(h, g_ref[...]).astype(BF16)
    n_ff = wd_ref.shape[0]
    acc = None
    for c in range(0, n_ff, tf):
        gate = _dot(xn, wgu_ref[:, c:c + tf])
        up = _dot(xn, wgu_ref[:, n_ff + c:n_ff + c + tf])
        part = _dot((_silu(gate) * up).astype(BF16), wd_ref[c:c + tf, :])
        acc = part if acc is None else acc + part
    out = h + 0.5 * acc
    if ple:
        pg_ref, wg_ref, p_ref, wp_ref, fg_ref = rest[:-1]
        gate = _sigmoid(_dot(_rms(out, pg_ref[...]).astype(BF16), wg_ref[...]))
        out = out + gate * _dot(p_ref[...].astype(BF16), wp_ref[...])
        if final:
            out = _rms(out, fg_ref[...])
    o_ref[...] = out


def _ffn(h, g, wgu, wd, layer, tm, tf, ple=None, final=False):
    T, D = h.shape
    FF = wd.shape[1]
    resident = dict(pipeline_mode=pl.Buffered(1))
    vec = pl.BlockSpec((1, D), lambda i: (0, 0))
    in_specs = [
        pl.BlockSpec((tm, D), lambda i: (i, 0)),
        vec,
        pl.BlockSpec((None, D, 2 * FF), lambda i: (layer, 0, 0), **resident),
        pl.BlockSpec((None, FF, D), lambda i: (layer, 0, 0), **resident),
    ]
    args = [h, g, wgu, wd]
    if ple is not None:
        P = ple[2].shape[2]
        in_specs += [vec, pl.BlockSpec((None, D, D), lambda i: (layer, 0, 0), **resident),
                     pl.BlockSpec((None, tm, P), lambda i: (layer, i, 0)),
                     pl.BlockSpec((None, P, D), lambda i: (layer, 0, 0), **resident), vec]
        args += list(ple)
    return pl.pallas_call(
        functools.partial(_ffn_kernel, tf=tf, ple=ple is not None, final=final),
        out_shape=jax.ShapeDtypeStruct((T, D), F32),
        grid=(T // tm,),
        in_specs=in_specs,
        out_specs=pl.BlockSpec((tm, D), lambda i: (i, 0)),
        compiler_params=_cparams(("parallel",)),
        name="ffn_ple" if ple is not None else "ffn",
    )(*args)


def _proj_kernel(h_ref, g_ref, w_ref, o_ref):
    o_ref[...] = _dot(_rms(h_ref[...], g_ref[...]).astype(BF16), w_ref[...])


def _proj(h, g, w, layer, tm, tn):
    T, D = h.shape
    N = w.shape[2]
    return pl.pallas_call(
        _proj_kernel,
        out_shape=jax.ShapeDtypeStruct((T, N), F32),
        grid=(N // tn, T // tm),
        in_specs=[
            pl.BlockSpec((tm, D), lambda j, i: (i, 0)),
            pl.BlockSpec((1, D), lambda j, i: (0, 0)),
            pl.BlockSpec((None, D, tn), lambda j, i: (layer, 0, j)),
        ],
        out_specs=pl.BlockSpec((tm, tn), lambda j, i: (i, j)),
        compiler_params=_cparams(("parallel", "parallel")),
        name="in_proj",
    )(h, g, w)


def _merge_kernel(h_ref, m0_ref, m1_ref, m2_ref, a_ref, b_ref, c_ref, wb_ref, wo_ref, o_ref):
    merged = _sigmoid(m0_ref[...]) * _dot(a_ref[...].astype(BF16), wb_ref[0])
    merged += _sigmoid(m1_ref[...]) * _dot(b_ref[...].astype(BF16), wb_ref[1])
    merged += _sigmoid(m2_ref[...]) * _dot(c_ref[...].astype(BF16), wb_ref[2])
    o_ref[...] = h_ref[...] + _dot(merged.astype(BF16), wo_ref[...])


def _merge(h, proj, o_hg, o_ns, o_rw, wb, wo, layer, tm):
    T, D = h.shape
    W = o_hg.shape[1]
    mg0 = COL_MG // D
    return pl.pallas_call(
        _merge_kernel,
        out_shape=jax.ShapeDtypeStruct((T, D), F32),
        grid=(T // tm,),
        in_specs=[
            pl.BlockSpec((tm, D), lambda i: (i, 0)),
            pl.BlockSpec((tm, D), lambda i: (i, mg0)),
            pl.BlockSpec((tm, D), lambda i: (i, mg0 + 1)),
            pl.BlockSpec((tm, D), lambda i: (i, mg0 + 2)),
            pl.BlockSpec((tm, W), lambda i: (i, 0)),
            pl.BlockSpec((tm, W), lambda i: (i, 0)),
            pl.BlockSpec((tm, W), lambda i: (i, 0)),
            pl.BlockSpec((None, 3, W, D), lambda i: (layer, 0, 0, 0)),
            pl.BlockSpec((None, D, D), lambda i: (layer, 0, 0)),
        ],
        out_specs=pl.BlockSpec((tm, D), lambda i: (i, 0)),
        compiler_params=_cparams(("parallel",)),
        name="merge",
    )(h, proj, proj, proj, o_hg, o_ns, o_rw, wb, wo)


def _softplus(x):
    return jnp.maximum(x, 0.0) + jnp.log(1.0 + jnp.exp(-jnp.abs(x)))


def _split2(x):
    hi = x.astype(BF16)
    lo = (x - hi.astype(F32)).astype(BF16)
    return hi, lo


def _split3(x):
    hi = x.astype(BF16)
    r1 = x - hi.astype(F32)
    mid = r1.astype(BF16)
    lo = (r1 - mid.astype(F32)).astype(BF16)
    return hi, mid, lo


def _segsum(x, j):
    hi, lo = _split2(x)
    return _dot(hi, j) + _dot(lo, j)


def _cumsum_rows(tri, x):
    hi, mid, lo = _split3(x)
    return _dot(tri, hi) + (_dot(tri, mid) + _dot(tri, lo))


def _pair_stack(x, lo_mask):
    return jnp.concatenate([jnp.where(lo_mask, x, 0.0), jnp.where(lo_mask, 0.0, x)], axis=0)


def _consts():
    lane = np.arange(HG_W)
    j512 = (lane[:, None] // HG_HD == lane[None, :] // HG_HD).astype(np.float32)
    step = np.arange(RW_CH)
    tri64 = (step[:, None] >= step[None, :]).astype(np.float32)
    r = np.arange(HG_ROWS)
    same = r[:, None] // HG_SUB == r[None, :] // HG_SUB
    tri16 = (same & (r[:, None] >= r[None, :])).astype(np.float32)
    tot16 = same.astype(np.float32)
    return {"j512": jnp.asarray(j512, BF16), "tri64": jnp.asarray(tri64, BF16),
            "tri16": jnp.asarray(tri16, BF16), "tot16": jnp.asarray(tot16, BF16)}


RW_HD = 64
RW_CH = 64
RW_GN_EPS = 64e-5


def _rwkv_steps(r_ref, k_ref, v_ref, l_ref, mu_ref, vec_ref, wb_ref, ab_ref, gb_ref, j_ref,
                tri_ref, o_ref, carry_ref, st_ref, *, nch):
    C = RW_CH
    TC = nch * C
    W = r_ref.shape[1]
    npair = W // LANES

    row = lax.broadcasted_iota(jnp.int32, (TC, W), 0)

    def shift(x_ref, idx):
        x = x_ref[...]
        prev = jnp.where(row == 0, carry_ref[idx:idx + 1, :], pltpu.roll(x, 1, axis=0))
        carry_ref[idx:idx + 1, :] = x[TC - 1:TC, :]
        return x + (prev - x) * mu_ref[idx:idx + 1, :]

    xr = shift(r_ref, 0)
    xk = shift(k_ref, 1)
    xv = shift(v_ref, 2)
    xl = shift(l_ref, 3)
    w0, a0, k_k, k_a = (vec_ref[i:i + 1, :] for i in range(4))
    ln_w, ln_b, r_k = (vec_ref[i:i + 1, :] for i in range(4, 7))
    jmat = j_ref[...]

    wlal = xl[:, 0:LANES]
    w_pre = w0 + _dot(jnp.tanh(wlal).astype(BF16), wb_ref[...])
    a_pre = a0 + _dot(wlal.astype(BF16), ab_ref[...])
    gate = _dot(_sigmoid(xl[:, LANES:2 * LANES]).astype(BF16), gb_ref[...])
    logw = -jnp.exp(-_softplus(-w_pre) - 0.5)
    a = _sigmoid(a_pre)
    kkr = xk * k_k
    kk = kkr / jnp.maximum(jnp.sqrt(_segsum(kkr * kkr, jmat)), 1e-12)
    k2 = xk * (1.0 + (a - 1.0) * k_a)
    ka = kk * a
    yield

    lane = lax.broadcasted_iota(jnp.int32, (C, LANES), 1)
    trow = lax.broadcasted_iota(jnp.int32, (C, LANES), 0)
    lo_mask = lane < RW_HD
    scol = lane & (RW_HD - 1)
    strict = trow > scol
    incl = trow >= scol
    eye2 = (trow == scol).astype(F32)
    r128 = lax.broadcasted_iota(jnp.int32, (LANES, LANES), 0)
    c128 = lax.broadcasted_iota(jnp.int32, (LANES, LANES), 1)
    bd_mask = (r128 // RW_HD) == (c128 // RW_HD)
    diag_mask = r128 == c128
    tri = tri_ref[...]

    def bf(x):
        return x.astype(BF16)

    def stack(x):
        return _pair_stack(x, lo_mask)

    ops = []
    for c in range(nch):
        rs = slice(c * C, (c + 1) * C)
        lw = logw[rs] * LOG2E
        b = _cumsum_rows(tri, lw)
        bend = b[C - 1:C, :]
        enb = jnp.exp2(-b)
        egc = jnp.exp2(bend - b)
        g_end = jnp.exp2(bend)
        full = (xr[rs] * jnp.exp2(b), k2[rs] * enb, ka[rs] * enb, kk[rs] * jnp.exp2(b - lw), k2[rs] * egc,
                ka[rs] * egc, xv[rs], jnp.broadcast_to(g_end, (C, W)))
        for p in range(npair):
            ops.append(tuple(t[:, p * LANES:(p + 1) * LANES] for t in full))
        yield
    n = len(ops)
    gms = [_nt(bf(jnp.concatenate([bt, rt], axis=0)), bf(jnp.concatenate([stack(at), stack(kt)], axis=0)))
           for rt, kt, at, bt, _, _, _, _ in ops]
    a_ba = [jnp.where(strict, gm[0:C, 0:LANES], 0.0) for gm in gms]
    a_bk = [jnp.where(strict, gm[0:C, LANES:], 0.0) for gm in gms]
    a_ra = [jnp.where(incl, gm[C:, 0:LANES], 0.0) for gm in gms]
    a_rk = [jnp.where(incl, gm[C:, LANES:], 0.0) for gm in gms]
    yield
    pw = [-a for a in a_ba]
    ti = [eye2 + x for x in pw]
    pw = [_dot(bf(x), bf(stack(x))) for x in pw]
    yield
    nsq = int(math.log2(C)) - 1
    for k in range(1, nsq):
        both = [_dot(bf(jnp.concatenate([x, t], axis=0)), bf(stack(x))) for x, t in zip(pw, ti)]
        pw = [m[0:C] for m in both]
        ti = [t + m[C:] for t, m in zip(ti, both)]
        yield
    ti = [t + _dot(bf(t), bf(stack(x))) for t, x in zip(ti, pw)]
    tib = [bf(t) for t in ti]
    yield
    wm = [_dot(tib[j], bf(stack(ops[j][3]))) for j in range(n)]
    av = [_dot(bf(a_bk[j]), bf(stack(ops[j][6]))) for j in range(n)]
    yield
    u0 = [_dot(tib[j], bf(stack(av[j]))) for j in range(n)]
    yield
    y0 = [_dot(bf(jnp.concatenate([a_rk[j], a_ra[j]], axis=1)),
               bf(jnp.concatenate([stack(ops[j][6]), -stack(u0[j])], axis=0))) for j in range(n)]
    rw = [ops[j][0] - _dot(bf(a_ra[j]), bf(stack(wm[j]))) for j in range(n)]
    yield
    m2 = [jnp.where(diag_mask, jnp.concatenate([ops[j][7], ops[j][7]], axis=0), 0.0)
          - jnp.where(bd_mask, _dot(bf(ops[j][5].T), bf(wm[j])), 0.0) for j in range(n)]
    yield
    n2 = [jnp.where(bd_mask, _dot(bf(jnp.concatenate([ops[j][4], ops[j][5]], axis=0).T),
                                  bf(jnp.concatenate([ops[j][6], -u0[j]], axis=0))), 0.0) for j in range(n)]
    yield
    s2 = [st_ref[p] for p in range(npair)]
    ys = []
    for c in range(nch):
        js = [c * npair + p for p in range(npair)]
        ys.append(jnp.concatenate([y0[j] + _dot(bf(rw[j]), bf(s2[p])) for p, j in enumerate(js)], axis=1))
        s2 = [_dot(bf(m2[j]), bf(s2[p])) + n2[j] for p, j in enumerate(js)]
        yield
    for p in range(npair):
        st_ref[p] = s2[p]
    y = jnp.concatenate(ys, axis=0) if nch > 1 else ys[0]

    inv_n = 1.0 / RW_HD
    mean = _segsum(y, jmat) * inv_n
    yc = y - mean
    var = _segsum(yc * yc, jmat) * inv_n
    yn = yc * lax.rsqrt(var + RW_GN_EPS) * ln_w + ln_b
    bonus = _segsum(xr * k2 * r_k, jmat) * xv
    o_ref[...] = (yn + bonus) * gate


def _rwkv_params(mu, w0, wB, a0, aB, gB, k_k, k_a, r_k, ln_w, ln_b):
    W = RW_W
    mu4 = jnp.stack([mu[0:W], mu[W:2 * W], mu[2 * W:3 * W], jnp.pad(mu[3 * W:], (0, W - (mu.shape[0] - 3 * W)))])
    vec = jnp.stack([w0, a0, k_k, k_a, ln_w, ln_b, r_k.reshape(-1), jnp.zeros_like(w0)])
    wb = jnp.pad(wB, ((0, LANES - wB.shape[0]), (0, 0))).astype(BF16)
    ab = jnp.pad(aB, ((LANES - aB.shape[0], 0), (0, 0))).astype(BF16)
    return mu4, vec, wb, ab, gB.astype(BF16)


HG_HD = 64
HG_SUB = 16
HG_ROWS = 128


def _hgrn_steps(q_ref, f_ref, i_ref, g_ref, lbp_ref, ng_ref, j_ref, tri_ref, tot_ref, o_ref, st_ref,
                *, layer, rows):
    W = q_ref.shape[1]
    TC = rows.stop - rows.start
    npair = W // LANES
    nsub = TC // HG_SUB

    z = f_ref[rows, :]
    sp = _softplus(-z)
    log_f = -sp
    k = _sigmoid(-z)
    if layer > 0:
        lbp = lbp_ref[...]
        e = jnp.exp(lbp - jnp.max(lbp, axis=0, keepdims=True))
        sm = e / jnp.sum(e, axis=0, keepdims=True)
        lb = sm[1:2, :]
        for j in range(2, layer + 1):
            lb = lb + sm[j:j + 1, :]
        lb = jnp.maximum(lb, 0.0)
        t2 = jnp.log(lb) - (sp + z)
        log_f = jnp.maximum(log_f, t2) + jnp.log(1.0 + jnp.exp(-jnp.abs(log_f - t2)))
        k = (1.0 - lb) * k
    q = _silu(q_ref[rows, :])
    v = i_ref[rows, :]
    log_f = log_f * LOG2E
    b = _cumsum_rows(tri_ref[...], log_f)
    bend = _cumsum_rows(tot_ref[...], log_f)
    qe = q * jnp.exp2(b)
    kg = k * jnp.exp2(bend - b)
    jmat = j_ref[...]
    j128 = jmat[0:LANES, 0:LANES]

    rowb = lax.broadcasted_iota(jnp.int32, (TC, LANES), 0)
    trow = lax.broadcasted_iota(jnp.int32, (HG_SUB // 2, W), 0)
    r128 = lax.broadcasted_iota(jnp.int32, (LANES, LANES), 0)
    c128 = lax.broadcasted_iota(jnp.int32, (LANES, LANES), 1)
    bd_mask = (r128 // HG_HD) == (c128 // HG_HD)

    def bf(x):
        return x.astype(BF16)

    vts = [bf(v[:, p * LANES:(p + 1) * LANES].T) for p in range(npair)]
    yield
    outs = []
    H8 = HG_SUB // 2
    for i in range(nsub):
        rs = slice(i * HG_SUB, (i + 1) * HG_SUB)
        r0 = i * HG_SUB
        (b_a, q_a, k_a, v_a), (b_b, q_b, k_b, v_b) = (
            tuple(t[r0 + h * H8:r0 + (h + 1) * H8] for t in (b, q, k, v)) for h in range(2))
        xs = []
        for s in range(H8):
            bs, ks = b_a[s:s + 1, :], k_a[s:s + 1, :]
            xs.append(jnp.where(trow >= s, q_a * (ks * jnp.exp2(b_a - bs)), 0.0))
            xs.append(q_b * (ks * jnp.exp2(b_b - bs)))
        for s in range(H8):
            bs, ks = b_b[s:s + 1, :], k_b[s:s + 1, :]
            xs.append(jnp.where(trow >= s, q_b * (ks * jnp.exp2(b_b - bs)), 0.0))
        x = bf(jnp.concatenate(xs, axis=0))
        g_end = jnp.exp2(bend[i * HG_SUB:i * HG_SUB + 1, :])
        op = []
        for p in range(npair):
            ls = slice(p * LANES, (p + 1) * LANES)
            pm = _dot(x[:, ls], j128)
            od_a = pm[0:H8] * v_a[0:1, ls]
            od_b = pm[H8:HG_SUB] * v_a[0:1, ls]
            for s in range(1, H8):
                od_a = od_a + pm[s * HG_SUB:s * HG_SUB + H8] * v_a[s:s + 1, ls]
                od_b = od_b + pm[s * HG_SUB + H8:(s + 1) * HG_SUB] * v_a[s:s + 1, ls]
            for s in range(H8):
                od_b = od_b + pm[(HG_SUB + s) * H8:(HG_SUB + s + 1) * H8] * v_b[s:s + 1, ls]
            od = jnp.concatenate([od_a, od_b], axis=0)
            st = st_ref[p]
            oi = lax.dot_general(bf(qe[rs, ls]), bf(st), (((1,), (1,)), ((), ())), preferred_element_type=F32)
            kgm = jnp.where((rowb >= i * HG_SUB) & (rowb < (i + 1) * HG_SUB), kg[:, ls], 0.0)
            st_ref[p] = st * g_end[:, ls] + jnp.where(bd_mask, _dot(vts[p], bf(kgm)), 0.0)
            op.append(od + oi)
        outs.append(jnp.concatenate(op, axis=1))
        yield
    o = jnp.concatenate(outs, axis=0)
    ms = _segsum(o * o, jmat) * (1.0 / HG_HD)
    o_ref[rows, :] = o * lax.rsqrt(ms + RMS_EPS) * ng_ref[...] * _silu(g_ref[rows, :])


def _interleave(gens):
    live = list(gens)
    while live:
        for g in list(live):
            try:
                next(g)
            except StopIteration:
                live.remove(g)


def _run_parts(parts, grid, name):
    n_in = [len(p["args"]) for p in parts]
    n_out = [len(p["out_shape"]) for p in parts]
    n_scr = [len(p["scratch"]) for p in parts]

    def take(refs, counts):
        out, pos = [], 0
        for c in counts:
            out.append(refs[pos:pos + c])
            pos += c
        return out

    def kern(*refs):
        ins = take(refs[:sum(n_in)], n_in)
        outs = take(refs[sum(n_in):sum(n_in) + sum(n_out)], n_out)
        scr = take(refs[sum(n_in) + sum(n_out):], n_scr)
        gens = []
        for p, a, b, c in zip(parts, ins, outs, scr):
            gens += p["gens"](a, b, c)
        _interleave(gens)

    cat = lambda key: [x for p in parts for x in p[key]]
    return pl.pallas_call(
        kern, out_shape=cat("out_shape"), grid=grid, in_specs=cat("in_specs"), out_specs=cat("out_specs"),
        scratch_shapes=cat("scratch"), compiler_params=_cparams(("parallel", "arbitrary")), name=name,
    )(*cat("args"))


def _seq_part_specs(proj, col0, consts, S):
    T = proj.shape[0]
    W = HG_W
    nblk = S // MIX_ROWS
    rows = lambda col: pl.BlockSpec((MIX_ROWS, W), lambda b, i: (b * nblk + i, col))
    const = lambda a: pl.BlockSpec(a.shape, lambda b, i: (0, 0))
    c0 = col0 // W
    return dict(args=[proj] * 4 + list(consts), in_specs=[rows(c0 + j) for j in range(4)] + [const(a) for a in consts],
                out_shape=[jax.ShapeDtypeStruct((T, W), F32)], out_specs=[rows(0)])


def _hgrn_part(proj, lbp, ng, jmat, tri16, tot16, S, layer):
    def gens(ins, outs, scr):
        (st_ref,) = scr

        @pl.when(pl.program_id(1) == 0)
        def _():
            st_ref[...] = jnp.zeros_like(st_ref)

        th = ins[7].shape[0]
        return [itertools.chain(*[_hgrn_steps(*ins, *outs, st_ref, layer=layer, rows=slice(r, r + th))
                                  for r in range(0, MIX_ROWS, th)])]

    return dict(_seq_part_specs(proj, COL_HG, (lbp, ng, jmat, tri16, tot16), S), gens=gens,
                scratch=[pltpu.VMEM((HG_W // LANES, LANES, LANES), F32)])


def _rwkv_part(proj, mu, vec, wb, ab, gb, jmat, tri64, S):
    def gens(ins, outs, scr):
        carry_ref, st_ref = scr

        @pl.when(pl.program_id(1) == 0)
        def _():
            carry_ref[...] = jnp.zeros_like(carry_ref)
            st_ref[...] = jnp.zeros_like(st_ref)

        return [_rwkv_steps(*ins, *outs, carry_ref, st_ref, nch=MIX_ROWS // RW_CH)]

    return dict(_seq_part_specs(proj, COL_RW, (mu, vec, wb, ab, gb, jmat, tri64), S), gens=gens,
                scratch=[pltpu.VMEM((SUBLANES, RW_W), F32), pltpu.VMEM((RW_W // LANES, LANES, LANES), F32)])


NSA_HD = 64
NSA_HEADS = 8
NSA_TQ = 128
CMP_STRIDE = 16
CMP_LEN = 32
SLC_BLOCK = 64
SLC_TOPN = 8
NEG_INF = float("-inf")
V_ROWS = NSA_HD + BF16_ROWS


def _nsa_steps(q_ref, kv1_ref, kv2_ref, gt_ref, w1_ref, pe_ref, w2_ref, biasc_ref, bt_ref, cov_ref, egt_ref,
               o_ref, kc_ref, vct_ref, k_ref, vt_ref, am_ref, m_ref, acc_ref, cx_ref, s_ref, p_ref, al_ref, sn_ref,
               *, bs):
    TQ = NSA_TQ
    S = kv1_ref.shape[0]
    nkc = S // TQ
    ncmp = S // CMP_STRIDE
    nslc = S // SLC_BLOCK
    GW = (NSA_HEADS // 2) * TQ
    i = 2 * pl.program_id(1) + bs
    nq = 2 * pl.num_programs(1)
    q_ref, gt_ref, o_ref = (r.at[bs * TQ:(bs + 1) * TQ] for r in (q_ref, gt_ref, o_ref))
    am_ref, m_ref, acc_ref, s_ref, p_ref, al_ref, sn_ref = (
        r.at[bs] for r in (am_ref, m_ref, acc_ref, s_ref, p_ref, al_ref, sn_ref))
    first_block_only = pl.when(i == 0) if bs == 0 else (lambda f: None)

    def with_ones(vt_g):
        return jnp.concatenate([vt_g, jnp.ones((V_ROWS - NSA_HD, vt_g.shape[1]), F32)], axis=0).astype(BF16)

    @first_block_only
    def _():
        k_ref[0] = kv1_ref[:, 2 * LANES:3 * LANES].astype(BF16)
        k_ref[1] = kv2_ref[:, 0:LANES].astype(BF16)
        for src, (ref, c0) in enumerate(((kv1_ref, 3), (kv2_ref, 1))):
            for c in range(nkc):
                vt = ref[c * TQ:(c + 1) * TQ, c0 * LANES:(c0 + 1) * LANES].T
                for g in range(2):
                    vt_ref[src, g, c] = with_ones(vt[g * NSA_HD:(g + 1) * NSA_HD])
        for t in range(2):
            cx_ref[...] = kv1_ref[:, t * LANES:(t + 1) * LANES]
            acc_a = jnp.zeros((ncmp, 2 * LANES), F32)
            acc_b = jnp.zeros((ncmp, 2 * LANES), F32)
            for l in range(CMP_STRIDE):
                xl = cx_ref[pl.ds(l, ncmp, stride=CMP_STRIDE), :]
                acc_a += _dot((xl + pe_ref[t, l:l + 1, :]).astype(BF16), w1_ref[t, l])
                acc_b += _dot((xl + pe_ref[t, CMP_STRIDE + l:CMP_STRIDE + l + 1, :]).astype(BF16),
                              w1_ref[t, CMP_STRIDE + l])
            hid = _silu(acc_a + pltpu.roll(acc_b, ncmp - 1, axis=0))
            cmp = _dot(hid.astype(BF16), w2_ref[t])
            if t == 0:
                kc_ref[...] = cmp.astype(BF16)
            else:
                ct = cmp.T
                for g in range(2):
                    vct_ref[g] = with_ones(ct[g * NSA_HD:(g + 1) * NSA_HD])

    qs = q_ref[...] * (LOG2E * NSA_HD ** -0.5)
    zero = jnp.zeros((NSA_HD, TQ), F32)
    cols = []
    for pp in range(NSA_HEADS // 2):
        qt = qs[:, pp * LANES:(pp + 1) * LANES].T
        for e in range(2):
            qh = qt[e * NSA_HD:(e + 1) * NSA_HD]
            cols.append(jnp.concatenate([qh, zero] if pp < 2 else [zero, qh], axis=0))
    wq = jnp.concatenate(cols, axis=1).astype(BF16)

    def group_rows(o_t):
        return jnp.concatenate([o_t[:, hh * TQ:(hh + 1) * TQ] for hh in range(NSA_HEADS // 2)], axis=0)

    def raw_scores(br, kc):
        return _dot(k_ref[br, pl.ds(pl.multiple_of(kc * TQ, TQ), TQ), :], wq)

    cpq = TQ // CMP_STRIDE
    sc = _dot(kc_ref[...], wq) + biasc_ref[pl.ds(pl.multiple_of(cpq * (nq - 1 - i), cpq), ncmp), :]
    near = [i - 2, i - 1, i]
    near_kc = [jnp.maximum(kc, 0) for kc in near]
    for t, kc in enumerate(near_kc):
        for br in range(2):
            sn_ref[br, t] = raw_scores(br, kc)
    for u in range(2):
        s_ref[0, u] = raw_scores(0, u)
    yield

    mx = jnp.max(sc, axis=0, keepdims=True)
    ex = jnp.exp2(sc - jnp.where(mx == NEG_INF, 0.0, mx))
    p = ex * (1.0 / jnp.maximum(jnp.sum(ex, axis=0, keepdims=True), 1e-30))
    o_cmp = []
    for g in range(2):
        pg = p[:, g * GW:(g + 1) * GW]
        o_cmp.append(group_rows(_dot(vct_ref[g, 0:NSA_HD, :], pg.astype(BF16))))
        psum = pg[:, 0:TQ]
        for hh in range(1, NSA_HEADS // 2):
            psum = psum + pg[:, hh * TQ:(hh + 1) * TQ]
        hi, lw = _split2(psum)
        imp_t = _dot(cov_ref[...], hi) + _dot(cov_ref[...], lw)
        blk = lax.broadcasted_iota(jnp.int32, (nslc, TQ), 0)
        cur = (i * TQ + lax.broadcasted_iota(jnp.int32, (nslc, TQ), 1)) // SLC_BLOCK
        forced = (blk == 0) | (blk == cur) | (blk == cur - 1)
        score = jnp.where(forced, jnp.inf, jnp.where(blk <= cur, imp_t, NEG_INF))
        cnt = jnp.zeros((nslc, TQ), F32)
        for mp in range(nslc):
            sm = score[mp:mp + 1, :]
            ahead = (sm > score) | ((sm == score) & (blk > mp))
            cnt = cnt + jnp.where(ahead, 1.0, 0.0)
        am_ref[g, 0:nslc] = jnp.where(cnt < SLC_TOPN, 0.0, NEG_INF)
        am_ref[g, nslc:] = jnp.full((am_ref.shape[1] - nslc, TQ), NEG_INF, F32)

    yield

    def reset(br):
        m_ref[br] = jnp.full(m_ref.shape[1:], NEG_INF, F32)
        acc_ref[br] = jnp.zeros(acc_ref.shape[1:], F32)

    def block_mask(r0):
        mk = []
        for g in range(2):
            halves = [jnp.broadcast_to(am_ref[g, pl.ds(r0 + u, 1), :], (SLC_BLOCK, TQ)) for u in range(2)]
            mk.append(jnp.concatenate(halves, axis=0))
        return jnp.concatenate([mk[0]] * (NSA_HEADS // 2) + [mk[1]] * (NSA_HEADS // 2), axis=1)

    def softmax_update(br, ss):
        m_prev = m_ref[br]
        m_new = m_prev
        for s in ss:
            m_new = jnp.maximum(m_new, jnp.max(s, axis=0, keepdims=True))
        m_safe = jnp.where(m_new == NEG_INF, 0.0, m_new)
        m_ref[br] = m_new
        return jnp.exp2(m_prev - m_safe), [jnp.exp2(s - m_safe).astype(BF16) for s in ss]

    def value_update(br, alpha, kcs, pts):
        for g in range(2):
            upd = alpha[:, g * GW:(g + 1) * GW] * acc_ref[br, g]
            for kc, pt in zip(kcs, pts):
                upd = upd + _dot(vt_ref[br, g, kc], pt[:, g * GW:(g + 1) * GW])
            acc_ref[br, g] = upd

    def result(br):
        outs = []
        for g in range(2):
            acc = acc_ref[br, g]
            inv = 1.0 / jnp.maximum(acc[NSA_HD:NSA_HD + 1, :], 1e-30)
            outs.append(group_rows(acc[0:NSA_HD, :] * inv))
        return outs

    reset(1)
    win_tiles = [jnp.where(kc < 0, 0, tile) for kc, tile in zip(near, (4, 2, 1))]
    alpha, pts = softmax_update(1, [sn_ref[1, t] + bt_ref[win_tiles[t]] for t in range(3)])
    value_update(1, alpha, near_kc, pts)
    yield

    reset(0)
    nfar = jnp.maximum(i - 1, 0)
    nfp = nfar // 2
    tail_tiles = [jnp.where((near[0] >= 0) & (nfar - 2 * nfp == 1), 3, 0), jnp.where(near[1] >= 0, 2, 0), 1]
    alpha, pts = softmax_update(0, [sn_ref[0, t] + bt_ref[tail_tiles[t]] + block_mask(2 * near_kc[t])
                                    for t in range(3)])
    value_update(0, alpha, near_kc, pts)
    yield

    def pair_scores(pr, slot):
        for u in range(2):
            s_ref[slot, u] = raw_scores(0, jnp.clip(2 * pr + u, 0, nkc - 1))

    def pair_values(pr, slot):
        value_update(0, al_ref[slot], [jnp.clip(2 * pr + u, 0, nkc - 1) for u in range(2)],
                     [p_ref[slot, u] for u in range(2)])

    def pair_step(pr, slot):
        pair_scores(pr + 1, 1 - slot)
        pair_values(pr - 1, 1 - slot)
        rows = [jnp.where(pr < nfp, 2 * (2 * pr + u), nslc) for u in range(2)]
        alpha, pts = softmax_update(0, [s_ref[slot, u] + block_mask(rows[u]) for u in range(2)])
        al_ref[slot] = alpha
        for u in range(2):
            p_ref[slot, u] = pts[u]

    p_ref[1] = jnp.zeros(p_ref.shape[1:], BF16)
    al_ref[1] = jnp.ones(al_ref.shape[1:], F32)

    def two_pairs(q2, carry):
        pair_step(2 * q2, 0)
        pair_step(2 * q2 + 1, 1)
        return carry

    ntrip = (nfp + 1) // 2
    lax.fori_loop(0, ntrip, two_pairs, 0)
    yield
    pair_values(2 * ntrip - 1, 1)

    o_slc = result(0)
    o_win = result(1)

    sig_hi, sig_lo = _split2(_sigmoid(gt_ref[...]).T)
    out_t = None
    for j, parts in enumerate((o_cmp, o_slc, o_win)):
        gate_t = _dot(egt_ref[j], sig_hi) + _dot(egt_ref[j], sig_lo)
        term = gate_t * jnp.concatenate(parts, axis=0)
        out_t = term if out_t is None else out_t + term
    o_ref[...] = jnp.concatenate([out_t[c * LANES:(c + 1) * LANES].T for c in range(NSA_W // LANES)], axis=1)


def _nsa_part(proj, w1b, peb, w2b, tabs, S):
    T = proj.shape[0]
    W = NSA_W
    TQ = NSA_TQ
    nq = S // TQ
    HW = NSA_HEADS * TQ
    c0 = COL_NSA // W
    cg = (COL_NSA + 2 * W + 2 * LANES) // LANES
    biasc, bt, cov, egt = tabs

    def const(shape):
        return pl.BlockSpec(shape, lambda b, i, _n=len(shape): (0,) * _n)

    nstep = nq // 2

    def gens(ins, outs, scr):
        return [_nsa_steps(*ins, *outs, *scr, bs=bs) for bs in range(2)]

    return dict(
        args=[proj, proj, proj, proj, w1b, peb, w2b, biasc, bt, cov, egt], gens=gens,
        out_shape=[jax.ShapeDtypeStruct((T, W), F32)],
        in_specs=[
            pl.BlockSpec((2 * TQ, W), lambda b, i: (b * nstep + i, c0)),
            pl.BlockSpec((S, W), lambda b, i: (b, c0 + 1)),
            pl.BlockSpec((S, 2 * LANES), lambda b, i: (b, (COL_NSA + 2 * W) // (2 * LANES))),
            pl.BlockSpec((2 * TQ, LANES), lambda b, i: (b * nstep + i, cg)),
            const(w1b.shape), const(peb.shape), const(w2b.shape),
            const(biasc.shape), const(bt.shape), const(cov.shape), const(egt.shape),
        ],
        out_specs=[pl.BlockSpec((2 * TQ, W), lambda b, i: (b * nstep + i, 0))],
        scratch=[
            pltpu.VMEM((S // CMP_STRIDE, LANES), BF16),
            pltpu.VMEM((2, V_ROWS, S // CMP_STRIDE), BF16),
            pltpu.VMEM((2, S, LANES), BF16),
            pltpu.VMEM((2, 2, S // TQ, V_ROWS, TQ), BF16),
            pltpu.VMEM((2, 2, S // SLC_BLOCK + SUBLANES, TQ), F32),
            pltpu.VMEM((2, 2, 1, HW), F32),
            pltpu.VMEM((2, 2, 2, V_ROWS, HW // 2), F32),
            pltpu.VMEM((S, LANES), F32),
            pltpu.VMEM((2, 2, 2, TQ, HW), F32),
            pltpu.VMEM((2, 2, 2, TQ, HW), BF16),
            pltpu.VMEM((2, 2, 1, HW), F32),
            pltpu.VMEM((2, 2, 3, TQ, HW), F32),
        ])


def _blockdiag2(w):
    z = jnp.zeros_like(w)
    return jnp.concatenate([jnp.concatenate([w, z], axis=-1), jnp.concatenate([z, w], axis=-1)], axis=-2)


def _nsa_params(pe, w1, w2):
    L, Dh = pe.shape[1], pe.shape[2]
    w1b = _blockdiag2(w1.reshape(2, L, Dh, -1)).astype(BF16)
    w2b = _blockdiag2(w2).astype(BF16)
    peb = jnp.concatenate([pe, pe], axis=-1)
    return w1b, peb, w2b


def _t5_bucket_np(n):
    n = np.maximum(n, 0)
    nf = np.maximum(n, 1).astype(np.float64)
    large = 16 + (np.log(nf / 16.0) / math.log(128 / 16) * 16).astype(np.int64)
    return np.where(n < 16, n, np.minimum(large, 31))


def _bias_table_kernel(rb_ref, idx_ref, add_ref, shift_ref, o_ref):
    nb, nh = rb_ref.shape
    idx = idx_ref[...]
    acc = [jnp.zeros(idx.shape, F32) for _ in range(nh)]
    for b in range(nb):
        hit = idx == b
        acc = [jnp.where(hit, rb_ref[b, h], a) for h, a in enumerate(acc)]
    for h in range(nh):
        o_ref[:, h * NSA_TQ:(h + 1) * NSA_TQ] = (acc[h] - shift_ref[...] * rb_ref[nb - 1, h]) * LOG2E + add_ref[...]


def _nsa_tables(rel_bias, S):
    TQ = NSA_TQ
    nq = S // TQ
    tiles = []

    def tile(dist, valid, shift):
        tiles.append((_t5_bucket_np(dist), np.where(valid, 0.0, -np.inf), np.full(dist.shape, float(shift))))

    ncmp = S // CMP_STRIDE
    end = np.arange(ncmp) * CMP_STRIDE + CMP_LEN - 1
    t = np.arange(TQ)[None, :]
    j = np.arange(TQ)[:, None]
    every = np.ones((TQ, TQ), bool)
    tile(t - j, ~every, True)
    tile(t - j, t >= j, True)
    tile(t - j + TQ, every, True)
    tile(t - j + 2 * TQ, every, True)
    tile(t - j + 2 * TQ, j > t, True)
    assert _t5_bucket_np(np.asarray(TQ + 1)) == _t5_bucket_np(np.asarray(S)) == rel_bias.shape[0] - 1
    cpq = TQ // CMP_STRIDE
    rel_blk = np.arange(cpq * (nq - 1) + ncmp) - cpq * (nq - 1)
    dist_c = np.arange(TQ)[None, :] - (rel_blk[:, None] * CMP_STRIDE + CMP_LEN - 1)
    tile(dist_c, dist_c >= 0, False)
    idx, add, shift = (np.concatenate([x[k] for x in tiles], axis=0) for k in range(3))
    full = lambda a: pl.BlockSpec(a.shape, lambda: (0,) * a.ndim)
    table = pl.pallas_call(
        _bias_table_kernel,
        out_shape=jax.ShapeDtypeStruct((idx.shape[0], NSA_HEADS * TQ), F32),
        in_specs=[pl.BlockSpec(memory_space=pltpu.SMEM), full(idx), full(add), full(shift)],
        out_specs=pl.BlockSpec((idx.shape[0], NSA_HEADS * TQ), lambda: (0, 0)),
        name="bias_tables",
    )(rel_bias.astype(F32), jnp.asarray(idx, jnp.int32), jnp.asarray(add, F32), jnp.asarray(shift, F32))
    bt = table[:5 * TQ].reshape(5, TQ, NSA_HEADS * TQ)
    biasc = table[5 * TQ:]
    nslc = S // SLC_BLOCK
    s_lo = np.arange(nslc) * SLC_BLOCK
    start = np.arange(ncmp) * CMP_STRIDE
    cover = ((start[:, None] <= (s_lo + SLC_BLOCK - 1)[None, :]) & (end[:, None] >= s_lo[None, :]))
    cover &= (np.arange(ncmp) < ncmp - 1)[:, None]
    egt = np.zeros((3, NSA_W, LANES), np.float32)
    for jj in range(3):
        egt[jj, np.arange(NSA_W), (np.arange(NSA_W) // NSA_HD) * 3 + jj] = 1.0
    return biasc, bt, jnp.asarray(cover.T.astype(np.float32), BF16), jnp.asarray(egt, BF16)


def _tiles(T, n_ff):
    tm = 512 if T % 512 == 0 else T
    tm_proj = 2 * tm if T % (2 * tm) == 0 else tm
    tf = n_ff // 2 if (n_ff // 2) % LANES == 0 else n_ff
    tn = N_PROJ // 4
    return tm, tm_proj, tf, tn


def _pack_w_in(w):
    hg_end = 4 * HG_W
    nsa_end = hg_end + NSA_W + 6 * 2 * NSA_HD + 3 * 8
    rw_end = nsa_end + 3 * RW_W + 256
    out = jnp.zeros(w.shape[:-1] + (N_PROJ,), BF16)
    for dst, lo, hi in ((COL_MG, rw_end, w.shape[-1]), (COL_HG, 0, hg_end), (COL_NSA, hg_end, nsa_end),
                        (COL_RW, nsa_end, rw_end)):
        out = out.at[..., dst:dst + hi - lo].set(w[..., lo:hi].astype(BF16))
    return out


def kernel(x, p, ffn1_norm, ffn1_wgu, ffn1_wd, mix_norm, w_in, hg_lb, hg_norm, cmp_pe, cmp_w1, cmp_w2, rel_bias, rw_mu, rw_w0, rw_wB, rw_a0, rw_aB, rw_gB, rw_kk, rw_ka, rw_rk, rw_ln_w, rw_ln_b, w_branch, w_out, ffn2_norm, ffn2_wgu, ffn2_wd, ple_norm, ple_gate_w, ple_w, final_norm):
    B, S, D = x.shape
    depth = ffn1_norm.shape[0]
    T = B * S
    assert D == D_MODEL and S // CMP_STRIDE == LANES and S % NSA_TQ == 0
    assert w_in.shape[2] - 3 * D_MODEL == 4 * HG_W + NSA_W + 12 * NSA_HD + 24 + 3 * RW_W + 256
    tm, tm_proj, tf, tn = _tiles(T, ffn1_wd.shape[1])
    consts = _consts()
    tabs = _nsa_tables(rel_bias, S)
    row = lambda v: v.reshape(1, -1)
    wgu1, wd1, wgu2, wd2 = (w.astype(BF16) for w in (ffn1_wgu, ffn1_wd, ffn2_wgu, ffn2_wd))
    w_proj = _pack_w_in(w_in)
    wb, wo, wpg, wpp = (w.astype(BF16) for w in (w_branch, w_out, ple_gate_w, ple_w))
    p_rows = p.reshape(depth, T, -1)
    h = x.reshape(T, D)
    for i in range(depth):
        h = _ffn(h, row(ffn1_norm[i]), wgu1, wd1, i, tm, tf)
        proj = _proj(h, row(mix_norm[i]), w_proj, i, tm_proj, tn)
        nsa_part = _nsa_part(proj, *_nsa_params(cmp_pe[i], cmp_w1[i], cmp_w2[i]), tabs, S)
        hg_part = _hgrn_part(proj, hg_lb, row(hg_norm[i]), consts["j512"], consts["tri16"], consts["tot16"], S, i)
        rw_part = _rwkv_part(
            proj, *_rwkv_params(rw_mu[i], rw_w0[i], rw_wB[i], rw_a0[i], rw_aB[i], rw_gB[i], rw_kk[i], rw_ka[i],
                                rw_rk[i], rw_ln_w[i], rw_ln_b[i]), consts["j512"], consts["tri64"], S)
        (o_ns,) = _run_parts([nsa_part], (B, S // (2 * NSA_TQ)), "nsa")
        o_hg, o_rw = _run_parts([hg_part, rw_part], (B, S // MIX_ROWS), "hgrn2_rwkv7")
        h = _merge(h, proj, o_hg, o_ns, o_rw, wb, wo, i, tm)
        h = _ffn(h, row(ffn2_norm[i]), wgu2, wd2, i, tm, tf,
                 ple=(row(ple_norm[i]), wpg, p_rows, wpp, row(final_norm)), final=i == depth - 1)
    return h.reshape(B, S, D)
```

```python
import functools
import itertools
import math

import jax
import jax.numpy as jnp
import numpy as np
from jax import lax
from jax.experimental import pallas as pl
from jax.experimental.pallas import tpu as pltpu

F32 = jnp.float32
BF16 = jnp.bfloat16

RMS_EPS = 1e-6
LOG2E = math.log2(math.e)
LANES = 128
SUBLANES = 8
BF16_ROWS = 16
VMEM_LIMIT = 48 * 1024 * 1024

HG_W = 512
NSA_W = 512
RW_W = 512
D_MODEL = 1024
MIX_ROWS = 512
SEQ_LANES = 256
COL_MG = 0
COL_HG = 3072
COL_NSA = 5120
COL_RW = 6656
N_PROJ = 8704


def _cparams(sem):
    return pltpu.CompilerParams(dimension_semantics=sem, vmem_limit_bytes=VMEM_LIMIT)


def _rms(x, g):
    return x * lax.rsqrt(jnp.mean(x * x, axis=-1, keepdims=True) + RMS_EPS) * g


def _sigmoid(x):
    return 1.0 / (1.0 + jnp.exp(-x))


def _silu(x):
    return x * _sigmoid(x)


def _dot(a, b):
    return jnp.dot(a, b, preferred_element_type=F32)


def _nt(a, b):
    return lax.dot_general(a, b, (((1,), (1,)), ((), ())), preferred_element_type=F32)


def _ffn_kernel(h_ref, g_ref, wgu_ref, wd_ref, *rest, tf, ple, final):
    o_ref = rest[-1]
    h = h_ref[...]
    xn = _rms(h, g_ref[...]).astype(BF16)
    n_ff = wd_ref.shape[0]
    acc = None
    for c in range(0, n_ff, tf):
        gate = _dot(xn, wgu_ref[:, c:c + tf])
        up = _dot(xn, wgu_ref[:, n_ff + c:n_ff + c + tf])
        part = _dot((_silu(gate) * up).astype(BF16), wd_ref[c:c + tf, :])
        acc = part if acc is None else acc + part
    out = h + 0.5 * acc
    if ple:
        pg_ref, wg_ref, p_ref, wp_ref, fg_ref = rest[:-1]
        gate = _sigmoid(_dot(_rms(out, pg_ref[...]).astype(BF16), wg_ref[...]))
        out = out + gate * _dot(p_ref[...].astype(BF16), wp_ref[...])
        if final:
            out = _rms(out, fg_ref[...])
    o_ref[...] = out


def _ffn(h, g, wgu, wd, layer, tm, tf, ple=None, final=False):
    T, D = h.shape
    FF = wd.shape[1]
    resident = dict(pipeline_mode=pl.Buffered(1))
    vec = pl.BlockSpec((1, D), lambda i: (0, 0))
    in_specs = [
        pl.BlockSpec((tm, D), lambda i: (i, 0)),
        vec,
        pl.BlockSpec((None, D, 2 * FF), lambda i: (layer, 0, 0), **resident),
        pl.BlockSpec((None, FF, D), lambda i: (layer, 0, 0), **resident),
    ]
    args = [h, g, wgu, wd]
    if ple is not None:
        P = ple[2].shape[2]
        in_specs += [vec, pl.BlockSpec((None, D, D), lambda i: (layer, 0, 0), **resident),
                     pl.BlockSpec((None, tm, P), lambda i: (layer, i, 0)),
                     pl.BlockSpec((None, P, D), lambda i: (layer, 0, 0), **resident), vec]
        args += list(ple)
    return pl.pallas_call(
        functools.partial(_ffn_kernel, tf=tf, ple=ple is not None, final=final),
        out_shape=jax.ShapeDtypeStruct((T, D), F32),
        grid=(T // tm,),
        in_specs=in_specs,
        out_specs=pl.BlockSpec((tm, D), lambda i: (i, 0)),
        compiler_params=_cparams(("parallel",)),
        name="ffn_ple" if ple is not None else "ffn",
    )(*args)


def _proj_kernel(h_ref, g_ref, w_ref, o_ref):
    o_ref[...] = _dot(_rms(h_ref[...], g_ref[...]).astype(BF16), w_ref[...])


def _proj(h, g, w, layer, tm, tn):
    T, D = h.shape
    N = w.shape[2]
    return pl.pallas_call(
        _proj_kernel,
        out_shape=jax.ShapeDtypeStruct((T, N), F32),
        grid=(N // tn, T // tm),
        in_specs=[
            pl.BlockSpec((tm, D), lambda j, i: (i, 0)),
            pl.BlockSpec((1, D), lambda j, i: (0, 0)),
            pl.BlockSpec((None, D, tn), lambda j, i: (layer, 0, j)),
        ],
        out_specs=pl.BlockSpec((tm, tn), lambda j, i: (i, j)),
        compiler_params=_cparams(("parallel", "parallel")),
        name="in_proj",
    )(h, g, w)


def _merge_kernel(h_ref, m0_ref, m1_ref, m2_ref, a_ref, b_ref, c_ref, wb_ref, wo_ref, o_ref):
    merged = _sigmoid(m0_ref[...]) * _dot(a_ref[...].astype(BF16), wb_ref[0])
    merged += _sigmoid(m1_ref[...]) * _dot(b_ref[...].astype(BF16), wb_ref[1])
    merged += _sigmoid(m2_ref[...]) * _dot(c_ref[...].astype(BF16), wb_ref[2])
    o_ref[...] = h_ref[...] + _dot(merged.astype(BF16), wo_ref[...])


def _merge(h, proj, o_hg, o_ns, o_rw, wb, wo, layer, tm):
    T, D = h.shape
    W = o_hg.shape[1]
    mg0 = COL_MG // D
    return pl.pallas_call(
        _merge_kernel,
        out_shape=jax.ShapeDtypeStruct((T, D), F32),
        grid=(T // tm,),
        in_specs=[
            pl.BlockSpec((tm, D), lambda i: (i, 0)),
            pl.BlockSpec((tm, D), lambda i: (i, mg0)),
            pl.BlockSpec((tm, D), lambda i: (i, mg0 + 1)),
            pl.BlockSpec((tm, D), lambda i: (i, mg0 + 2)),
            pl.BlockSpec((tm, W), lambda i: (i, 0)),
            pl.BlockSpec((tm, W), lambda i: (i, 0)),
            pl.BlockSpec((tm, W), lambda i: (i, 0)),
            pl.BlockSpec((None, 3, W, D), lambda i: (layer, 0, 0, 0)),
            pl.BlockSpec((None, D, D), lambda i: (layer, 0, 0)),
        ],
        out_specs=pl.BlockSpec((tm, D), lambda i: (i, 0)),
        compiler_params=_cparams(("parallel",)),
        name="merge",
    )(h, proj, proj, proj, o_hg, o_ns, o_rw, wb, wo)


def _softplus(x):
    return jnp.maximum(x, 0.0) + jnp.log(1.0 + jnp.exp(-jnp.abs(x)))


def _split2(x):
    hi = x.astype(BF16)
    lo = (x - hi.astype(F32)).astype(BF16)
    return hi, lo


def _split3(x):
    hi = x.astype(BF16)
    r1 = x - hi.astype(F32)
    mid = r1.astype(BF16)
    lo = (r1 - mid.astype(F32)).astype(BF16)
    return hi, mid, lo


def _segsum(x, j):
    hi, lo = _split2(x)
    return _dot(hi, j) + _dot(lo, j)


def _cumsum_rows(tri, x):
    hi, mid, lo = _split3(x)
    return _dot(tri, hi) + (_dot(tri, mid) + _dot(tri, lo))


def _pair_stack(x, lo_mask):
    return jnp.concatenate([jnp.where(lo_mask, x, 0.0), jnp.where(lo_mask, 0.0, x)], axis=0)


def _consts():
    lane = np.arange(SEQ_LANES)
    j512 = (lane[:, None] // HG_HD == lane[None, :] // HG_HD).astype(np.float32)
    step = np.arange(RW_CH)
    tri64 = (step[:, None] >= step[None, :]).astype(np.float32)
    r = np.arange(HG_ROWS)
    same = r[:, None] // HG_SUB == r[None, :] // HG_SUB
    tri16 = (same & (r[:, None] >= r[None, :])).astype(np.float32)
    tot16 = same.astype(np.float32)
    return {"j512": jnp.asarray(j512, BF16), "tri64": jnp.asarray(tri64, BF16),
            "tri16": jnp.asarray(tri16, BF16), "tot16": jnp.asarray(tot16, BF16)}


RW_HD = 64
RW_CH = 64
RW_GN_EPS = 64e-5


def _rwkv_steps(r_ref, k_ref, v_ref, l_ref, mu_ref, vec_ref, wb_ref, ab_ref, gb_ref, j_ref,
                tri_ref, o_ref, carry_ref, st_ref, *, nch):
    C = RW_CH
    TC = nch * C
    W = r_ref.shape[1]
    npair = W // LANES

    row = lax.broadcasted_iota(jnp.int32, (TC, W), 0)

    def shift(x_ref, idx):
        x = x_ref[...]
        prev = jnp.where(row == 0, carry_ref[idx:idx + 1, :], pltpu.roll(x, 1, axis=0))
        carry_ref[idx:idx + 1, :] = x[TC - 1:TC, :]
        return x + (prev - x) * mu_ref[idx:idx + 1, :]

    xr = shift(r_ref, 0)
    xk = shift(k_ref, 1)
    xv = shift(v_ref, 2)
    xl = shift(l_ref, 3)
    w0, a0, k_k, k_a = (vec_ref[i:i + 1, :] for i in range(4))
    ln_w, ln_b, r_k = (vec_ref[i:i + 1, :] for i in range(4, 7))
    jmat = j_ref[...]

    wlal = xl[:, 0:LANES]
    w_pre = w0 + _dot(jnp.tanh(wlal).astype(BF16), wb_ref[...])
    a_pre = a0 + _dot(wlal.astype(BF16), ab_ref[...])
    gate = _dot(_sigmoid(xl[:, LANES:2 * LANES]).astype(BF16), gb_ref[...])
    logw = -jnp.exp(-_softplus(-w_pre) - 0.5)
    a = _sigmoid(a_pre)
    kkr = xk * k_k
    kk = kkr / jnp.maximum(jnp.sqrt(_segsum(kkr * kkr, jmat)), 1e-12)
    k2 = xk * (1.0 + (a - 1.0) * k_a)
    ka = kk * a
    yield

    lane = lax.broadcasted_iota(jnp.int32, (C, LANES), 1)
    trow = lax.broadcasted_iota(jnp.int32, (C, LANES), 0)
    lo_mask = lane < RW_HD
    scol = lane & (RW_HD - 1)
    strict = trow > scol
    incl = trow >= scol
    eye2 = (trow == scol).astype(F32)
    r128 = lax.broadcasted_iota(jnp.int32, (LANES, LANES), 0)
    c128 = lax.broadcasted_iota(jnp.int32, (LANES, LANES), 1)
    bd_mask = (r128 // RW_HD) == (c128 // RW_HD)
    diag_mask = r128 == c128
    tri = tri_ref[...]

    def bf(x):
        return x.astype(BF16)

    def stack(x):
        return _pair_stack(x, lo_mask)

    ops = []
    for c in range(nch):
        rs = slice(c * C, (c + 1) * C)
        lw = logw[rs] * LOG2E
        b = _cumsum_rows(tri, lw)
        bend = b[C - 1:C, :]
        enb = jnp.exp2(-b)
        egc = jnp.exp2(bend - b)
        g_end = jnp.exp2(bend)
        full = (xr[rs] * jnp.exp2(b), k2[rs] * enb, ka[rs] * enb, kk[rs] * jnp.exp2(b - lw), k2[rs] * egc,
                ka[rs] * egc, xv[rs], jnp.broadcast_to(g_end, (C, W)))
        for p in range(npair):
            ops.append(tuple(t[:, p * LANES:(p + 1) * LANES] for t in full))
        yield
    n = len(ops)
    gms = [_nt(bf(jnp.concatenate([bt, rt], axis=0)), bf(jnp.concatenate([stack(at), stack(kt)], axis=0)))
           for rt, kt, at, bt, _, _, _, _ in ops]
    a_ba = [jnp.where(strict, gm[0:C, 0:LANES], 0.0) for gm in gms]
    a_bk = [jnp.where(strict, gm[0:C, LANES:], 0.0) for gm in gms]
    a_ra = [jnp.where(incl, gm[C:, 0:LANES], 0.0) for gm in gms]
    a_rk = [jnp.where(incl, gm[C:, LANES:], 0.0) for gm in gms]
    yield
    pw = [-a for a in a_ba]
    ti = [eye2 + x for x in pw]
    pw = [_dot(bf(x), bf(stack(x))) for x in pw]
    yield
    nsq = int(math.log2(C)) - 1
    for k in range(1, nsq):
        both = [_dot(bf(jnp.concatenate([x, t], axis=0)), bf(stack(x))) for x, t in zip(pw, ti)]
        pw = [m[0:C] for m in both]
        ti = [t + m[C:] for t, m in zip(ti, both)]
        yield
    ti = [t + _dot(bf(t), bf(stack(x))) for t, x in zip(ti, pw)]
    tib = [bf(t) for t in ti]
    yield
    wm = [_dot(tib[j], bf(stack(ops[j][3]))) for j in range(n)]
    av = [_dot(bf(a_bk[j]), bf(stack(ops[j][6]))) for j in range(n)]
    yield
    u0 = [_dot(tib[j], bf(stack(av[j]))) for j in range(n)]
    yield
    y0 = [_dot(bf(jnp.concatenate([a_rk[j], a_ra[j]], axis=1)),
               bf(jnp.concatenate([stack(ops[j][6]), -stack(u0[j])], axis=0))) for j in range(n)]
    rw = [ops[j][0] - _dot(bf(a_ra[j]), bf(stack(wm[j]))) for j in range(n)]
    yield
    m2 = [jnp.where(diag_mask, jnp.concatenate([ops[j][7], ops[j][7]], axis=0), 0.0)
          - jnp.where(bd_mask, _dot(bf(ops[j][5].T), bf(wm[j])), 0.0) for j in range(n)]
    yield
    n2 = [jnp.where(bd_mask, _dot(bf(jnp.concatenate([ops[j][4], ops[j][5]], axis=0).T),
                                  bf(jnp.concatenate([ops[j][6], -u0[j]], axis=0))), 0.0) for j in range(n)]
    yield
    s2 = [st_ref[p] for p in range(npair)]
    ys = []
    for c in range(nch):
        js = [c * npair + p for p in range(npair)]
        ys.append(jnp.concatenate([y0[j] + _dot(bf(rw[j]), bf(s2[p])) for p, j in enumerate(js)], axis=1))
        s2 = [_dot(bf(m2[j]), bf(s2[p])) + n2[j] for p, j in enumerate(js)]
        yield
    for p in range(npair):
        st_ref[p] = s2[p]
    y = jnp.concatenate(ys, axis=0) if nch > 1 else ys[0]

    inv_n = 1.0 / RW_HD
    mean = _segsum(y, jmat) * inv_n
    yc = y - mean
    var = _segsum(yc * yc, jmat) * inv_n
    yn = yc * lax.rsqrt(var + RW_GN_EPS) * ln_w + ln_b
    bonus = _segsum(xr * k2 * r_k, jmat) * xv
    o_ref[...] = (yn + bonus) * gate


def _rwkv_params(mu, w0, wB, a0, aB, gB, k_k, k_a, r_k, ln_w, ln_b):
    W = RW_W
    assert mu.shape[0] - 3 * W == SEQ_LANES
    mu4 = jnp.stack([mu[0:W], mu[W:2 * W], mu[2 * W:3 * W], jnp.concatenate([mu[3 * W:], mu[3 * W:]])])
    vec = jnp.stack([w0, a0, k_k, k_a, ln_w, ln_b, r_k.reshape(-1), jnp.zeros_like(w0)])
    wb = jnp.pad(wB, ((0, LANES - wB.shape[0]), (0, 0))).astype(BF16)
    ab = jnp.pad(aB, ((LANES - aB.shape[0], 0), (0, 0))).astype(BF16)
    return mu4, vec, wb, ab, gB.astype(BF16)


HG_HD = 64
HG_SUB = 16
HG_ROWS = 128


def _hgrn_steps(q_ref, f_ref, i_ref, g_ref, lbp_ref, ng_ref, j_ref, tri_ref, tot_ref, o_ref, st_ref,
                *, layer, rows):
    W = q_ref.shape[1]
    TC = rows.stop - rows.start
    npair = W // LANES
    nsub = TC // HG_SUB

    z = f_ref[rows, :]
    sp = _softplus(-z)
    log_f = -sp
    k = _sigmoid(-z)
    if layer > 0:
        lbp = lbp_ref[...]
        e = jnp.exp(lbp - jnp.max(lbp, axis=0, keepdims=True))
        sm = e / jnp.sum(e, axis=0, keepdims=True)
        lb = sm[1:2, :]
        for j in range(2, layer + 1):
            lb = lb + sm[j:j + 1, :]
        lb = jnp.maximum(lb, 0.0)
        t2 = jnp.log(lb) - (sp + z)
        log_f = jnp.maximum(log_f, t2) + jnp.log(1.0 + jnp.exp(-jnp.abs(log_f - t2)))
        k = (1.0 - lb) * k
    q = _silu(q_ref[rows, :])
    v = i_ref[rows, :]
    log_f = log_f * LOG2E
    b = _cumsum_rows(tri_ref[...], log_f)
    bend = _cumsum_rows(tot_ref[...], log_f)
    qe = q * jnp.exp2(b)
    kg = k * jnp.exp2(bend - b)
    jmat = j_ref[...]
    j128 = jmat[0:LANES, 0:LANES]

    rowb = lax.broadcasted_iota(jnp.int32, (TC, LANES), 0)
    trow = lax.broadcasted_iota(jnp.int32, (HG_SUB // 2, W), 0)
    r128 = lax.broadcasted_iota(jnp.int32, (LANES, LANES), 0)
    c128 = lax.broadcasted_iota(jnp.int32, (LANES, LANES), 1)
    bd_mask = (r128 // HG_HD) == (c128 // HG_HD)

    def bf(x):
        return x.astype(BF16)

    vts = [bf(v[:, p * LANES:(p + 1) * LANES].T) for p in range(npair)]
    yield
    outs = []
    H8 = HG_SUB // 2
    for i in range(nsub):
        rs = slice(i * HG_SUB, (i + 1) * HG_SUB)
        r0 = i * HG_SUB
        (b_a, q_a, k_a, v_a), (b_b, q_b, k_b, v_b) = (
            tuple(t[r0 + h * H8:r0 + (h + 1) * H8] for t in (b, q, k, v)) for h in range(2))
        xs = []
        for s in range(H8):
            bs, ks = b_a[s:s + 1, :], k_a[s:s + 1, :]
            xs.append(jnp.where(trow >= s, q_a * (ks * jnp.exp2(b_a - bs)), 0.0))
            xs.append(q_b * (ks * jnp.exp2(b_b - bs)))
        for s in range(H8):
            bs, ks = b_b[s:s + 1, :], k_b[s:s + 1, :]
            xs.append(jnp.where(trow >= s, q_b * (ks * jnp.exp2(b_b - bs)), 0.0))
        x = bf(jnp.concatenate(xs, axis=0))
        g_end = jnp.exp2(bend[i * HG_SUB:i * HG_SUB + 1, :])
        op = []
        for p in range(npair):
            ls = slice(p * LANES, (p + 1) * LANES)
            pm = _dot(x[:, ls], j128)
            od_a = pm[0:H8] * v_a[0:1, ls]
            od_b = pm[H8:HG_SUB] * v_a[0:1, ls]
            for s in range(1, H8):
                od_a = od_a + pm[s * HG_SUB:s * HG_SUB + H8] * v_a[s:s + 1, ls]
                od_b = od_b + pm[s * HG_SUB + H8:(s + 1) * HG_SUB] * v_a[s:s + 1, ls]
            for s in range(H8):
                od_b = od_b + pm[(HG_SUB + s) * H8:(HG_SUB + s + 1) * H8] * v_b[s:s + 1, ls]
            od = jnp.concatenate([od_a, od_b], axis=0)
            st = st_ref[p]
            oi = lax.dot_general(bf(qe[rs, ls]), bf(st), (((1,), (1,)), ((), ())), preferred_element_type=F32)
            kgm = jnp.where((rowb >= i * HG_SUB) & (rowb < (i + 1) * HG_SUB), kg[:, ls], 0.0)
            st_ref[p] = st * g_end[:, ls] + jnp.where(bd_mask, _dot(vts[p], bf(kgm)), 0.0)
            op.append(od + oi)
        outs.append(jnp.concatenate(op, axis=1))
        yield
    o = jnp.concatenate(outs, axis=0)
    ms = _segsum(o * o, jmat) * (1.0 / HG_HD)
    o_ref[rows, :] = o * lax.rsqrt(ms + RMS_EPS) * ng_ref[...] * _silu(g_ref[rows, :])


def _interleave(gens):
    live = list(gens)
    while live:
        for g in list(live):
            try:
                next(g)
            except StopIteration:
                live.remove(g)


def _run_parts(parts, grid, name):
    n_in = [len(p["args"]) for p in parts]
    n_out = [len(p["out_shape"]) for p in parts]
    n_scr = [len(p["scratch"]) for p in parts]

    def take(refs, counts):
        out, pos = [], 0
        for c in counts:
            out.append(refs[pos:pos + c])
            pos += c
        return out

    def kern(*refs):
        ins = take(refs[:sum(n_in)], n_in)
        outs = take(refs[sum(n_in):sum(n_in) + sum(n_out)], n_out)
        scr = take(refs[sum(n_in) + sum(n_out):], n_scr)
        gens = []
        for p, a, b, c in zip(parts, ins, outs, scr):
            gens += p["gens"](a, b, c)
        _interleave(gens)

    cat = lambda key: [x for p in parts for x in p[key]]
    return pl.pallas_call(
        kern, out_shape=cat("out_shape"), grid=grid, in_specs=cat("in_specs"), out_specs=cat("out_specs"),
        scratch_shapes=cat("scratch"), compiler_params=_cparams(("parallel",) * (len(grid) - 1) + ("arbitrary",)),
        name=name,
    )(*cat("args"))


def _seq_part_specs(proj, col_blocks, consts, S):
    T = proj.shape[0]
    nblk = S // MIX_ROWS

    def rows(off, per_half):
        c = off // SEQ_LANES
        return pl.BlockSpec((MIX_ROWS, SEQ_LANES), lambda b, h, i: (b * nblk + i, c + (h if per_half else 0)))

    def const(a, per_half):
        if per_half:
            return pl.BlockSpec(a.shape[:-1] + (SEQ_LANES,), lambda b, h, i: (0,) * (a.ndim - 1) + (h,))
        return pl.BlockSpec(a.shape, lambda b, h, i: (0,) * a.ndim)

    return dict(args=[proj] * len(col_blocks) + [a for a, _ in consts],
                in_specs=[rows(*cb) for cb in col_blocks] + [const(*c) for c in consts],
                out_shape=[jax.ShapeDtypeStruct((T, HG_W), F32)],
                out_specs=[pl.BlockSpec((MIX_ROWS, SEQ_LANES), lambda b, h, i: (b * nblk + i, h))])


def _hgrn_part(proj, lbp, ng, jmat, tri16, tot16, S, layer):
    def gens(ins, outs, scr):
        (st_ref,) = scr

        @pl.when(pl.program_id(2) == 0)
        def _():
            st_ref[...] = jnp.zeros_like(st_ref)

        th = ins[7].shape[0]
        return [itertools.chain(*[_hgrn_steps(*ins, *outs, st_ref, layer=layer, rows=slice(r, r + th))
                                  for r in range(0, MIX_ROWS, th)])]

    cols = [(COL_HG + j * HG_W, True) for j in range(4)]
    consts = [(lbp, True), (ng, True), (jmat, False), (tri16, False), (tot16, False)]
    return dict(_seq_part_specs(proj, cols, consts, S), gens=gens,
                scratch=[pltpu.VMEM((SEQ_LANES // LANES, LANES, LANES), F32)])


def _rwkv_part(proj, mu, vec, wb, ab, gb, jmat, tri64, S):
    def gens(ins, outs, scr):
        carry_ref, st_ref = scr

        @pl.when(pl.program_id(2) == 0)
        def _():
            carry_ref[...] = jnp.zeros_like(carry_ref)
            st_ref[...] = jnp.zeros_like(st_ref)

        return [_rwkv_steps(*ins, *outs, carry_ref, st_ref, nch=MIX_ROWS // RW_CH)]

    cols = [(COL_RW + j * RW_W, True) for j in range(3)] + [(COL_RW + 3 * RW_W, False)]
    consts = [(mu, True), (vec, True), (wb, True), (ab, True), (gb, True), (jmat, False), (tri64, False)]
    return dict(_seq_part_specs(proj, cols, consts, S), gens=gens,
                scratch=[pltpu.VMEM((SUBLANES, SEQ_LANES), F32), pltpu.VMEM((SEQ_LANES // LANES, LANES, LANES), F32)])


NSA_HD = 64
NSA_HEADS = 8
NSA_TQ = 128
CMP_STRIDE = 16
CMP_LEN = 32
SLC_BLOCK = 64
SLC_TOPN = 8
NEG_INF = float("-inf")
V_ROWS = NSA_HD + BF16_ROWS


def _nsa_steps(q_ref, kv1_ref, kv2_ref, gt_ref, w1_ref, pe_ref, w2_ref, biasc_ref, bt_ref, cov_ref, egt_ref,
               o_ref, kc_ref, vct_ref, k_ref, vt_ref, am_ref, m_ref, acc_ref, cx_ref, s_ref, p_ref, al_ref, sn_ref,
               *, bs):
    TQ = NSA_TQ
    S = kv1_ref.shape[0]
    nkc = S // TQ
    ncmp = S // CMP_STRIDE
    nslc = S // SLC_BLOCK
    GW = (NSA_HEADS // 2) * TQ
    i = 2 * pl.program_id(1) + bs
    nq = 2 * pl.num_programs(1)
    q_ref, gt_ref, o_ref = (r.at[bs * TQ:(bs + 1) * TQ] for r in (q_ref, gt_ref, o_ref))
    am_ref, m_ref, acc_ref, s_ref, p_ref, al_ref, sn_ref = (
        r.at[bs] for r in (am_ref, m_ref, acc_ref, s_ref, p_ref, al_ref, sn_ref))
    first_block_only = pl.when(i == 0) if bs == 0 else (lambda f: None)

    def with_ones(vt_g):
        return jnp.concatenate([vt_g, jnp.ones((V_ROWS - NSA_HD, vt_g.shape[1]), F32)], axis=0).astype(BF16)

    @first_block_only
    def _():
        k_ref[0] = kv1_ref[:, 2 * LANES:3 * LANES].astype(BF16)
        k_ref[1] = kv2_ref[:, 0:LANES].astype(BF16)
        for src, (ref, c0) in enumerate(((kv1_ref, 3), (kv2_ref, 1))):
            for c in range(nkc):
                vt = ref[c * TQ:(c + 1) * TQ, c0 * LANES:(c0 + 1) * LANES].T
                for g in range(2):
                    vt_ref[src, g, c] = with_ones(vt[g * NSA_HD:(g + 1) * NSA_HD])
        for t in range(2):
            cx_ref[...] = kv1_ref[:, t * LANES:(t + 1) * LANES]
            acc_a = jnp.zeros((ncmp, 2 * LANES), F32)
            acc_b = jnp.zeros((ncmp, 2 * LANES), F32)
            for l in range(CMP_STRIDE):
                xl = cx_ref[pl.ds(l, ncmp, stride=CMP_STRIDE), :]
                acc_a += _dot((xl + pe_ref[t, l:l + 1, :]).astype(BF16), w1_ref[t, l])
                acc_b += _dot((xl + pe_ref[t, CMP_STRIDE + l:CMP_STRIDE + l + 1, :]).astype(BF16),
                              w1_ref[t, CMP_STRIDE + l])
            hid = _silu(acc_a + pltpu.roll(acc_b, ncmp - 1, axis=0))
            cmp = _dot(hid.astype(BF16), w2_ref[t])
            if t == 0:
                kc_ref[...] = cmp.astype(BF16)
            else:
                ct = cmp.T
                for g in range(2):
                    vct_ref[g] = with_ones(ct[g * NSA_HD:(g + 1) * NSA_HD])

    qs = q_ref[...] * (LOG2E * NSA_HD ** -0.5)
    zero = jnp.zeros((NSA_HD, TQ), F32)
    cols = []
    for pp in range(NSA_HEADS // 2):
        qt = qs[:, pp * LANES:(pp + 1) * LANES].T
        for e in range(2):
            qh = qt[e * NSA_HD:(e + 1) * NSA_HD]
            cols.append(jnp.concatenate([qh, zero] if pp < 2 else [zero, qh], axis=0))
    wq = jnp.concatenate(cols, axis=1).astype(BF16)

    def group_rows(o_t):
        return jnp.concatenate([o_t[:, hh * TQ:(hh + 1) * TQ] for hh in range(NSA_HEADS // 2)], axis=0)

    def raw_scores(br, kc):
        return _dot(k_ref[br, pl.ds(pl.multiple_of(kc * TQ, TQ), TQ), :], wq)

    cpq = TQ // CMP_STRIDE
    sc = _dot(kc_ref[...], wq) + biasc_ref[pl.ds(pl.multiple_of(cpq * (nq - 1 - i), cpq), ncmp), :]
    near = [i - 2, i - 1, i]
    near_kc = [jnp.maximum(kc, 0) for kc in near]
    for t, kc in enumerate(near_kc):
        for br in range(2):
            sn_ref[br, t] = raw_scores(br, kc)
    for u in range(2):
        s_ref[0, u] = raw_scores(0, u)
    yield

    mx = jnp.max(sc, axis=0, keepdims=True)
    ex = jnp.exp2(sc - jnp.where(mx == NEG_INF, 0.0, mx))
    p = ex * (1.0 / jnp.maximum(jnp.sum(ex, axis=0, keepdims=True), 1e-30))
    o_cmp = []
    for g in range(2):
        pg = p[:, g * GW:(g + 1) * GW]
        o_cmp.append(group_rows(_dot(vct_ref[g, 0:NSA_HD, :], pg.astype(BF16))))
        psum = pg[:, 0:TQ]
        for hh in range(1, NSA_HEADS // 2):
            psum = psum + pg[:, hh * TQ:(hh + 1) * TQ]
        hi, lw = _split2(psum)
        imp_t = _dot(cov_ref[...], hi) + _dot(cov_ref[...], lw)
        blk = lax.broadcasted_iota(jnp.int32, (nslc, TQ), 0)
        cur = (i * TQ + lax.broadcasted_iota(jnp.int32, (nslc, TQ), 1)) // SLC_BLOCK
        forced = (blk == 0) | (blk == cur) | (blk == cur - 1)
        score = jnp.where(forced, jnp.inf, jnp.where(blk <= cur, imp_t, NEG_INF))
        cnt = jnp.zeros((nslc, TQ), F32)
        for mp in range(nslc):
            sm = score[mp:mp + 1, :]
            ahead = (sm > score) | ((sm == score) & (blk > mp))
            cnt = cnt + jnp.where(ahead, 1.0, 0.0)
        am_ref[g, 0:nslc] = jnp.where(cnt < SLC_TOPN, 0.0, NEG_INF)
        am_ref[g, nslc:] = jnp.full((am_ref.shape[1] - nslc, TQ), NEG_INF, F32)

    yield

    def reset(br):
        m_ref[br] = jnp.full(m_ref.shape[1:], NEG_INF, F32)
        acc_ref[br] = jnp.zeros(acc_ref.shape[1:], F32)

    def block_mask(r0):
        mk = []
        for g in range(2):
            halves = [jnp.broadcast_to(am_ref[g, pl.ds(r0 + u, 1), :], (SLC_BLOCK, TQ)) for u in range(2)]
            mk.append(jnp.concatenate(halves, axis=0))
        return jnp.concatenate([mk[0]] * (NSA_HEADS // 2) + [mk[1]] * (NSA_HEADS // 2), axis=1)

    def softmax_update(br, ss):
        m_prev = m_ref[br]
        m_new = m_prev
        for s in ss:
            m_new = jnp.maximum(m_new, jnp.max(s, axis=0, keepdims=True))
        m_safe = jnp.where(m_new == NEG_INF, 0.0, m_new)
        m_ref[br] = m_new
        return jnp.exp2(m_prev - m_safe), [jnp.exp2(s - m_safe).astype(BF16) for s in ss]

    def value_update(br, alpha, kcs, pts):
        for g in range(2):
            upd = alpha[:, g * GW:(g + 1) * GW] * acc_ref[br, g]
            for kc, pt in zip(kcs, pts):
                upd = upd + _dot(vt_ref[br, g, kc], pt[:, g * GW:(g + 1) * GW])
            acc_ref[br, g] = upd

    def result(br):
        outs = []
        for g in range(2):
            acc = acc_ref[br, g]
            inv = 1.0 / jnp.maximum(acc[NSA_HD:NSA_HD + 1, :], 1e-30)
            outs.append(group_rows(acc[0:NSA_HD, :] * inv))
        return outs

    reset(1)
    win_tiles = [jnp.where(kc < 0, 0, tile) for kc, tile in zip(near, (4, 2, 1))]
    alpha, pts = softmax_update(1, [sn_ref[1, t] + bt_ref[win_tiles[t]] for t in range(3)])
    value_update(1, alpha, near_kc, pts)
    yield

    reset(0)
    nfar = jnp.maximum(i - 1, 0)
    nfp = nfar // 2
    tail_tiles = [jnp.where((near[0] >= 0) & (nfar - 2 * nfp == 1), 3, 0), jnp.where(near[1] >= 0, 2, 0), 1]
    alpha, pts = softmax_update(0, [sn_ref[0, t] + bt_ref[tail_tiles[t]] + block_mask(2 * near_kc[t])
                                    for t in range(3)])
    value_update(0, alpha, near_kc, pts)
    yield

    def pair_scores(pr, slot):
        for u in range(2):
            s_ref[slot, u] = raw_scores(0, jnp.clip(2 * pr + u, 0, nkc - 1))

    def pair_values(pr, slot):
        value_update(0, al_ref[slot], [jnp.clip(2 * pr + u, 0, nkc - 1) for u in range(2)],
                     [p_ref[slot, u] for u in range(2)])

    def pair_step(pr, slot):
        pair_scores(pr + 1, 1 - slot)
        pair_values(pr - 1, 1 - slot)
        rows = [jnp.where(pr < nfp, 2 * (2 * pr + u), nslc) for u in range(2)]
        alpha, pts = softmax_update(0, [s_ref[slot, u] + block_mask(rows[u]) for u in range(2)])
        al_ref[slot] = alpha
        for u in range(2):
            p_ref[slot, u] = pts[u]

    p_ref[1] = jnp.zeros(p_ref.shape[1:], BF16)
    al_ref[1] = jnp.ones(al_ref.shape[1:], F32)

    def two_pairs(q2, carry):
        pair_step(2 * q2, 0)
        pair_step(2 * q2 + 1, 1)
        return carry

    ntrip = (nfp + 1) // 2
    lax.fori_loop(0, ntrip, two_pairs, 0)
    yield
    pair_values(2 * ntrip - 1, 1)

    o_slc = result(0)
    o_win = result(1)

    sig_hi, sig_lo = _split2(_sigmoid(gt_ref[...]).T)
    out_t = None
    for j, parts in enumerate((o_cmp, o_slc, o_win)):
        gate_t = _dot(egt_ref[j], sig_hi) + _dot(egt_ref[j], sig_lo)
        term = gate_t * jnp.concatenate(parts, axis=0)
        out_t = term if out_t is None else out_t + term
    o_ref[...] = jnp.concatenate([out_t[c * LANES:(c + 1) * LANES].T for c in range(NSA_W // LANES)], axis=1)


def _nsa_part(proj, w1b, peb, w2b, tabs, S):
    T = proj.shape[0]
    W = NSA_W
    TQ = NSA_TQ
    nq = S // TQ
    HW = NSA_HEADS * TQ
    c0 = COL_NSA // W
    cg = (COL_NSA + 2 * W + 2 * LANES) // LANES
    biasc, bt, cov, egt = tabs

    def const(shape):
        return pl.BlockSpec(shape, lambda b, i, _n=len(shape): (0,) * _n)

    nstep = nq // 2

    def gens(ins, outs, scr):
        return [_nsa_steps(*ins, *outs, *scr, bs=bs) for bs in range(2)]

    return dict(
        args=[proj, proj, proj, proj, w1b, peb, w2b, biasc, bt, cov, egt], gens=gens,
        out_shape=[jax.ShapeDtypeStruct((T, W), F32)],
        in_specs=[
            pl.BlockSpec((2 * TQ, W), lambda b, i: (b * nstep + i, c0)),
            pl.BlockSpec((S, W), lambda b, i: (b, c0 + 1)),
            pl.BlockSpec((S, 2 * LANES), lambda b, i: (b, (COL_NSA + 2 * W) // (2 * LANES))),
            pl.BlockSpec((2 * TQ, LANES), lambda b, i: (b * nstep + i, cg)),
            const(w1b.shape), const(peb.shape), const(w2b.shape),
            const(biasc.shape), const(bt.shape), const(cov.shape), const(egt.shape),
        ],
        out_specs=[pl.BlockSpec((2 * TQ, W), lambda b, i: (b * nstep + i, 0))],
        scratch=[
            pltpu.VMEM((S // CMP_STRIDE, LANES), BF16),
            pltpu.VMEM((2, V_ROWS, S // CMP_STRIDE), BF16),
            pltpu.VMEM((2, S, LANES), BF16),
            pltpu.VMEM((2, 2, S // TQ, V_ROWS, TQ), BF16),
            pltpu.VMEM((2, 2, S // SLC_BLOCK + SUBLANES, TQ), F32),
            pltpu.VMEM((2, 2, 1, HW), F32),
            pltpu.VMEM((2, 2, 2, V_ROWS, HW // 2), F32),
            pltpu.VMEM((S, LANES), F32),
            pltpu.VMEM((2, 2, 2, TQ, HW), F32),
            pltpu.VMEM((2, 2, 2, TQ, HW), BF16),
            pltpu.VMEM((2, 2, 1, HW), F32),
            pltpu.VMEM((2, 2, 3, TQ, HW), F32),
        ])


def _blockdiag2(w):
    z = jnp.zeros_like(w)
    return jnp.concatenate([jnp.concatenate([w, z], axis=-1), jnp.concatenate([z, w], axis=-1)], axis=-2)


def _nsa_params(pe, w1, w2):
    L, Dh = pe.shape[1], pe.shape[2]
    w1b = _blockdiag2(w1.reshape(2, L, Dh, -1)).astype(BF16)
    w2b = _blockdiag2(w2).astype(BF16)
    peb = jnp.concatenate([pe, pe], axis=-1)
    return w1b, peb, w2b


def _t5_bucket_np(n):
    n = np.maximum(n, 0)
    nf = np.maximum(n, 1).astype(np.float64)
    large = 16 + (np.log(nf / 16.0) / math.log(128 / 16) * 16).astype(np.int64)
    return np.where(n < 16, n, np.minimum(large, 31))


def _bias_table_kernel(rb_ref, idx_ref, add_ref, shift_ref, o_ref):
    nb, nh = rb_ref.shape
    idx = idx_ref[...]
    acc = [jnp.zeros(idx.shape, F32) for _ in range(nh)]
    for b in range(nb):
        hit = idx == b
        acc = [jnp.where(hit, rb_ref[b, h], a) for h, a in enumerate(acc)]
    for h in range(nh):
        o_ref[:, h * NSA_TQ:(h + 1) * NSA_TQ] = (acc[h] - shift_ref[...] * rb_ref[nb - 1, h]) * LOG2E + add_ref[...]


def _nsa_tables(rel_bias, S):
    TQ = NSA_TQ
    nq = S // TQ
    tiles = []

    def tile(dist, valid, shift):
        tiles.append((_t5_bucket_np(dist), np.where(valid, 0.0, -np.inf), np.full(dist.shape, float(shift))))

    ncmp = S // CMP_STRIDE
    end = np.arange(ncmp) * CMP_STRIDE + CMP_LEN - 1
    t = np.arange(TQ)[None, :]
    j = np.arange(TQ)[:, None]
    every = np.ones((TQ, TQ), bool)
    tile(t - j, ~every, True)
    tile(t - j, t >= j, True)
    tile(t - j + TQ, every, True)
    tile(t - j + 2 * TQ, every, True)
    tile(t - j + 2 * TQ, j > t, True)
    assert _t5_bucket_np(np.asarray(TQ + 1)) == _t5_bucket_np(np.asarray(S)) == rel_bias.shape[0] - 1
    cpq = TQ // CMP_STRIDE
    rel_blk = np.arange(cpq * (nq - 1) + ncmp) - cpq * (nq - 1)
    dist_c = np.arange(TQ)[None, :] - (rel_blk[:, None] * CMP_STRIDE + CMP_LEN - 1)
    tile(dist_c, dist_c >= 0, False)
    idx, add, shift = (np.concatenate([x[k] for x in tiles], axis=0) for k in range(3))
    full = lambda a: pl.BlockSpec(a.shape, lambda: (0,) * a.ndim)
    table = pl.pallas_call(
        _bias_table_kernel,
        out_shape=jax.ShapeDtypeStruct((idx.shape[0], NSA_HEADS * TQ), F32),
        in_specs=[pl.BlockSpec(memory_space=pltpu.SMEM), full(idx), full(add), full(shift)],
        out_specs=pl.BlockSpec((idx.shape[0], NSA_HEADS * TQ), lambda: (0, 0)),
        name="bias_tables",
    )(rel_bias.astype(F32), jnp.asarray(idx, jnp.int32), jnp.asarray(add, F32), jnp.asarray(shift, F32))
    bt = table[:5 * TQ].reshape(5, TQ, NSA_HEADS * TQ)
    biasc = table[5 * TQ:]
    nslc = S // SLC_BLOCK
    s_lo = np.arange(nslc) * SLC_BLOCK
    start = np.arange(ncmp) * CMP_STRIDE
    cover = ((start[:, None] <= (s_lo + SLC_BLOCK - 1)[None, :]) & (end[:, None] >= s_lo[None, :]))
    cover &= (np.arange(ncmp) < ncmp - 1)[:, None]
    egt = np.zeros((3, NSA_W, LANES), np.float32)
    for jj in range(3):
        egt[jj, np.arange(NSA_W), (np.arange(NSA_W) // NSA_HD) * 3 + jj] = 1.0
    return biasc, bt, jnp.asarray(cover.T.astype(np.float32), BF16), jnp.asarray(egt, BF16)


def _tiles(T, n_ff):
    tm = 512 if T % 512 == 0 else T
    tm_proj = 2 * tm if T % (2 * tm) == 0 else tm
    tf = n_ff // 2 if (n_ff // 2) % LANES == 0 else n_ff
    tn = N_PROJ // 4
    return tm, tm_proj, tf, tn


def _pack_w_in(w):
    hg_end = 4 * HG_W
    nsa_end = hg_end + NSA_W + 6 * 2 * NSA_HD + 3 * 8
    rw_end = nsa_end + 3 * RW_W + 256
    out = jnp.zeros(w.shape[:-1] + (N_PROJ,), BF16)
    for dst, lo, hi in ((COL_MG, rw_end, w.shape[-1]), (COL_HG, 0, hg_end), (COL_NSA, hg_end, nsa_end),
                        (COL_RW, nsa_end, rw_end)):
        out = out.at[..., dst:dst + hi - lo].set(w[..., lo:hi].astype(BF16))
    return out


def kernel(x, p, ffn1_norm, ffn1_wgu, ffn1_wd, mix_norm, w_in, hg_lb, hg_norm, cmp_pe, cmp_w1, cmp_w2, rel_bias, rw_mu, rw_w0, rw_wB, rw_a0, rw_aB, rw_gB, rw_kk, rw_ka, rw_rk, rw_ln_w, rw_ln_b, w_branch, w_out, ffn2_norm, ffn2_wgu, ffn2_wd, ple_norm, ple_gate_w, ple_w, final_norm):
    B, S, D = x.shape
    depth = ffn1_norm.shape[0]
    T = B * S
    assert D == D_MODEL and S // CMP_STRIDE == LANES and S % NSA_TQ == 0
    assert w_in.shape[2] - 3 * D_MODEL == 4 * HG_W + NSA_W + 12 * NSA_HD + 24 + 3 * RW_W + 256
    tm, tm_proj, tf, tn = _tiles(T, ffn1_wd.shape[1])
    consts = _consts()
    tabs = _nsa_tables(rel_bias, S)
    row = lambda v: v.reshape(1, -1)
    wgu1, wd1, wgu2, wd2 = (w.astype(BF16) for w in (ffn1_wgu, ffn1_wd, ffn2_wgu, ffn2_wd))
    w_proj = _pack_w_in(w_in)
    wb, wo, wpg, wpp = (w.astype(BF16) for w in (w_branch, w_out, ple_gate_w, ple_w))
    p_rows = p.reshape(depth, T, -1)
    h = x.reshape(T, D)
    for i in range(depth):
        h = _ffn(h, row(ffn1_norm[i]), wgu1, wd1, i, tm, tf)
        proj = _proj(h, row(mix_norm[i]), w_proj, i, tm_proj, tn)
        nsa_part = _nsa_part(proj, *_nsa_params(cmp_pe[i], cmp_w1[i], cmp_w2[i]), tabs, S)
        hg_part = _hgrn_part(proj, hg_lb, row(hg_norm[i]), consts["j512"], consts["tri16"], consts["tot16"], S, i)
        rw_part = _rwkv_part(
            proj, *_rwkv_params(rw_mu[i], rw_w0[i], rw_wB[i], rw_a0[i], rw_aB[i], rw_gB[i], rw_kk[i], rw_ka[i],
                                rw_rk[i], rw_ln_w[i], rw_ln_b[i]), consts["j512"], consts["tri64"], S)
        (o_ns,) = _run_parts([nsa_part], (B, S // (2 * NSA_TQ)), "nsa")
        o_hg, o_rw = _run_parts([hg_part, rw_part], (B, HG_W // SEQ_LANES, S // MIX_ROWS), "hgrn2_rwkv7")
        h = _merge(h, proj, o_hg, o_ns, o_rw, wb, wo, i, tm)
        h = _ffn(h, row(ffn2_norm[i]), wgu2, wd2, i, tm, tf,
                 ple=(row(ple_norm[i]), wpg, p_rows, wpp, row(final_norm)), final=i == depth - 1)
    return h.reshape(B, S, D)
```
